```python
import jax, jax.numpy as jnp
from jax import lax
import numpy as np


D_MODEL = 2048
BATCH = 4
SEQ = 4096
DEPTH = 2

PLE_DIM = 256
EPS = 1e-6
NEG = -1e30
RET_DV = 256
RET_DK = 128
RET_HEADS = (D_MODEL // 2) // RET_DV
RET_CHUNK = 128
ATT_HD = 128
ATT_HEADS = (D_MODEL // 2) // ATT_HD
DILATED_PATTERNS = ((128, 1), (512, 4), (2048, 16))
ATT_BLOCK = 128
RET_WIDTH = RET_HEADS * RET_DV
ATT_WIDTH = ATT_HEADS * ATT_HD
MIX_WIDTH_EVEN = RET_WIDTH + ATT_WIDTH
IN_WIDTH_EVEN = 2 * RET_HEADS * RET_DK + RET_WIDTH + 3 * ATT_WIDTH + MIX_WIDTH_EVEN
CONV_WIDTH = 3
CONV_CH = D_MODEL
IN_WIDTH_ODD = 4 * CONV_CH
N_EVEN = (DEPTH + 1) // 2
N_ODD = DEPTH // 2

kernel_name = 'hybrid_retention_dilated_attn_shortconv'


def rms_norm(x, g):
    x32 = x.astype(jnp.float32)
    y = x32 * lax.rsqrt(jnp.mean(x32 * x32, axis=-1, keepdims=True) + EPS)
    return (y * g.astype(jnp.float32)).astype(x.dtype)


def alibi_slopes(n):
    start = 2.0 ** (-8.0 / n)
    return jnp.asarray(start ** np.arange(1, n + 1), dtype=jnp.float32)


def retention(q, k, v, gn_g):
    B, T, H, dk = q.shape
    dv = v.shape[-1]
    C = RET_CHUNK
    N = T // C
    f32 = jnp.float32
    log_g = jnp.asarray(np.log(1.0 - 2.0 ** (-5.0 - np.arange(H))), dtype=f32)
    pos = jnp.arange(C, dtype=f32)
    diff = pos[:, None] - pos[None, :]
    decay = jnp.where(diff >= 0, jnp.exp(log_g[:, None, None] * jnp.maximum(diff, 0.0)), 0.0)
    xi = jnp.exp(log_g[:, None] * (pos + 1.0))
    zeta = jnp.exp(log_g[:, None] * (C - 1.0 - pos))
    g_chunk = jnp.exp(log_g * C)
    qc = q.astype(f32).reshape(B, N, C, H, dk)
    kc = (k.astype(f32) * dk ** -0.5).reshape(B, N, C, H, dk)
    vc = v.astype(f32).reshape(B, N, C, H, dv)
    scores = jnp.einsum('bnchk,bnshk->bnhcs', qc, kc) * decay
    y_inner = jnp.einsum('bnhcs,bnshv->bnchv', scores, vc)
    contrib = jnp.einsum('bnshk,bnshv,hs->nbhkv', kc, vc, zeta)

    def step(R, S):
        return g_chunk[None, :, None, None] * R + S, R

    _, R_prev = lax.scan(step, jnp.zeros((B, H, dk, dv), f32), contrib)
    y_cross = jnp.einsum('bnchk,nbhkv,hc->bnchv', qc, R_prev, xi)
    y = (y_inner + y_cross).reshape(B, T, H, dv)
    mu = jnp.mean(y, axis=-1, keepdims=True)
    var = jnp.mean(jnp.square(y - mu), axis=-1, keepdims=True)
    y = (y - mu) * lax.rsqrt(var + EPS) * gn_g.astype(f32)
    return y.reshape(B, T, H * dv).astype(v.dtype)


def dilated_branch(q, k, v, window, dil, slopes):
    B, H, T, hd = q.shape
    steps = window // dil
    assert steps <= ATT_BLOCK
    L = T // dil
    nb = -(-L // ATT_BLOCK)
    Lp = nb * ATT_BLOCK

    def gather(a):
        a = a.reshape(B, H, L, dil, hd).transpose(0, 1, 3, 2, 4)
        a = jnp.pad(a, ((0, 0), (0, 0), (0, 0), (0, Lp - L), (0, 0)))
        return a.reshape(B, H, dil, nb, ATT_BLOCK, hd)

    def band(a):
        prev = jnp.pad(a[:, :, :, :-1], ((0, 0), (0, 0), (0, 0), (1, 0), (0, 0), (0, 0)))
        return jnp.concatenate([prev, a], axis=-2)

    qb = gather(q)
    kband = band(gather(k))
    vband = band(gather(v))
    s = jnp.einsum('bhrnqd,bhrnkd->bhrnqk', qb, kband)
    qi = jnp.arange(ATT_BLOCK)[:, None]
    ki = jnp.arange(2 * ATT_BLOCK)[None, :]
    rel = qi - ki + ATT_BLOCK
    blk = jnp.arange(nb)[:, None, None]
    valid = (rel >= 0) & (rel <= steps) & ((blk > 0) | (ki >= ATT_BLOCK))
    bias = -slopes[:, None, None] * (rel * dil).astype(jnp.float32)[None]
    s = jnp.where(valid[None, None, None], s + bias[None, :, None, None], NEG)
    m = jnp.max(s, axis=-1, keepdims=True)
    pr = jnp.exp(s - m)
    l = jnp.sum(pr, axis=-1)
    o = jnp.einsum('bhrnqk,bhrnkd->bhrnqd', pr, vband) / l[..., None]
    lse = m[..., 0] + jnp.log(l)
    o = o.reshape(B, H, dil, Lp, hd)[:, :, :, :L].transpose(0, 1, 3, 2, 4).reshape(B, H, T, hd)
    lse = lse.reshape(B, H, dil, Lp)[:, :, :, :L].transpose(0, 1, 3, 2).reshape(B, H, T)
    return o, lse


def dilated_attention(q, k, v, slopes):
    outs, lses = [], []
    for window, dil in DILATED_PATTERNS:
        o, lse = dilated_branch(q, k, v, window, dil, slopes)
        outs.append(o)
        lses.append(lse)
    w = jax.nn.softmax(jnp.stack(lses), axis=0)
    return jnp.einsum('gbht,gbhtd->bhtd', w, jnp.stack(outs))


def even_layer(h, w_in, q_g, k_g, gn_g, w_out, slopes):
    B, T, _ = h.shape
    rk_w = RET_HEADS * RET_DK
    offs = [rk_w, 2 * rk_w, 2 * rk_w + RET_WIDTH, 2 * rk_w + RET_WIDTH + ATT_WIDTH,
            2 * rk_w + RET_WIDTH + 2 * ATT_WIDTH, 2 * rk_w + RET_WIDTH + 3 * ATT_WIDTH]
    rq, rk, rv, aq, ak, av, z = jnp.split(h @ w_in, offs, axis=-1)
    y_ret = retention(rq.reshape(B, T, RET_HEADS, RET_DK), rk.reshape(B, T, RET_HEADS, RET_DK),
                      rv.reshape(B, T, RET_HEADS, RET_DV), gn_g)
    aq = rms_norm(aq.reshape(B, T, ATT_HEADS, ATT_HD), q_g)
    ak = rms_norm(ak.reshape(B, T, ATT_HEADS, ATT_HD), k_g)
    to_bhtd = lambda a: a.reshape(B, T, ATT_HEADS, ATT_HD).transpose(0, 2, 1, 3).astype(jnp.float32)
    y_att = dilated_attention(to_bhtd(aq) * ATT_HD ** -0.5, to_bhtd(ak), to_bhtd(av), slopes)
    y_att = y_att.transpose(0, 2, 1, 3).reshape(B, T, ATT_WIDTH).astype(h.dtype)
    y = jnp.concatenate([y_ret, y_att], axis=-1) * jax.nn.silu(z)
    return y @ w_out


def odd_layer(h, w_in, conv_w, w_out):
    bg, cg, u, z = jnp.split(h @ w_in, 4, axis=-1)
    u = cg * u
    T = u.shape[1]
    up = jnp.pad(u, ((0, 0), (CONV_WIDTH - 1, 0), (0, 0)))
    conv = sum(conv_w[j] * up[:, j:j + T] for j in range(CONV_WIDTH))
    return (bg * conv * jax.nn.silu(z)) @ w_out


def setup_inputs(seed: int = 0) -> dict:
    key = jax.random.key(seed)
    ks = jax.random.split(key, 14)
    f32 = jnp.float32

    def normal(k, shape, scale):
        return jax.random.normal(k, shape, f32) * scale

    def gain(k, shape):
        return 1.0 + 0.02 * jax.random.normal(k, shape, f32)

    return {
        'x': normal(ks[0], (BATCH, SEQ, D_MODEL), 1.0),
        'p': normal(ks[1], (DEPTH, BATCH, SEQ, PLE_DIM), 1.0),
        'pre_norm_g': gain(ks[2], (DEPTH, D_MODEL)),
        'w_in_even': normal(ks[3], (N_EVEN, D_MODEL, IN_WIDTH_EVEN), D_MODEL ** -0.5),
        'q_norm_g': gain(ks[4], (N_EVEN, ATT_HD)),
        'k_norm_g': gain(ks[5], (N_EVEN, ATT_HD)),
        'ret_gn_g': gain(ks[6], (N_EVEN, RET_HEADS, RET_DV)),
        'w_out_even': normal(ks[7], (N_EVEN, MIX_WIDTH_EVEN, D_MODEL), MIX_WIDTH_EVEN ** -0.5),
        'w_in_odd': normal(ks[8], (N_ODD, D_MODEL, IN_WIDTH_ODD), D_MODEL ** -0.5),
        'conv_w_odd': normal(ks[9], (N_ODD, CONV_WIDTH, CONV_CH), CONV_WIDTH ** -0.5),
        'w_out_odd': normal(ks[10], (N_ODD, CONV_CH, D_MODEL), CONV_CH ** -0.5),
        'ple_norm_g': gain(ks[11], (DEPTH, D_MODEL)),
        'w_ple_gate': normal(ks[12], (DEPTH, D_MODEL, D_MODEL), D_MODEL ** -0.5),
        'w_ple_proj': normal(ks[13], (DEPTH, PLE_DIM, D_MODEL), PLE_DIM ** -0.5),
    }


def reference(x, p, pre_norm_g, w_in_even, q_norm_g, k_norm_g, ret_gn_g, w_out_even,
              w_in_odd, conv_w_odd, w_out_odd, ple_norm_g, w_ple_gate, w_ple_proj):
    slopes = alibi_slopes(ATT_HEADS)
    for i in range(DEPTH):
        h = rms_norm(x, pre_norm_g[i])
        if i % 2 == 0:
            j = i // 2
            mix = even_layer(h, w_in_even[j], q_norm_g[j], k_norm_g[j], ret_gn_g[j], w_out_even[j], slopes)
        else:
            j = i // 2
            mix = odd_layer(h, w_in_odd[j], conv_w_odd[j], w_out_odd[j])
        x = x + mix
        gate = jax.nn.sigmoid(rms_norm(x, ple_norm_g[i]) @ w_ple_gate[i])
        x = x + gate * (p[i] @ w_ple_proj[i])
    return x
```

```python
import functools
import math

import jax
import jax.numpy as jnp
from jax import lax
from jax.experimental import pallas as pl
from jax.experimental.pallas import tpu as pltpu

F32 = jnp.float32
BF16 = jnp.bfloat16

D_MODEL = 2048
PLE_DIM = 256
EPS = 1e-6
NEG = -1e30
RET_DK = 128
RET_DV = 256
RET_HEADS = 4
RET_CHUNK = 128
ATT_HD = 128
ATT_HEADS = 8
ATT_BLOCK = 128
DILATED_PATTERNS = ((128, 1), (512, 4), (2048, 16))
RET_QK_WIDTH = RET_HEADS * RET_DK
RET_WIDTH = RET_HEADS * RET_DV
ATT_WIDTH = ATT_HEADS * ATT_HD
IN_WIDTH_EVEN = 2 * RET_QK_WIDTH + RET_WIDTH + 3 * ATT_WIDTH + RET_WIDTH + ATT_WIDTH
CONV_WIDTH = 3

OFF_RQ = 0
OFF_RK = RET_QK_WIDTH
OFF_RV = 2 * RET_QK_WIDTH
OFF_AQ = OFF_RV + RET_WIDTH
OFF_AK = OFF_AQ + ATT_WIDTH
OFF_AV = OFF_AK + ATT_WIDTH
OFF_Z = OFF_AV + ATT_WIDTH

VMEM_LIMIT_BYTES = 56 * 1024 * 1024

IN_TM = 1024
IN_EVEN_TN = 1024
IN_ODD_TC = 256
RET_TQ = 1024
POST_TM = 256
POST_NC = 512
HALO = 8


def _rms_rows(x, g):
    ms = jnp.mean(x * x, axis=-1, keepdims=True)
    return x * lax.rsqrt(ms + EPS) * g


def _silu(z):
    return z * jax.nn.sigmoid(z)


def _in_even_kernel(x_ref, g_ref, w_ref, gain_ref, o_ref, hn_ref):
    j = pl.program_id(1)

    @pl.when(j == 0)
    def _():
        hn_ref[...] = _rms_rows(x_ref[...], g_ref[...]).astype(BF16)

    acc = jnp.dot(hn_ref[...], w_ref[...], preferred_element_type=F32)
    tn = acc.shape[1]
    j_aq = OFF_AQ // tn
    j_av = OFF_AV // tn
    j_z = OFF_Z // tn

    @pl.when((j < j_aq) | (j == j_av))
    def _():
        o_ref[...] = acc.astype(BF16)

    @pl.when((j >= j_aq) & (j < j_av))
    def _():
        for hh in range(tn // ATT_HD):
            sl = slice(hh * ATT_HD, (hh + 1) * ATT_HD)
            o_ref[:, sl] = _rms_rows(acc[:, sl], gain_ref[:, sl]).astype(BF16)

    @pl.when(j >= j_z)
    def _():
        o_ref[...] = _silu(acc).astype(BF16)


def _in_even(x2, g, w_bf, gain_row):
    m = x2.shape[0]
    tm, tn = IN_TM, IN_EVEN_TN
    return pl.pallas_call(
        _in_even_kernel,
        out_shape=jax.ShapeDtypeStruct((m, IN_WIDTH_EVEN), BF16),
        grid=(m // tm, IN_WIDTH_EVEN // tn),
        in_specs=[
            pl.BlockSpec((tm, D_MODEL), lambda i, j: (i, 0)),
            pl.BlockSpec((1, D_MODEL), lambda i, j: (0, 0)),
            pl.BlockSpec((D_MODEL, tn), lambda i, j: (0, j)),
            pl.BlockSpec((1, tn), lambda i, j: (0, j)),
        ],
        out_specs=pl.BlockSpec((tm, tn), lambda i, j: (i, j)),
        scratch_shapes=[pltpu.VMEM((tm, D_MODEL), BF16)],
        compiler_params=pltpu.CompilerParams(
            dimension_semantics=("arbitrary", "arbitrary"), vmem_limit_bytes=VMEM_LIMIT_BYTES),
        name="in_even",
    )(x2, g, w_bf, gain_row)


def _ret_kernel(q_ref, k_ref, v_ref, z_ref, gn_ref, o_ref, r_ref):
    c = RET_CHUNK
    tq = q_ref.shape[0]

    @pl.when(pl.program_id(1) == 0)
    def _():
        r_ref[...] = jnp.zeros(r_ref.shape, F32)

    row = lax.broadcasted_iota(jnp.int32, (c, c), 0)
    col = lax.broadcasted_iota(jnp.int32, (c, c), 1)
    diff = (row - col).astype(F32)
    pos = lax.broadcasted_iota(jnp.int32, (c, 1), 0).astype(F32)
    kscale = RET_DK ** -0.5

    for h in range(RET_HEADS):
        lg = math.log(1.0 - 2.0 ** (-5.0 - h))
        decay = jnp.where(diff >= 0, jnp.exp(lg * jnp.maximum(diff, 0.0)), 0.0) * kscale
        xi = jnp.exp(lg * (pos + 1.0))
        zeta = jnp.exp(lg * (c - 1.0 - pos)) * kscale
        g_chunk = math.exp(lg * c)
        qk_sl = slice(h * RET_DK, (h + 1) * RET_DK)
        v_sl = slice(h * RET_DV, (h + 1) * RET_DV)
        gn = gn_ref[h:h + 1, :]

        def chunk(ci, carry, decay=decay, xi=xi, zeta=zeta, g_chunk=g_chunk, qk_sl=qk_sl, v_sl=v_sl,
                  gn=gn, h=h):
            rows = pl.ds(pl.multiple_of(ci * c, c), c)
            q = q_ref[rows, qk_sl]
            k = k_ref[rows, qk_sl]
            v = v_ref[rows, v_sl]
            s = lax.dot_general(q, k, (((1,), (1,)), ((), ())), preferred_element_type=F32) * decay
            r_prev = r_ref[h]
            y = jnp.dot(s.astype(BF16), v, preferred_element_type=F32)
            y = y + jnp.dot(q, r_prev.astype(BF16), preferred_element_type=F32) * xi
            kz_t = (k.astype(F32) * zeta).T.astype(BF16)
            r_ref[h] = g_chunk * r_prev + jnp.dot(kz_t, v, preferred_element_type=F32)
            mu = jnp.mean(y, axis=-1, keepdims=True)
            yc = y - mu
            var = jnp.mean(yc * yc, axis=-1, keepdims=True)
            yn = yc * lax.rsqrt(var + EPS) * gn
            o_ref[rows, v_sl] = (yn * z_ref[rows, v_sl].astype(F32)).astype(BF16)
            return carry

        lax.fori_loop(0, tq // c, chunk, 0)


def _retention(proj, gn_g, batch, seq):
    m = proj.shape[0]
    tq = RET_TQ
    nt = seq // tq
    row = lambda b, t: b * nt + t
    return pl.pallas_call(
        _ret_kernel,
        out_shape=jax.ShapeDtypeStruct((m, RET_WIDTH), BF16),
        grid=(batch, nt),
        in_specs=[
            pl.BlockSpec((tq, RET_QK_WIDTH), lambda b, t: (row(b, t), OFF_RQ // RET_QK_WIDTH)),
            pl.BlockSpec((tq, RET_QK_WIDTH), lambda b, t: (row(b, t), OFF_RK // RET_QK_WIDTH)),
            pl.BlockSpec((tq, RET_WIDTH), lambda b, t: (row(b, t), OFF_RV // RET_WIDTH)),
            pl.BlockSpec((tq, RET_WIDTH), lambda b, t: (row(b, t), OFF_Z // RET_WIDTH)),
            pl.BlockSpec((RET_HEADS, RET_DV), lambda b, t: (0, 0)),
        ],
        out_specs=pl.BlockSpec((tq, RET_WIDTH), lambda b, t: (row(b, t), 0)),
        scratch_shapes=[pltpu.VMEM((RET_HEADS, RET_DK, RET_DV), F32)],
        compiler_params=pltpu.CompilerParams(
            dimension_semantics=("arbitrary", "arbitrary"), vmem_limit_bytes=VMEM_LIMIT_BYTES),
        name="retention",
    )(proj, proj, proj, proj, gn_g)


def _att_kernel(slopes_ref, q_ref, k_ref, v_ref, z_ref, o_ref, qf, kf, vf, o0, o1, o2, l0, l1, l2):
    blk = ATT_BLOCK
    seq = q_ref.shape[0]
    slope = slopes_ref[pl.program_id(1)]
    qf[...] = q_ref[...].astype(F32)
    kf[...] = k_ref[...].astype(F32)
    vf[...] = v_ref[...].astype(F32)

    qi = lax.broadcasted_iota(jnp.int32, (blk, blk), 0)
    ki = lax.broadcasted_iota(jnp.int32, (blk, blk), 1)
    d_cur = qi - ki
    d_prev = d_cur + blk
    nt = (((1,), (1,)), ((), ()))

    for (window, dil), o_g, l_g in zip(DILATED_PATTERNS, (o0, o1, o2), (l0, l1, l2)):
        steps = window // dil
        nb = seq // dil // blk
        bias_cur = jnp.where(d_cur >= 0, -slope * (d_cur * dil).astype(F32), NEG)
        bias_prev = jnp.where(d_prev <= steps, -slope * (d_prev * dil).astype(F32), NEG)

        def rows_at(start, dil=dil):
            if dil == 1:
                return pl.ds(start, blk)
            return pl.ds(start, blk, stride=dil)

        def body(bi, carry, dil=dil, nb=nb, bias_cur=bias_cur, bias_prev=bias_prev, o_g=o_g, l_g=l_g,
                 rows_at=rows_at):
            r = bi // nb
            n = bi % nb
            start = r + n * (blk * dil)
            pstart = jnp.maximum(start - blk * dil, 0)
            q = qf[rows_at(start), :].astype(BF16)
            kc = kf[rows_at(start), :].astype(BF16)
            vc = vf[rows_at(start), :].astype(BF16)
            kp = kf[rows_at(pstart), :].astype(BF16)
            vp = vf[rows_at(pstart), :].astype(BF16)
            s_cur = lax.dot_general(q, kc, nt, preferred_element_type=F32) + bias_cur
            s_prev = lax.dot_general(q, kp, nt, preferred_element_type=F32) + bias_prev
            s_prev = s_prev + jnp.where(n > 0, 0.0, NEG)
            mx = jnp.maximum(jnp.max(s_cur, axis=-1, keepdims=True), jnp.max(s_prev, axis=-1, keepdims=True))
            p_cur = jnp.exp(s_cur - mx)
            p_prev = jnp.exp(s_prev - mx)
            den = jnp.sum(p_cur, axis=-1, keepdims=True) + jnp.sum(p_prev, axis=-1, keepdims=True)
            acc = jnp.dot(p_cur.astype(BF16), vc, preferred_element_type=F32)
            acc = acc + jnp.dot(p_prev.astype(BF16), vp, preferred_element_type=F32)
            o_g[rows_at(start), :] = acc / den
            l_g[rows_at(start), :] = jnp.broadcast_to(mx + jnp.log(den), (blk, ATT_HD))
            return carry

        lax.fori_loop(0, seq // blk, body, 0)

    mrows = 256

    def merge(ci, carry):
        rows = pl.ds(pl.multiple_of(ci * mrows, mrows), mrows)
        a0, a1, a2 = l0[rows, :], l1[rows, :], l2[rows, :]
        mx = jnp.maximum(jnp.maximum(a0, a1), a2)
        e0, e1, e2 = jnp.exp(a0 - mx), jnp.exp(a1 - mx), jnp.exp(a2 - mx)
        y = (e0 * o0[rows, :] + e1 * o1[rows, :] + e2 * o2[rows, :]) / (e0 + e1 + e2)
        o_ref[rows, :] = (y * z_ref[rows, :].astype(F32)).astype(BF16)
        return carry

    lax.fori_loop(0, seq // mrows, merge, 0)


def _attention(proj, slopes, batch, seq):
    m = proj.shape[0]
    hd = ATT_HD
    spec = lambda off: pl.BlockSpec((seq, hd), lambda b, h: (b, off // hd + h))
    return pl.pallas_call(
        _att_kernel,
        out_shape=jax.ShapeDtypeStruct((m, ATT_WIDTH), BF16),
        grid=(batch, ATT_HEADS),
        in_specs=[
            pl.BlockSpec(memory_space=pltpu.SMEM),
            spec(OFF_AQ), spec(OFF_AK), spec(OFF_AV), spec(OFF_Z + RET_WIDTH),
        ],
        out_specs=pl.BlockSpec((seq, hd), lambda b, h: (b, h)),
        scratch_shapes=[pltpu.VMEM((seq, hd), F32) for _ in range(9)],
        compiler_params=pltpu.CompilerParams(
            dimension_semantics=("arbitrary", "arbitrary"), vmem_limit_bytes=VMEM_LIMIT_BYTES),
        name="dilated_attention",
    )(slopes, proj, proj, proj, proj)


def _post_tail(x1, p_ref, pg_ref, wg_ref, wp_ref, o_ref):
    h2 = _rms_rows(x1, pg_ref[...]).astype(BF16)
    pb = p_ref[...].astype(BF16)
    for cc in range(D_MODEL // POST_NC):
        sl = slice(cc * POST_NC, (cc + 1) * POST_NC)
        gate = jax.nn.sigmoid(jnp.dot(h2, wg_ref[:, sl], preferred_element_type=F32))
        pp = jnp.dot(pb, wp_ref[:, sl], preferred_element_type=F32)
        o_ref[:, sl] = x1[:, sl] + gate * pp


def _post_even_kernel(x_ref, ya_ref, yb_ref, p_ref, wa_ref, wb_ref, pg_ref, wg_ref, wp_ref, o_ref):
    x1 = x_ref[...] + jnp.dot(ya_ref[...], wa_ref[...], preferred_element_type=F32)
    x1 = x1 + jnp.dot(yb_ref[...], wb_ref[...], preferred_element_type=F32)
    _post_tail(x1, p_ref, pg_ref, wg_ref, wp_ref, o_ref)


def _post_odd_kernel(x_ref, cu_ref, halo_ref, gz_ref, cw_ref, p_ref, wo_ref, pg_ref, wg_ref, wp_ref,
                     o_ref, ext_ref, *, tiles_per_seq):
    tm = x_ref.shape[0]
    first = (pl.program_id(0) % tiles_per_seq) == 0
    halo = halo_ref[...].astype(F32)
    ext_ref[0:HALO, :] = jnp.where(first, 0.0, halo)
    ext_ref[HALO:HALO + tm, :] = cu_ref[...].astype(F32)
    conv = (cw_ref[0:1, :] * ext_ref[HALO - 2:HALO - 2 + tm, :]
            + cw_ref[1:2, :] * ext_ref[HALO - 1:HALO - 1 + tm, :]
            + cw_ref[2:3, :] * ext_ref[HALO:HALO + tm, :])
    y = (gz_ref[...].astype(F32) * conv).astype(BF16)
    x1 = x_ref[...] + jnp.dot(y, wo_ref[...], preferred_element_type=F32)
    _post_tail(x1, p_ref, pg_ref, wg_ref, wp_ref, o_ref)


def _const_spec(shape):
    return pl.BlockSpec(shape, lambda i: (0,) * len(shape), pipeline_mode=pl.Buffered(1))


def _post_even(x2, y_ret, y_att, p2, w_out_bf, pg, wg_bf, wp_bf):
    m = x2.shape[0]
    tm = POST_TM
    half = D_MODEL // 2
    rows = lambda w: pl.BlockSpec((tm, w), lambda i: (i, 0))
    return pl.pallas_call(
        _post_even_kernel,
        out_shape=jax.ShapeDtypeStruct((m, D_MODEL), F32),
        grid=(m // tm,),
        in_specs=[
            rows(D_MODEL), rows(half), rows(half), rows(PLE_DIM),
            pl.BlockSpec((half, D_MODEL), lambda i: (0, 0), pipeline_mode=pl.Buffered(1)),
            pl.BlockSpec((half, D_MODEL), lambda i: (1, 0), pipeline_mode=pl.Buffered(1)),
            _const_spec((1, D_MODEL)), _const_spec((D_MODEL, D_MODEL)), _const_spec((PLE_DIM, D_MODEL)),
        ],
        out_specs=rows(D_MODEL),
        compiler_params=pltpu.CompilerParams(
            dimension_semantics=("arbitrary",), vmem_limit_bytes=VMEM_LIMIT_BYTES),
        name="post_even",
    )(x2, y_ret, y_att, p2, w_out_bf, w_out_bf, pg, wg_bf, wp_bf)


def _post_odd(x2, cu, gz, conv_w, p2, w_out_bf, pg, wg_bf, wp_bf, seq):
    m = x2.shape[0]
    tm = POST_TM
    rows = lambda w: pl.BlockSpec((tm, w), lambda i: (i, 0))
    halo_spec = pl.BlockSpec((HALO, D_MODEL), lambda i: (jnp.maximum(i * (tm // HALO) - 1, 0), 0))
    return pl.pallas_call(
        functools.partial(_post_odd_kernel, tiles_per_seq=seq // tm),
        out_shape=jax.ShapeDtypeStruct((m, D_MODEL), F32),
        grid=(m // tm,),
        in_specs=[
            rows(D_MODEL), rows(D_MODEL), halo_spec, rows(D_MODEL),
            _const_spec((CONV_WIDTH, D_MODEL)), rows(PLE_DIM),
            _const_spec((D_MODEL, D_MODEL)),
            _const_spec((1, D_MODEL)), _const_spec((D_MODEL, D_MODEL)), _const_spec((PLE_DIM, D_MODEL)),
        ],
        out_specs=rows(D_MODEL),
        scratch_shapes=[pltpu.VMEM((HALO + tm, D_MODEL), F32)],
        compiler_params=pltpu.CompilerParams(
            dimension_semantics=("arbitrary",), vmem_limit_bytes=VMEM_LIMIT_BYTES),
        name="post_odd",
    )(x2, cu, cu, gz, conv_w, p2, w_out_bf, pg, wg_bf, wp_bf)


def _in_odd_kernel(x_ref, g_ref, wb_ref, wc_ref, wu_ref, wz_ref, cu_ref, gz_ref, hn_ref):
    @pl.when(pl.program_id(1) == 0)
    def _():
        hn_ref[...] = _rms_rows(x_ref[...], g_ref[...]).astype(BF16)

    hn = hn_ref[...]
    cg = jnp.dot(hn, wc_ref[...], preferred_element_type=F32)
    u = jnp.dot(hn, wu_ref[...], preferred_element_type=F32)
    cu_ref[...] = (cg * u).astype(BF16)
    bg = jnp.dot(hn, wb_ref[...], preferred_element_type=F32)
    z = jnp.dot(hn, wz_ref[...], preferred_element_type=F32)
    gz_ref[...] = (bg * _silu(z)).astype(BF16)


def _in_odd(x2, g, w_bf):
    m = x2.shape[0]
    tm, tc = IN_TM, IN_ODD_TC
    nb = D_MODEL // tc
    wspec = lambda seg: pl.BlockSpec((D_MODEL, tc), lambda i, j: (0, seg * nb + j))
    out = jax.ShapeDtypeStruct((m, D_MODEL), BF16)
    return pl.pallas_call(
        _in_odd_kernel,
        out_shape=(out, out),
        grid=(m // tm, nb),
        in_specs=[
            pl.BlockSpec((tm, D_MODEL), lambda i, j: (i, 0)),
            pl.BlockSpec((1, D_MODEL), lambda i, j: (0, 0)),
            wspec(0), wspec(1), wspec(2), wspec(3),
        ],
        out_specs=(pl.BlockSpec((tm, tc), lambda i, j: (i, j)), pl.BlockSpec((tm, tc), lambda i, j: (i, j))),
        scratch_shapes=[pltpu.VMEM((tm, D_MODEL), BF16)],
        compiler_params=pltpu.CompilerParams(
            dimension_semantics=("arbitrary", "arbitrary"), vmem_limit_bytes=VMEM_LIMIT_BYTES),
        name="in_odd",
    )(x2, g, w_bf, w_bf, w_bf, w_bf)


def _even_gain_row(q_g, k_g):
    ones = jnp.ones((OFF_AQ,), F32)
    return jnp.concatenate([
        ones,
        jnp.tile(q_g * ATT_HD ** -0.5, ATT_HEADS),
        jnp.tile(k_g, ATT_HEADS),
        jnp.ones((IN_WIDTH_EVEN - OFF_AV,), F32),
    ]).reshape(1, IN_WIDTH_EVEN)


def kernel(x, p, pre_norm_g, w_in_even, q_norm_g, k_norm_g, ret_gn_g, w_out_even, w_in_odd, conv_w_odd,
           w_out_odd, ple_norm_g, w_ple_gate, w_ple_proj):
    batch, seq, d = x.shape
    depth = p.shape[0]
    assert d == D_MODEL and seq % (ATT_BLOCK * DILATED_PATTERNS[-1][1]) == 0 and seq % IN_TM == 0
    m = batch * seq
    slopes = jnp.asarray([(2.0 ** (-8.0 / ATT_HEADS)) ** (i + 1) for i in range(ATT_HEADS)], F32)
    x2 = x.reshape(m, d)
    for i in range(depth):
        j = i // 2
        p2 = p[i].reshape(m, PLE_DIM)
        pg = ple_norm_g[i].reshape(1, d)
        wg_bf = w_ple_gate[i].astype(BF16)
        wp_bf = w_ple_proj[i].astype(BF16)
        g = pre_norm_g[i].reshape(1, d)
        if i % 2 == 0:
            proj = _in_even(x2, g, w_in_even[j].astype(BF16), _even_gain_row(q_norm_g[j], k_norm_g[j]))
            y_ret = _retention(proj, ret_gn_g[j], batch, seq)
            y_att = _attention(proj, slopes, batch, seq)
            x2 = _post_even(x2, y_ret, y_att, p2, w_out_even[j].astype(BF16), pg, wg_bf, wp_bf)
        else:
            cu, gz = _in_odd(x2, g, w_in_odd[j].astype(BF16))
            x2 = _post_odd(x2, cu, gz, conv_w_odd[j], p2, w_out_odd[j].astype(BF16), pg, wg_bf, wp_bf, seq)
    return x2.reshape(batch, seq, d)
```

```python
import functools
import math

import jax
import jax.numpy as jnp
from jax import lax
from jax.experimental import pallas as pl
from jax.experimental.pallas import tpu as pltpu

F32 = jnp.float32
BF16 = jnp.bfloat16

D_MODEL = 2048
PLE_DIM = 256
EPS = 1e-6
NEG = -1e30
RET_DK = 128
RET_DV = 256
RET_HEADS = 4
RET_CHUNK = 128
ATT_HD = 128
ATT_HEADS = 8
ATT_BLOCK = 128
DILATED_PATTERNS = ((128, 1), (512, 4), (2048, 16))
RET_QK_WIDTH = RET_HEADS * RET_DK
RET_WIDTH = RET_HEADS * RET_DV
ATT_WIDTH = ATT_HEADS * ATT_HD
IN_WIDTH_EVEN = 2 * RET_QK_WIDTH + RET_WIDTH + 3 * ATT_WIDTH + RET_WIDTH + ATT_WIDTH
CONV_WIDTH = 3

OFF_RQ = 0
OFF_RK = RET_QK_WIDTH
OFF_RV = 2 * RET_QK_WIDTH
OFF_AQ = OFF_RV + RET_WIDTH
OFF_AK = OFF_AQ + ATT_WIDTH
OFF_AV = OFF_AK + ATT_WIDTH
OFF_Z = OFF_AV + ATT_WIDTH

VMEM_LIMIT_BYTES = 56 * 1024 * 1024

IN_TM = 1024
IN_EVEN_TN = 1024
IN_ODD_TC = 256
RET_TQ = 1024
ATT_GROUP = 4
POST_TM = 256
POST_NC = 512
HALO = 8


def _rms_rows(x, g):
    ms = jnp.mean(x * x, axis=-1, keepdims=True)
    return x * lax.rsqrt(ms + EPS) * g


def _silu(z):
    return z * jax.nn.sigmoid(z)


def _in_even_kernel(x_ref, g_ref, w_ref, gain_ref, o_ref, hn_ref):
    j = pl.program_id(1)

    @pl.when(j == 0)
    def _():
        hn_ref[...] = _rms_rows(x_ref[...], g_ref[...]).astype(BF16)

    acc = jnp.dot(hn_ref[...], w_ref[...], preferred_element_type=F32)
    tn = acc.shape[1]
    j_aq = OFF_AQ // tn
    j_av = OFF_AV // tn
    j_z = OFF_Z // tn

    @pl.when((j < j_aq) | (j == j_av))
    def _():
        o_ref[...] = acc.astype(BF16)

    @pl.when((j >= j_aq) & (j < j_av))
    def _():
        for hh in range(tn // ATT_HD):
            sl = slice(hh * ATT_HD, (hh + 1) * ATT_HD)
            o_ref[:, sl] = _rms_rows(acc[:, sl], gain_ref[:, sl]).astype(BF16)

    @pl.when(j >= j_z)
    def _():
        o_ref[...] = _silu(acc).astype(BF16)


def _in_even(x2, g, w_bf, gain_row):
    m = x2.shape[0]
    tm, tn = IN_TM, IN_EVEN_TN
    return pl.pallas_call(
        _in_even_kernel,
        out_shape=jax.ShapeDtypeStruct((m, IN_WIDTH_EVEN), BF16),
        grid=(m // tm, IN_WIDTH_EVEN // tn),
        in_specs=[
            pl.BlockSpec((tm, D_MODEL), lambda i, j: (i, 0)),
            pl.BlockSpec((1, D_MODEL), lambda i, j: (0, 0)),
            pl.BlockSpec((D_MODEL, tn), lambda i, j: (0, j)),
            pl.BlockSpec((1, tn), lambda i, j: (0, j)),
        ],
        out_specs=pl.BlockSpec((tm, tn), lambda i, j: (i, j)),
        scratch_shapes=[pltpu.VMEM((tm, D_MODEL), BF16)],
        compiler_params=pltpu.CompilerParams(
            dimension_semantics=("arbitrary", "arbitrary"), vmem_limit_bytes=VMEM_LIMIT_BYTES),
        name="in_even",
    )(x2, g, w_bf, gain_row)


def _ret_kernel(q_ref, k_ref, v_ref, z_ref, gn_ref, o_ref, r_ref):
    c = RET_CHUNK
    tq = q_ref.shape[0]

    @pl.when(pl.program_id(1) == 0)
    def _():
        r_ref[...] = jnp.zeros(r_ref.shape, F32)

    row = lax.broadcasted_iota(jnp.int32, (c, c), 0)
    col = lax.broadcasted_iota(jnp.int32, (c, c), 1)
    diff = (row - col).astype(F32)
    pos = lax.broadcasted_iota(jnp.int32, (c, 1), 0).astype(F32)
    kscale = RET_DK ** -0.5

    for h in range(RET_HEADS):
        lg = math.log(1.0 - 2.0 ** (-5.0 - h))
        decay = jnp.where(diff >= 0, jnp.exp(lg * jnp.maximum(diff, 0.0)), 0.0) * kscale
        xi = jnp.exp(lg * (pos + 1.0))
        zeta = jnp.exp(lg * (c - 1.0 - pos)) * kscale
        g_chunk = math.exp(lg * c)
        qk_sl = slice(h * RET_DK, (h + 1) * RET_DK)
        v_sl = slice(h * RET_DV, (h + 1) * RET_DV)
        gn = gn_ref[h:h + 1, :]

        def chunk(ci, carry, decay=decay, xi=xi, zeta=zeta, g_chunk=g_chunk, qk_sl=qk_sl, v_sl=v_sl,
                  gn=gn, h=h):
            rows = pl.ds(pl.multiple_of(ci * c, c), c)
            q = q_ref[rows, qk_sl]
            k = k_ref[rows, qk_sl]
            v = v_ref[rows, v_sl]
            s = lax.dot_general(q, k, (((1,), (1,)), ((), ())), preferred_element_type=F32) * decay
            r_prev = r_ref[h]
            y = jnp.dot(s.astype(BF16), v, preferred_element_type=F32)
            y = y + jnp.dot(q, r_prev.astype(BF16), preferred_element_type=F32) * xi
            kz_t = (k.astype(F32) * zeta).T.astype(BF16)
            r_ref[h] = g_chunk * r_prev + jnp.dot(kz_t, v, preferred_element_type=F32)
            mu = jnp.mean(y, axis=-1, keepdims=True)
            yc = y - mu
            var = jnp.mean(yc * yc, axis=-1, keepdims=True)
            yn = yc * lax.rsqrt(var + EPS) * gn
            o_ref[rows, v_sl] = (yn * z_ref[rows, v_sl].astype(F32)).astype(BF16)
            return carry

        lax.fori_loop(0, tq // c, chunk, 0)


def _retention(proj, gn_g, batch, seq):
    m = proj.shape[0]
    tq = RET_TQ
    nt = seq // tq
    row = lambda b, t: b * nt + t
    return pl.pallas_call(
        _ret_kernel,
        out_shape=jax.ShapeDtypeStruct((m, RET_WIDTH), BF16),
        grid=(batch, nt),
        in_specs=[
            pl.BlockSpec((tq, RET_QK_WIDTH), lambda b, t: (row(b, t), OFF_RQ // RET_QK_WIDTH)),
            pl.BlockSpec((tq, RET_QK_WIDTH), lambda b, t: (row(b, t), OFF_RK // RET_QK_WIDTH)),
            pl.BlockSpec((tq, RET_WIDTH), lambda b, t: (row(b, t), OFF_RV // RET_WIDTH)),
            pl.BlockSpec((tq, RET_WIDTH), lambda b, t: (row(b, t), OFF_Z // RET_WIDTH)),
            pl.BlockSpec((RET_HEADS, RET_DV), lambda b, t: (0, 0)),
        ],
        out_specs=pl.BlockSpec((tq, RET_WIDTH), lambda b, t: (row(b, t), 0)),
        scratch_shapes=[pltpu.VMEM((RET_HEADS, RET_DK, RET_DV), F32)],
        compiler_params=pltpu.CompilerParams(
            dimension_semantics=("arbitrary", "arbitrary"), vmem_limit_bytes=VMEM_LIMIT_BYTES),
        name="retention",
    )(proj, proj, proj, proj, gn_g)


def _att_kernel(slopes_ref, q_ref, k_ref, v_ref, z_ref, o_ref, qf, kf, vf, qp, kp, vp, bias_ref, pn_ref,
                o0, o1, o2, l0, l1, l2):
    blk = ATT_BLOCK
    seq = q_ref.shape[0]
    nblk = seq // blk
    slope = slopes_ref[pl.program_id(1)]
    qf[...] = q_ref[...].astype(F32)
    kf[...] = k_ref[...].astype(F32)
    vf[...] = v_ref[...].astype(F32)
    kp[0:blk, :] = jnp.zeros((blk, ATT_HD), BF16)
    vp[0:blk, :] = jnp.zeros((blk, ATT_HD), BF16)

    qi = lax.broadcasted_iota(jnp.int32, (blk, 2 * blk), 0)
    kj = lax.broadcasted_iota(jnp.int32, (blk, 2 * blk), 1)
    rel = qi - kj + blk
    nt = (((1,), (1,)), ((), ()))

    for (window, dil), o_g, l_g in zip(DILATED_PATTERNS, (o0, o1, o2), (l0, l1, l2)):
        steps = window // dil
        nb = nblk // dil
        valid = (rel >= 0) & (rel <= steps)
        alibi = -slope * (rel * dil).astype(F32)
        bias_ref[0] = jnp.where(valid & (kj >= blk), alibi, NEG)
        bias_ref[1] = jnp.where(valid, alibi, NEG)

        def rows_at(bi, dil=dil, nb=nb):
            start = (bi // nb) + (bi % nb) * (blk * dil)
            if dil == 1:
                return pl.ds(pl.multiple_of(start, blk), blk)
            return pl.ds(start, blk, stride=dil)

        def permute(bi, carry, rows_at=rows_at):
            dst = pl.multiple_of(bi * blk, blk)
            src = rows_at(bi)
            qp[pl.ds(dst, blk), :] = qf[src, :].astype(BF16)
            kp[pl.ds(dst + blk, blk), :] = kf[src, :].astype(BF16)
            vp[pl.ds(dst + blk, blk), :] = vf[src, :].astype(BF16)
            return carry

        lax.fori_loop(0, nblk, permute, 0)

        def probs(it, nb=nb, l_g=l_g, rows_at=rows_at):
            for g in range(ATT_GROUP):
                bi = it * ATT_GROUP + g
                off = pl.multiple_of(bi * blk, blk)
                q = qp[pl.ds(off, blk), :]
                kb = kp[pl.ds(off, 2 * blk), :]
                has_prev = jnp.minimum(bi % nb, 1)
                s = lax.dot_general(q, kb, nt, preferred_element_type=F32) + bias_ref[has_prev]
                mx = jnp.max(s, axis=-1, keepdims=True)
                pr = jnp.exp(s - mx)
                den = jnp.sum(pr, axis=-1, keepdims=True)
                pn_ref[it % 2, g] = (pr * (1.0 / den)).astype(BF16)
                l_g[rows_at(bi), :] = jnp.broadcast_to(mx + jnp.log(den), (blk, ATT_HD))

        def values(it, o_g=o_g, rows_at=rows_at):
            for g in range(ATT_GROUP):
                bi = it * ATT_GROUP + g
                off = pl.multiple_of(bi * blk, blk)
                vb = vp[pl.ds(off, 2 * blk), :]
                o_g[rows_at(bi), :] = jnp.dot(pn_ref[it % 2, g], vb, preferred_element_type=F32)

        n_it = nblk // ATT_GROUP
        probs(0)

        def body(it, carry, probs=probs, values=values):
            values(it - 1)
            probs(it)
            return carry

        lax.fori_loop(1, n_it, body, 0)
        values(n_it - 1)

    mrows = 256

    def merge(ci, carry):
        rows = pl.ds(pl.multiple_of(ci * mrows, mrows), mrows)
        a0, a1, a2 = l0[rows, :], l1[rows, :], l2[rows, :]
        mx = jnp.maximum(jnp.maximum(a0, a1), a2)
        e0, e1, e2 = jnp.exp(a0 - mx), jnp.exp(a1 - mx), jnp.exp(a2 - mx)
        y = (e0 * o0[rows, :] + e1 * o1[rows, :] + e2 * o2[rows, :]) / (e0 + e1 + e2)
        o_ref[rows, :] = (y * z_ref[rows, :].astype(F32)).astype(BF16)
        return carry

    lax.fori_loop(0, seq // mrows, merge, 0)


def _attention(proj, slopes, batch, seq):
    m = proj.shape[0]
    hd = ATT_HD
    spec = lambda off: pl.BlockSpec((seq, hd), lambda b, h: (b, off // hd + h))
    f32_rows = pltpu.VMEM((seq, hd), F32)
    return pl.pallas_call(
        _att_kernel,
        out_shape=jax.ShapeDtypeStruct((m, ATT_WIDTH), BF16),
        grid=(batch, ATT_HEADS),
        in_specs=[
            pl.BlockSpec(memory_space=pltpu.SMEM),
            spec(OFF_AQ), spec(OFF_AK), spec(OFF_AV), spec(OFF_Z + RET_WIDTH),
        ],
        out_specs=pl.BlockSpec((seq, hd), lambda b, h: (b, h)),
        scratch_shapes=[
            f32_rows, f32_rows, f32_rows,
            pltpu.VMEM((seq, hd), BF16),
            pltpu.VMEM((seq + ATT_BLOCK, hd), BF16),
            pltpu.VMEM((seq + ATT_BLOCK, hd), BF16),
            pltpu.VMEM((2, ATT_BLOCK, 2 * ATT_BLOCK), F32),
            pltpu.VMEM((2, ATT_GROUP, ATT_BLOCK, 2 * ATT_BLOCK), BF16),
            f32_rows, f32_rows, f32_rows, f32_rows, f32_rows, f32_rows,
        ],
        compiler_params=pltpu.CompilerParams(
            dimension_semantics=("arbitrary", "arbitrary"), vmem_limit_bytes=VMEM_LIMIT_BYTES),
        name="dilated_attention",
    )(slopes, proj, proj, proj, proj)


def _post_tail(x1, p_ref, pg_ref, wg_ref, wp_ref, o_ref):
    h2 = _rms_rows(x1, pg_ref[...]).astype(BF16)
    pb = p_ref[...].astype(BF16)
    for cc in range(D_MODEL // POST_NC):
        sl = slice(cc * POST_NC, (cc + 1) * POST_NC)
        gate = jax.nn.sigmoid(jnp.dot(h2, wg_ref[:, sl], preferred_element_type=F32))
        pp = jnp.dot(pb, wp_ref[:, sl], preferred_element_type=F32)
        o_ref[:, sl] = x1[:, sl] + gate * pp


def _post_even_kernel(x_ref, ya_ref, yb_ref, p_ref, wa_ref, wb_ref, pg_ref, wg_ref, wp_ref, o_ref):
    x1 = x_ref[...] + jnp.dot(ya_ref[...], wa_ref[...], preferred_element_type=F32)
    x1 = x1 + jnp.dot(yb_ref[...], wb_ref[...], preferred_element_type=F32)
    _post_tail(x1, p_ref, pg_ref, wg_ref, wp_ref, o_ref)


def _post_odd_kernel(x_ref, cu_ref, halo_ref, gz_ref, cw_ref, p_ref, wo_ref, pg_ref, wg_ref, wp_ref,
                     o_ref, ext_ref, *, tiles_per_seq):
    tm = x_ref.shape[0]
    first = (pl.program_id(0) % tiles_per_seq) == 0
    halo = halo_ref[...].astype(F32)
    ext_ref[0:HALO, :] = jnp.where(first, 0.0, halo)
    ext_ref[HALO:HALO + tm, :] = cu_ref[...].astype(F32)
    conv = (cw_ref[0:1, :] * ext_ref[HALO - 2:HALO - 2 + tm, :]
            + cw_ref[1:2, :] * ext_ref[HALO - 1:HALO - 1 + tm, :]
            + cw_ref[2:3, :] * ext_ref[HALO:HALO + tm, :])
    y = (gz_ref[...].astype(F32) * conv).astype(BF16)
    x1 = x_ref[...] + jnp.dot(y, wo_ref[...], preferred_element_type=F32)
    _post_tail(x1, p_ref, pg_ref, wg_ref, wp_ref, o_ref)


def _const_spec(shape):
    return pl.BlockSpec(shape, lambda i: (0,) * len(shape), pipeline_mode=pl.Buffered(1))


def _post_even(x2, y_ret, y_att, p2, w_out_bf, pg, wg_bf, wp_bf):
    m = x2.shape[0]
    tm = POST_TM
    half = D_MODEL // 2
    rows = lambda w: pl.BlockSpec((tm, w), lambda i: (i, 0))
    return pl.pallas_call(
        _post_even_kernel,
        out_shape=jax.ShapeDtypeStruct((m, D_MODEL), F32),
        grid=(m // tm,),
        in_specs=[
            rows(D_MODEL), rows(half), rows(half), rows(PLE_DIM),
            pl.BlockSpec((half, D_MODEL), lambda i: (0, 0), pipeline_mode=pl.Buffered(1)),
            pl.BlockSpec((half, D_MODEL), lambda i: (1, 0), pipeline_mode=pl.Buffered(1)),
            _const_spec((1, D_MODEL)), _const_spec((D_MODEL, D_MODEL)), _const_spec((PLE_DIM, D_MODEL)),
        ],
        out_specs=rows(D_MODEL),
        compiler_params=pltpu.CompilerParams(
            dimension_semantics=("arbitrary",), vmem_limit_bytes=VMEM_LIMIT_BYTES),
        name="post_even",
    )(x2, y_ret, y_att, p2, w_out_bf, w_out_bf, pg, wg_bf, wp_bf)


def _post_odd(x2, cu, gz, conv_w, p2, w_out_bf, pg, wg_bf, wp_bf, seq):
    m = x2.shape[0]
    tm = POST_TM
    rows = lambda w: pl.BlockSpec((tm, w), lambda i: (i, 0))
    halo_spec = pl.BlockSpec((HALO, D_MODEL), lambda i: (jnp.maximum(i * (tm // HALO) - 1, 0), 0))
    return pl.pallas_call(
        functools.partial(_post_odd_kernel, tiles_per_seq=seq // tm),
        out_shape=jax.ShapeDtypeStruct((m, D_MODEL), F32),
        grid=(m // tm,),
        in_specs=[
            rows(D_MODEL), rows(D_MODEL), halo_spec, rows(D_MODEL),
            _const_spec((CONV_WIDTH, D_MODEL)), rows(PLE_DIM),
            _const_spec((D_MODEL, D_MODEL)),
            _const_spec((1, D_MODEL)), _const_spec((D_MODEL, D_MODEL)), _const_spec((PLE_DIM, D_MODEL)),
        ],
        out_specs=rows(D_MODEL),
        scratch_shapes=[pltpu.VMEM((HALO + tm, D_MODEL), F32)],
        compiler_params=pltpu.CompilerParams(
            dimension_semantics=("arbitrary",), vmem_limit_bytes=VMEM_LIMIT_BYTES),
        name="post_odd",
    )(x2, cu, cu, gz, conv_w, p2, w_out_bf, pg, wg_bf, wp_bf)


def _in_odd_kernel(x_ref, g_ref, wb_ref, wc_ref, wu_ref, wz_ref, cu_ref, gz_ref, hn_ref):
    @pl.when(pl.program_id(1) == 0)
    def _():
        hn_ref[...] = _rms_rows(x_ref[...], g_ref[...]).astype(BF16)

    hn = hn_ref[...]
    cg = jnp.dot(hn, wc_ref[...], preferred_element_type=F32)
    u = jnp.dot(hn, wu_ref[...], preferred_element_type=F32)
    cu_ref[...] = (cg * u).astype(BF16)
    bg = jnp.dot(hn, wb_ref[...], preferred_element_type=F32)
    z = jnp.dot(hn, wz_ref[...], preferred_element_type=F32)
    gz_ref[...] = (bg * _silu(z)).astype(BF16)


def _in_odd(x2, g, w_bf):
    m = x2.shape[0]
    tm, tc = IN_TM, IN_ODD_TC
    nb = D_MODEL // tc
    wspec = lambda seg: pl.BlockSpec((D_MODEL, tc), lambda i, j: (0, seg * nb + j))
    out = jax.ShapeDtypeStruct((m, D_MODEL), BF16)
    return pl.pallas_call(
        _in_odd_kernel,
        out_shape=(out, out),
        grid=(m // tm, nb),
        in_specs=[
            pl.BlockSpec((tm, D_MODEL), lambda i, j: (i, 0)),
            pl.BlockSpec((1, D_MODEL), lambda i, j: (0, 0)),
            wspec(0), wspec(1), wspec(2), wspec(3),
        ],
        out_specs=(pl.BlockSpec((tm, tc), lambda i, j: (i, j)), pl.BlockSpec((tm, tc), lambda i, j: (i, j))),
        scratch_shapes=[pltpu.VMEM((tm, D_MODEL), BF16)],
        compiler_params=pltpu.CompilerParams(
            dimension_semantics=("arbitrary", "arbitrary"), vmem_limit_bytes=VMEM_LIMIT_BYTES),
        name="in_odd",
    )(x2, g, w_bf, w_bf, w_bf, w_bf)


def _even_gain_row(q_g, k_g):
    ones = jnp.ones((OFF_AQ,), F32)
    return jnp.concatenate([
        ones,
        jnp.tile(q_g * ATT_HD ** -0.5, ATT_HEADS),
        jnp.tile(k_g, ATT_HEADS),
        jnp.ones((IN_WIDTH_EVEN - OFF_AV,), F32),
    ]).reshape(1, IN_WIDTH_EVEN)


def kernel(x, p, pre_norm_g, w_in_even, q_norm_g, k_norm_g, ret_gn_g, w_out_even, w_in_odd, conv_w_odd,
           w_out_odd, ple_norm_g, w_ple_gate, w_ple_proj):
    batch, seq, d = x.shape
    depth = p.shape[0]
    assert d == D_MODEL and seq % (ATT_BLOCK * DILATED_PATTERNS[-1][1]) == 0 and seq % IN_TM == 0
    m = batch * seq
    slopes = jnp.asarray([(2.0 ** (-8.0 / ATT_HEADS)) ** (i + 1) for i in range(ATT_HEADS)], F32)
    x2 = x.reshape(m, d)
    for i in range(depth):
        j = i // 2
        p2 = p[i].reshape(m, PLE_DIM)
        pg = ple_norm_g[i].reshape(1, d)
        wg_bf = w_ple_gate[i].astype(BF16)
        wp_bf = w_ple_proj[i].astype(BF16)
        g = pre_norm_g[i].reshape(1, d)
        if i % 2 == 0:
            proj = _in_even(x2, g, w_in_even[j].astype(BF16), _even_gain_row(q_norm_g[j], k_norm_g[j]))
            y_ret = _retention(proj, ret_gn_g[j], batch, seq)
            y_att = _attention(proj, slopes, batch, seq)
            x2 = _post_even(x2, y_ret, y_att, p2, w_out_even[j].astype(BF16), pg, wg_bf, wp_bf)
        else:
            cu, gz = _in_odd(x2, g, w_in_odd[j].astype(BF16))
            x2 = _post_odd(x2, cu, gz, conv_w_odd[j], p2, w_out_odd[j].astype(BF16), pg, wg_bf, wp_bf, seq)
    return x2.reshape(batch, seq, d)
```

```python
import functools
import math

import jax
import jax.numpy as jnp
from jax import lax
from jax.experimental import pallas as pl
from jax.experimental.pallas import tpu as pltpu

F32 = jnp.float32
BF16 = jnp.bfloat16

D_MODEL = 2048
PLE_DIM = 256
EPS = 1e-6
NEG = -1e30
RET_DK = 128
RET_DV = 256
RET_HEADS = 4
RET_CHUNK = 256
ATT_HD = 128
ATT_HEADS = 8
ATT_BLOCK = 128
DILATED_PATTERNS = ((128, 1), (512, 4), (2048, 16))
RET_QK_WIDTH = RET_HEADS * RET_DK
RET_WIDTH = RET_HEADS * RET_DV
ATT_WIDTH = ATT_HEADS * ATT_HD
IN_WIDTH_EVEN = 2 * RET_QK_WIDTH + RET_WIDTH + 3 * ATT_WIDTH + RET_WIDTH + ATT_WIDTH
CONV_WIDTH = 3

OFF_RQ = 0
OFF_RK = RET_QK_WIDTH
OFF_RV = 2 * RET_QK_WIDTH
OFF_AQ = OFF_RV + RET_WIDTH
OFF_AK = OFF_AQ + ATT_WIDTH
OFF_AV = OFF_AK + ATT_WIDTH
OFF_Z = OFF_AV + ATT_WIDTH

VMEM_LIMIT_BYTES = 56 * 1024 * 1024

IN_TM = 1024
IN_EVEN_TN = 1024
IN_ODD_TC = 256
RET_TQ = 1024
ATT_GROUP = 4
POST_TM = 256
POST_NC = 512
HALO = 8


def _rms_rows(x, g):
    ms = jnp.mean(x * x, axis=-1, keepdims=True)
    return x * lax.rsqrt(ms + EPS) * g


def _silu(z):
    return z * jax.nn.sigmoid(z)


def _in_even_kernel(x_ref, g_ref, w_ref, gain_ref, o_ref, hn_ref):
    j = pl.program_id(1)

    @pl.when(j == 0)
    def _():
        hn_ref[...] = _rms_rows(x_ref[...], g_ref[...]).astype(BF16)

    acc = jnp.dot(hn_ref[...], w_ref[...], preferred_element_type=F32)
    tn = acc.shape[1]
    j_aq = OFF_AQ // tn
    j_av = OFF_AV // tn
    j_z = OFF_Z // tn

    @pl.when((j < j_aq) | (j == j_av))
    def _():
        o_ref[...] = acc.astype(BF16)

    @pl.when((j >= j_aq) & (j < j_av))
    def _():
        for hh in range(tn // ATT_HD):
            sl = slice(hh * ATT_HD, (hh + 1) * ATT_HD)
            o_ref[:, sl] = _rms_rows(acc[:, sl], gain_ref[:, sl]).astype(BF16)

    @pl.when(j >= j_z)
    def _():
        o_ref[...] = _silu(acc).astype(BF16)


def _in_even(x2, g, w_bf, gain_row):
    m = x2.shape[0]
    tm, tn = IN_TM, IN_EVEN_TN
    return pl.pallas_call(
        _in_even_kernel,
        out_shape=jax.ShapeDtypeStruct((m, IN_WIDTH_EVEN), BF16),
        grid=(m // tm, IN_WIDTH_EVEN // tn),
        in_specs=[
            pl.BlockSpec((tm, D_MODEL), lambda i, j: (i, 0)),
            pl.BlockSpec((1, D_MODEL), lambda i, j: (0, 0)),
            pl.BlockSpec((D_MODEL, tn), lambda i, j: (0, j)),
            pl.BlockSpec((1, tn), lambda i, j: (0, j)),
        ],
        out_specs=pl.BlockSpec((tm, tn), lambda i, j: (i, j)),
        scratch_shapes=[pltpu.VMEM((tm, D_MODEL), BF16)],
        compiler_params=pltpu.CompilerParams(
            dimension_semantics=("arbitrary", "arbitrary"), vmem_limit_bytes=VMEM_LIMIT_BYTES),
        name="in_even",
    )(x2, g, w_bf, gain_row)


def _ret_kernel(q_ref, k_ref, v_ref, z_ref, gn_ref, o_ref, r_ref):
    c = RET_CHUNK
    tq = q_ref.shape[0]

    @pl.when(pl.program_id(1) == 0)
    def _():
        r_ref[...] = jnp.zeros(r_ref.shape, F32)

    row = lax.broadcasted_iota(jnp.int32, (c, c), 0)
    col = lax.broadcasted_iota(jnp.int32, (c, c), 1)
    diff = (row - col).astype(F32)
    pos = lax.broadcasted_iota(jnp.int32, (c, 1), 0).astype(F32)
    kscale = RET_DK ** -0.5

    for h in range(RET_HEADS):
        lg = math.log(1.0 - 2.0 ** (-5.0 - h))
        decay = jnp.where(diff >= 0, jnp.exp(lg * jnp.maximum(diff, 0.0)), 0.0) * kscale
        xi = jnp.exp(lg * (pos + 1.0))
        zeta = jnp.exp(lg * (c - 1.0 - pos)) * kscale
        g_chunk = math.exp(lg * c)
        qk_sl = slice(h * RET_DK, (h + 1) * RET_DK)
        v_sl = slice(h * RET_DV, (h + 1) * RET_DV)
        gn = gn_ref[h:h + 1, :]
        chunks = [slice(ci * c, (ci + 1) * c) for ci in range(tq // c)]

        contrib = []
        for rows in chunks:
            kz_t = (k_ref[rows, qk_sl].astype(F32) * zeta).T.astype(BF16)
            contrib.append(jnp.dot(kz_t, v_ref[rows, v_sl], preferred_element_type=F32))

        state = r_ref[h]
        states = []
        for s_c in contrib:
            states.append(state)
            state = g_chunk * state + s_c
        r_ref[h] = state

        for rows, r_prev in zip(chunks, states):
            q = q_ref[rows, qk_sl]
            v = v_ref[rows, v_sl]
            s = lax.dot_general(q, k_ref[rows, qk_sl], (((1,), (1,)), ((), ())),
                                preferred_element_type=F32) * decay
            y = jnp.dot(s.astype(BF16), v, preferred_element_type=F32)
            y = y + jnp.dot(q, r_prev.astype(BF16), preferred_element_type=F32) * xi
            mu = jnp.mean(y, axis=-1, keepdims=True)
            yc = y - mu
            var = jnp.mean(yc * yc, axis=-1, keepdims=True)
            yn = yc * lax.rsqrt(var + EPS) * gn
            o_ref[rows, v_sl] = (yn * z_ref[rows, v_sl].astype(F32)).astype(BF16)


def _retention(proj, gn_g, batch, seq):
    m = proj.shape[0]
    tq = RET_TQ
    nt = seq // tq
    row = lambda b, t: b * nt + t
    return pl.pallas_call(
        _ret_kernel,
        out_shape=jax.ShapeDtypeStruct((m, RET_WIDTH), BF16),
        grid=(batch, nt),
        in_specs=[
            pl.BlockSpec((tq, RET_QK_WIDTH), lambda b, t: (row(b, t), OFF_RQ // RET_QK_WIDTH)),
            pl.BlockSpec((tq, RET_QK_WIDTH), lambda b, t: (row(b, t), OFF_RK // RET_QK_WIDTH)),
            pl.BlockSpec((tq, RET_WIDTH), lambda b, t: (row(b, t), OFF_RV // RET_WIDTH)),
            pl.BlockSpec((tq, RET_WIDTH), lambda b, t: (row(b, t), OFF_Z // RET_WIDTH)),
            pl.BlockSpec((RET_HEADS, RET_DV), lambda b, t: (0, 0)),
        ],
        out_specs=pl.BlockSpec((tq, RET_WIDTH), lambda b, t: (row(b, t), 0)),
        scratch_shapes=[pltpu.VMEM((RET_HEADS, RET_DK, RET_DV), F32)],
        compiler_params=pltpu.CompilerParams(
            dimension_semantics=("arbitrary", "arbitrary"), vmem_limit_bytes=VMEM_LIMIT_BYTES),
        name="retention",
    )(proj, proj, proj, proj, gn_g)


def _att_kernel(slopes_ref, q_ref, k_ref, v_ref, z_ref, o_ref, qf, kf, vf, qp, kp, vp, bias_ref, pn_ref,
                o0, o1, o2, l0, l1, l2):
    blk = ATT_BLOCK
    seq = q_ref.shape[0]
    nblk = seq // blk
    slope = slopes_ref[pl.program_id(1)]
    qf[...] = q_ref[...].astype(F32)
    kf[...] = k_ref[...].astype(F32)
    vf[...] = v_ref[...].astype(F32)
    kp[0:blk, :] = jnp.zeros((blk, ATT_HD), BF16)
    vp[0:blk, :] = jnp.zeros((blk, ATT_HD), BF16)

    qi = lax.broadcasted_iota(jnp.int32, (blk, 2 * blk), 0)
    kj = lax.broadcasted_iota(jnp.int32, (blk, 2 * blk), 1)
    rel = qi - kj + blk
    nt = (((1,), (1,)), ((), ()))

    for (window, dil), o_g, l_g in zip(DILATED_PATTERNS, (o0, o1, o2), (l0, l1, l2)):
        steps = window // dil
        nb = nblk // dil
        valid = (rel >= 0) & (rel <= steps)
        alibi = -slope * (rel * dil).astype(F32)
        bias_ref[0] = jnp.where(valid & (kj >= blk), alibi, NEG)
        bias_ref[1] = jnp.where(valid, alibi, NEG)

        def rows_at(bi, dil=dil, nb=nb):
            start = (bi // nb) + (bi % nb) * (blk * dil)
            if dil == 1:
                return pl.ds(pl.multiple_of(start, blk), blk)
            return pl.ds(start, blk, stride=dil)

        def permute(bi, carry, rows_at=rows_at):
            dst = pl.multiple_of(bi * blk, blk)
            src = rows_at(bi)
            qp[pl.ds(dst, blk), :] = qf[src, :].astype(BF16)
            kp[pl.ds(dst + blk, blk), :] = kf[src, :].astype(BF16)
            vp[pl.ds(dst + blk, blk), :] = vf[src, :].astype(BF16)
            return carry

        lax.fori_loop(0, nblk, permute, 0)

        def probs(it, nb=nb, l_g=l_g, rows_at=rows_at):
            for g in range(ATT_GROUP):
                bi = it * ATT_GROUP + g
                off = pl.multiple_of(bi * blk, blk)
                q = qp[pl.ds(off, blk), :]
                kb = kp[pl.ds(off, 2 * blk), :]
                has_prev = jnp.minimum(bi % nb, 1)
                s = lax.dot_general(q, kb, nt, preferred_element_type=F32) + bias_ref[has_prev]
                mx = jnp.max(s, axis=-1, keepdims=True)
                pr = jnp.exp(s - mx)
                den = jnp.sum(pr, axis=-1, keepdims=True)
                pn_ref[it % 2, g] = (pr * (1.0 / den)).astype(BF16)
                l_g[rows_at(bi), :] = jnp.broadcast_to(mx + jnp.log(den), (blk, ATT_HD))

        def values(it, o_g=o_g, rows_at=rows_at):
            for g in range(ATT_GROUP):
                bi = it * ATT_GROUP + g
                off = pl.multiple_of(bi * blk, blk)
                vb = vp[pl.ds(off, 2 * blk), :]
                o_g[rows_at(bi), :] = jnp.dot(pn_ref[it % 2, g], vb, preferred_element_type=F32)

        n_it = nblk // ATT_GROUP
        probs(0)

        def body(it, carry, probs=probs, values=values):
            values(it - 1)
            probs(it)
            return carry

        lax.fori_loop(1, n_it, body, 0)
        values(n_it - 1)

    mrows = 256

    def merge(ci, carry):
        rows = pl.ds(pl.multiple_of(ci * mrows, mrows), mrows)
        a0, a1, a2 = l0[rows, :], l1[rows, :], l2[rows, :]
        mx = jnp.maximum(jnp.maximum(a0, a1), a2)
        e0, e1, e2 = jnp.exp(a0 - mx), jnp.exp(a1 - mx), jnp.exp(a2 - mx)
        y = (e0 * o0[rows, :] + e1 * o1[rows, :] + e2 * o2[rows, :]) / (e0 + e1 + e2)
        o_ref[rows, :] = (y * z_ref[rows, :].astype(F32)).astype(BF16)
        return carry

    lax.fori_loop(0, seq // mrows, merge, 0)


def _attention(proj, slopes, batch, seq):
    m = proj.shape[0]
    hd = ATT_HD
    spec = lambda off: pl.BlockSpec((seq, hd), lambda b, h: (b, off // hd + h))
    f32_rows = pltpu.VMEM((seq, hd), F32)
    return pl.pallas_call(
        _att_kernel,
        out_shape=jax.ShapeDtypeStruct((m, ATT_WIDTH), BF16),
        grid=(batch, ATT_HEADS),
        in_specs=[
            pl.BlockSpec(memory_space=pltpu.SMEM),
            spec(OFF_AQ), spec(OFF_AK), spec(OFF_AV), spec(OFF_Z + RET_WIDTH),
        ],
        out_specs=pl.BlockSpec((seq, hd), lambda b, h: (b, h)),
        scratch_shapes=[
            f32_rows, f32_rows, f32_rows,
            pltpu.VMEM((seq, hd), BF16),
            pltpu.VMEM((seq + ATT_BLOCK, hd), BF16),
            pltpu.VMEM((seq + ATT_BLOCK, hd), BF16),
            pltpu.VMEM((2, ATT_BLOCK, 2 * ATT_BLOCK), F32),
            pltpu.VMEM((2, ATT_GROUP, ATT_BLOCK, 2 * ATT_BLOCK), BF16),
            f32_rows, f32_rows, f32_rows, f32_rows, f32_rows, f32_rows,
        ],
        compiler_params=pltpu.CompilerParams(
            dimension_semantics=("arbitrary", "arbitrary"), vmem_limit_bytes=VMEM_LIMIT_BYTES),
        name="dilated_attention",
    )(slopes, proj, proj, proj, proj)


def _post_tail(x1, p_ref, pg_ref, wg_ref, wp_ref, o_ref):
    h2 = _rms_rows(x1, pg_ref[...]).astype(BF16)
    pb = p_ref[...].astype(BF16)
    for cc in range(D_MODEL // POST_NC):
        sl = slice(cc * POST_NC, (cc + 1) * POST_NC)
        gate = jax.nn.sigmoid(jnp.dot(h2, wg_ref[:, sl], preferred_element_type=F32))
        pp = jnp.dot(pb, wp_ref[:, sl], preferred_element_type=F32)
        o_ref[:, sl] = x1[:, sl] + gate * pp


def _post_even_kernel(x_ref, ya_ref, yb_ref, p_ref, wa_ref, wb_ref, pg_ref, wg_ref, wp_ref, o_ref):
    x1 = x_ref[...] + jnp.dot(ya_ref[...], wa_ref[...], preferred_element_type=F32)
    x1 = x1 + jnp.dot(yb_ref[...], wb_ref[...], preferred_element_type=F32)
    _post_tail(x1, p_ref, pg_ref, wg_ref, wp_ref, o_ref)


def _post_odd_kernel(x_ref, cu_ref, halo_ref, gz_ref, cw_ref, p_ref, wo_ref, pg_ref, wg_ref, wp_ref,
                     o_ref, ext_ref, *, tiles_per_seq):
    tm = x_ref.shape[0]
    first = (pl.program_id(0) % tiles_per_seq) == 0
    halo = halo_ref[...].astype(F32)
    ext_ref[0:HALO, :] = jnp.where(first, 0.0, halo)
    ext_ref[HALO:HALO + tm, :] = cu_ref[...].astype(F32)
    conv = (cw_ref[0:1, :] * ext_ref[HALO - 2:HALO - 2 + tm, :]
            + cw_ref[1:2, :] * ext_ref[HALO - 1:HALO - 1 + tm, :]
            + cw_ref[2:3, :] * ext_ref[HALO:HALO + tm, :])
    y = (gz_ref[...].astype(F32) * conv).astype(BF16)
    x1 = x_ref[...] + jnp.dot(y, wo_ref[...], preferred_element_type=F32)
    _post_tail(x1, p_ref, pg_ref, wg_ref, wp_ref, o_ref)


def _const_spec(shape):
    return pl.BlockSpec(shape, lambda i: (0,) * len(shape), pipeline_mode=pl.Buffered(1))


def _post_even(x2, y_ret, y_att, p2, w_out_bf, pg, wg_bf, wp_bf):
    m = x2.shape[0]
    tm = POST_TM
    half = D_MODEL // 2
    rows = lambda w: pl.BlockSpec((tm, w), lambda i: (i, 0))
    return pl.pallas_call(
        _post_even_kernel,
        out_shape=jax.ShapeDtypeStruct((m, D_MODEL), F32),
        grid=(m // tm,),
        in_specs=[
            rows(D_MODEL), rows(half), rows(half), rows(PLE_DIM),
            pl.BlockSpec((half, D_MODEL), lambda i: (0, 0), pipeline_mode=pl.Buffered(1)),
            pl.BlockSpec((half, D_MODEL), lambda i: (1, 0), pipeline_mode=pl.Buffered(1)),
            _const_spec((1, D_MODEL)), _const_spec((D_MODEL, D_MODEL)), _const_spec((PLE_DIM, D_MODEL)),
        ],
        out_specs=rows(D_MODEL),
        compiler_params=pltpu.CompilerParams(
            dimension_semantics=("arbitrary",), vmem_limit_bytes=VMEM_LIMIT_BYTES),
        name="post_even",
    )(x2, y_ret, y_att, p2, w_out_bf, w_out_bf, pg, wg_bf, wp_bf)


def _post_odd(x2, cu, gz, conv_w, p2, w_out_bf, pg, wg_bf, wp_bf, seq):
    m = x2.shape[0]
    tm = POST_TM
    rows = lambda w: pl.BlockSpec((tm, w), lambda i: (i, 0))
    halo_spec = pl.BlockSpec((HALO, D_MODEL), lambda i: (jnp.maximum(i * (tm // HALO) - 1, 0), 0))
    return pl.pallas_call(
        functools.partial(_post_odd_kernel, tiles_per_seq=seq // tm),
        out_shape=jax.ShapeDtypeStruct((m, D_MODEL), F32),
        grid=(m // tm,),
        in_specs=[
            rows(D_MODEL), rows(D_MODEL), halo_spec, rows(D_MODEL),
            _const_spec((CONV_WIDTH, D_MODEL)), rows(PLE_DIM),
            _const_spec((D_MODEL, D_MODEL)),
            _const_spec((1, D_MODEL)), _const_spec((D_MODEL, D_MODEL)), _const_spec((PLE_DIM, D_MODEL)),
        ],
        out_specs=rows(D_MODEL),
        scratch_shapes=[pltpu.VMEM((HALO + tm, D_MODEL), F32)],
        compiler_params=pltpu.CompilerParams(
            dimension_semantics=("arbitrary",), vmem_limit_bytes=VMEM_LIMIT_BYTES),
        name="post_odd",
    )(x2, cu, cu, gz, conv_w, p2, w_out_bf, pg, wg_bf, wp_bf)


def _in_odd_kernel(x_ref, g_ref, wb_ref, wc_ref, wu_ref, wz_ref, cu_ref, gz_ref, hn_ref):
    @pl.when(pl.program_id(1) == 0)
    def _():
        hn_ref[...] = _rms_rows(x_ref[...], g_ref[...]).astype(BF16)

    hn = hn_ref[...]
    cg = jnp.dot(hn, wc_ref[...], preferred_element_type=F32)
    u = jnp.dot(hn, wu_ref[...], preferred_element_type=F32)
    cu_ref[...] = (cg * u).astype(BF16)
    bg = jnp.dot(hn, wb_ref[...], preferred_element_type=F32)
    z = jnp.dot(hn, wz_ref[...], preferred_element_type=F32)
    gz_ref[...] = (bg * _silu(z)).astype(BF16)


def _in_odd(x2, g, w_bf):
    m = x2.shape[0]
    tm, tc = IN_TM, IN_ODD_TC
    nb = D_MODEL // tc
    wspec = lambda seg: pl.BlockSpec((D_MODEL, tc), lambda i, j: (0, seg * nb + j))
    out = jax.ShapeDtypeStruct((m, D_MODEL), BF16)
    return pl.pallas_call(
        _in_odd_kernel,
        out_shape=(out, out),
        grid=(m // tm, nb),
        in_specs=[
            pl.BlockSpec((tm, D_MODEL), lambda i, j: (i, 0)),
            pl.BlockSpec((1, D_MODEL), lambda i, j: (0, 0)),
            wspec(0), wspec(1), wspec(2), wspec(3),
        ],
        out_specs=(pl.BlockSpec((tm, tc), lambda i, j: (i, j)), pl.BlockSpec((tm, tc), lambda i, j: (i, j))),
        scratch_shapes=[pltpu.VMEM((tm, D_MODEL), BF16)],
        compiler_params=pltpu.CompilerParams(
            dimension_semantics=("arbitrary", "arbitrary"), vmem_limit_bytes=VMEM_LIMIT_BYTES),
        name="in_odd",
    )(x2, g, w_bf, w_bf, w_bf, w_bf)


def _even_gain_row(q_g, k_g):
    ones = jnp.ones((OFF_AQ,), F32)
    return jnp.concatenate([
        ones,
        jnp.tile(q_g * ATT_HD ** -0.5, ATT_HEADS),
        jnp.tile(k_g, ATT_HEADS),
        jnp.ones((IN_WIDTH_EVEN - OFF_AV,), F32),
    ]).reshape(1, IN_WIDTH_EVEN)


def kernel(x, p, pre_norm_g, w_in_even, q_norm_g, k_norm_g, ret_gn_g, w_out_even, w_in_odd, conv_w_odd,
           w_out_odd, ple_norm_g, w_ple_gate, w_ple_proj):
    batch, seq, d = x.shape
    depth = p.shape[0]
    assert d == D_MODEL and seq % (ATT_BLOCK * DILATED_PATTERNS[-1][1]) == 0 and seq % IN_TM == 0
    m = batch * seq
    slopes = jnp.asarray([(2.0 ** (-8.0 / ATT_HEADS)) ** (i + 1) for i in range(ATT_HEADS)], F32)
    x2 = x.reshape(m, d)
    for i in range(depth):
        j = i // 2
        p2 = p[i].reshape(m, PLE_DIM)
        pg = ple_norm_g[i].reshape(1, d)
        wg_bf = w_ple_gate[i].astype(BF16)
        wp_bf = w_ple_proj[i].astype(BF16)
        g = pre_norm_g[i].reshape(1, d)
        if i % 2 == 0:
            proj = _in_even(x2, g, w_in_even[j].astype(BF16), _even_gain_row(q_norm_g[j], k_norm_g[j]))
            y_ret = _retention(proj, ret_gn_g[j], batch, seq)
            y_att = _attention(proj, slopes, batch, seq)
            x2 = _post_even(x2, y_ret, y_att, p2, w_out_even[j].astype(BF16), pg, wg_bf, wp_bf)
        else:
            cu, gz = _in_odd(x2, g, w_in_odd[j].astype(BF16))
            x2 = _post_odd(x2, cu, gz, conv_w_odd[j], p2, w_out_odd[j].astype(BF16), pg, wg_bf, wp_bf, seq)
    return x2.reshape(batch, seq, d)
```

```python
import functools
import math

import jax
import jax.numpy as jnp
from jax import lax
from jax.experimental import pallas as pl
from jax.experimental.pallas import tpu as pltpu

F32 = jnp.float32
BF16 = jnp.bfloat16

D_MODEL = 2048
PLE_DIM = 256
EPS = 1e-6
NEG = -1e30
RET_DK = 128
RET_DV = 256
RET_HEADS = 4
RET_CHUNK = 256
ATT_HD = 128
ATT_HEADS = 8
ATT_BLOCK = 128
DILATED_PATTERNS = ((128, 1), (512, 4), (2048, 16))
RET_QK_WIDTH = RET_HEADS * RET_DK
RET_WIDTH = RET_HEADS * RET_DV
ATT_WIDTH = ATT_HEADS * ATT_HD
IN_WIDTH_EVEN = 2 * RET_QK_WIDTH + RET_WIDTH + 3 * ATT_WIDTH + RET_WIDTH + ATT_WIDTH
CONV_WIDTH = 3

OFF_RQ = 0
OFF_RK = RET_QK_WIDTH
OFF_RV = 2 * RET_QK_WIDTH
OFF_AQ = OFF_RV + RET_WIDTH
OFF_AK = OFF_AQ + ATT_WIDTH
OFF_AV = OFF_AK + ATT_WIDTH
OFF_Z = OFF_AV + ATT_WIDTH

VMEM_LIMIT_BYTES = 56 * 1024 * 1024

IN_TM = 1024
IN_EVEN_TN = 1024
IN_ODD_TC = 256
RET_TQ = 1024
ATT_GROUP = 4
POST_TM = 256
POST_NC = 512
HALO = 8


def _rms_rows(x, g):
    ms = jnp.mean(x * x, axis=-1, keepdims=True)
    return x * lax.rsqrt(ms + EPS) * g


def _silu(z):
    return z * jax.nn.sigmoid(z)


def _in_even_kernel(x_ref, g_ref, w_ref, gain_ref, o_ref, hn_ref):
    j = pl.program_id(1)

    @pl.when(j == 0)
    def _():
        hn_ref[...] = _rms_rows(x_ref[...], g_ref[...]).astype(BF16)

    acc = jnp.dot(hn_ref[...], w_ref[...], preferred_element_type=F32)
    tn = acc.shape[1]
    j_aq = OFF_AQ // tn
    j_av = OFF_AV // tn
    j_z = OFF_Z // tn

    @pl.when((j < j_aq) | (j == j_av))
    def _():
        o_ref[...] = acc.astype(BF16)

    @pl.when((j >= j_aq) & (j < j_av))
    def _():
        for hh in range(tn // ATT_HD):
            sl = slice(hh * ATT_HD, (hh + 1) * ATT_HD)
            o_ref[:, sl] = _rms_rows(acc[:, sl], gain_ref[:, sl]).astype(BF16)

    @pl.when(j >= j_z)
    def _():
        o_ref[...] = _silu(acc).astype(BF16)


def _in_even(x2, g, w_bf, gain_row):
    m = x2.shape[0]
    tm, tn = IN_TM, IN_EVEN_TN
    return pl.pallas_call(
        _in_even_kernel,
        out_shape=jax.ShapeDtypeStruct((m, IN_WIDTH_EVEN), BF16),
        grid=(m // tm, IN_WIDTH_EVEN // tn),
        in_specs=[
            pl.BlockSpec((tm, D_MODEL), lambda i, j: (i, 0)),
            pl.BlockSpec((1, D_MODEL), lambda i, j: (0, 0)),
            pl.BlockSpec((D_MODEL, tn), lambda i, j: (0, j)),
            pl.BlockSpec((1, tn), lambda i, j: (0, j)),
        ],
        out_specs=pl.BlockSpec((tm, tn), lambda i, j: (i, j)),
        scratch_shapes=[pltpu.VMEM((tm, D_MODEL), BF16)],
        compiler_params=pltpu.CompilerParams(
            dimension_semantics=("arbitrary", "arbitrary"), vmem_limit_bytes=VMEM_LIMIT_BYTES),
        name="in_even",
    )(x2, g, w_bf, gain_row)


def _ret_kernel(q_ref, k_ref, v_ref, z_ref, gn_ref, o_ref, r_ref):
    c = RET_CHUNK
    tq = q_ref.shape[0]

    @pl.when(pl.program_id(1) == 0)
    def _():
        r_ref[...] = jnp.zeros(r_ref.shape, F32)

    row = lax.broadcasted_iota(jnp.int32, (c, c), 0)
    col = lax.broadcasted_iota(jnp.int32, (c, c), 1)
    diff = (row - col).astype(F32)
    pos = lax.broadcasted_iota(jnp.int32, (c, 1), 0).astype(F32)
    kscale = RET_DK ** -0.5

    for h in range(RET_HEADS):
        lg = math.log(1.0 - 2.0 ** (-5.0 - h))
        decay = jnp.where(diff >= 0, jnp.exp(lg * jnp.maximum(diff, 0.0)), 0.0) * kscale
        xi = jnp.exp(lg * (pos + 1.0))
        zeta = jnp.exp(lg * (c - 1.0 - pos)) * kscale
        g_chunk = math.exp(lg * c)
        qk_sl = slice(h * RET_DK, (h + 1) * RET_DK)
        v_sl = slice(h * RET_DV, (h + 1) * RET_DV)
        gn = gn_ref[h:h + 1, :]
        chunks = [slice(ci * c, (ci + 1) * c) for ci in range(tq // c)]

        contrib = []
        for rows in chunks:
            kz_t = (k_ref[rows, qk_sl].astype(F32) * zeta).T.astype(BF16)
            contrib.append(jnp.dot(kz_t, v_ref[rows, v_sl], preferred_element_type=F32))

        state = r_ref[h]
        states = []
        for s_c in contrib:
            states.append(state)
            state = g_chunk * state + s_c
        r_ref[h] = state

        for rows, r_prev in zip(chunks, states):
            q = q_ref[rows, qk_sl]
            v = v_ref[rows, v_sl]
            s = lax.dot_general(q, k_ref[rows, qk_sl], (((1,), (1,)), ((), ())),
                                preferred_element_type=F32) * decay
            y = jnp.dot(s.astype(BF16), v, preferred_element_type=F32)
            y = y + jnp.dot(q, r_prev.astype(BF16), preferred_element_type=F32) * xi
            mu = jnp.mean(y, axis=-1, keepdims=True)
            yc = y - mu
            var = jnp.mean(yc * yc, axis=-1, keepdims=True)
            yn = yc * lax.rsqrt(var + EPS) * gn
            o_ref[rows, v_sl] = (yn * z_ref[rows, v_sl].astype(F32)).astype(BF16)


def _retention(proj, gn_g, batch, seq):
    m = proj.shape[0]
    tq = RET_TQ
    nt = seq // tq
    row = lambda b, t: b * nt + t
    return pl.pallas_call(
        _ret_kernel,
        out_shape=jax.ShapeDtypeStruct((m, RET_WIDTH), BF16),
        grid=(batch, nt),
        in_specs=[
            pl.BlockSpec((tq, RET_QK_WIDTH), lambda b, t: (row(b, t), OFF_RQ // RET_QK_WIDTH)),
            pl.BlockSpec((tq, RET_QK_WIDTH), lambda b, t: (row(b, t), OFF_RK // RET_QK_WIDTH)),
            pl.BlockSpec((tq, RET_WIDTH), lambda b, t: (row(b, t), OFF_RV // RET_WIDTH)),
            pl.BlockSpec((tq, RET_WIDTH), lambda b, t: (row(b, t), OFF_Z // RET_WIDTH)),
            pl.BlockSpec((RET_HEADS, RET_DV), lambda b, t: (0, 0)),
        ],
        out_specs=pl.BlockSpec((tq, RET_WIDTH), lambda b, t: (row(b, t), 0)),
        scratch_shapes=[pltpu.VMEM((RET_HEADS, RET_DK, RET_DV), F32)],
        compiler_params=pltpu.CompilerParams(
            dimension_semantics=("arbitrary", "arbitrary"), vmem_limit_bytes=VMEM_LIMIT_BYTES),
        name="retention",
    )(proj, proj, proj, proj, gn_g)


def _att_kernel(slopes_ref, q_ref, k_ref, v_ref, z_ref, o_ref, qf, kf, vf, q4, k4, v4, qp, kp, vp, bias_ref,
                s_ref, pn_ref, o0, o1, o2, l0, l1, l2):
    blk = ATT_BLOCK
    grp = ATT_GROUP
    seq = q_ref.shape[0]
    nblk = seq // blk
    n_it = nblk // grp
    slope = slopes_ref[pl.program_id(1)]
    qf[...] = q_ref[...].astype(F32)
    kf[...] = k_ref[...].astype(F32)
    vf[...] = v_ref[...].astype(F32)
    kp[0:blk, :] = jnp.zeros((blk, ATT_HD), BF16)
    vp[0:blk, :] = jnp.zeros((blk, ATT_HD), BF16)

    qi = lax.broadcasted_iota(jnp.int32, (blk, 2 * blk), 0)
    kj = lax.broadcasted_iota(jnp.int32, (blk, 2 * blk), 1)
    rel = qi - kj + blk
    nt = (((1,), (1,)), ((), ()))

    def natural_rows(bi, dil):
        nb = nblk // dil
        start = (bi // nb) + (bi % nb) * (blk * dil)
        if dil == 1:
            return pl.ds(pl.multiple_of(start, blk), blk)
        return pl.ds(start, blk, stride=dil)

    def permute_1():
        qp[...] = q_ref[...]
        kp[blk:, :] = k_ref[...]
        vp[blk:, :] = v_ref[...]

    def permute_4():
        def step(bi, carry):
            dst = pl.ds(pl.multiple_of(bi * blk, blk), blk)
            dst_pad = pl.ds(pl.multiple_of(bi * blk + blk, blk), blk)
            src = natural_rows(bi, 4)
            q, k, v = qf[src, :], kf[src, :], vf[src, :]
            q4[dst, :], k4[dst, :], v4[dst, :] = q, k, v
            qp[dst, :] = q.astype(BF16)
            kp[dst_pad, :] = k.astype(BF16)
            vp[dst_pad, :] = v.astype(BF16)
            return carry
        lax.fori_loop(0, nblk, step, 0)

    def permute_16():
        nb = nblk // 16

        def step(bi, carry):
            dst = pl.ds(pl.multiple_of(bi * blk, blk), blk)
            dst_pad = pl.ds(pl.multiple_of(bi * blk + blk, blk), blk)
            r16 = bi // nb
            start = (r16 % 4) * (seq // 4) + r16 // 4 + (bi % nb) * (4 * blk)
            src = pl.ds(start, blk, stride=4)
            qp[dst, :] = q4[src, :].astype(BF16)
            kp[dst_pad, :] = k4[src, :].astype(BF16)
            vp[dst_pad, :] = v4[src, :].astype(BF16)
            return carry
        lax.fori_loop(0, nblk, step, 0)

    for (window, dil), permute, o_g, l_g in zip(DILATED_PATTERNS, (permute_1, permute_4, permute_16),
                                                (o0, o1, o2), (l0, l1, l2)):
        steps = window // dil
        nb = nblk // dil
        valid = (rel >= 0) & (rel <= steps)
        alibi = -slope * (rel * dil).astype(F32)
        bias_ref[0] = jnp.where(valid & (kj >= blk), alibi, NEG)
        bias_ref[1] = jnp.where(valid, alibi, NEG)
        permute()

        def scores(it):
            for g in range(grp):
                off = pl.multiple_of((it * grp + g) * blk, blk)
                kb = kp[pl.ds(off, 2 * blk), :]
                s_ref[it % 2, g] = lax.dot_general(qp[pl.ds(off, blk), :], kb, nt, preferred_element_type=F32)

        def probs(it, dil=dil, nb=nb, l_g=l_g):
            for g in range(grp):
                bi = it * grp + g
                s = s_ref[it % 2, g] + bias_ref[jnp.minimum(bi % nb, 1)]
                mx = jnp.max(jnp.maximum(s[:, :blk], s[:, blk:]), axis=-1, keepdims=True)
                pr = jnp.exp(s - mx)
                den = jnp.sum(pr[:, :blk] + pr[:, blk:], axis=-1, keepdims=True)
                pn_ref[it % 2, g] = (pr * (1.0 / den)).astype(BF16)
                l_g[natural_rows(bi, dil), :] = jnp.broadcast_to(mx + jnp.log(den), (blk, ATT_HD))

        def values(it, dil=dil, o_g=o_g):
            for g in range(grp):
                bi = it * grp + g
                vb = vp[pl.ds(pl.multiple_of(bi * blk, blk), 2 * blk), :]
                o_g[natural_rows(bi, dil), :] = jnp.dot(pn_ref[it % 2, g], vb, preferred_element_type=F32)

        scores(0)
        probs(0)
        scores(1)

        def body(it, carry, scores=scores, probs=probs, values=values):
            values(it - 2)
            probs(it - 1)
            scores(it)
            return carry

        lax.fori_loop(2, n_it, body, 0)
        values(n_it - 2)
        probs(n_it - 1)
        values(n_it - 1)

    mrows = 256

    def merge(ci, carry):
        rows = pl.ds(pl.multiple_of(ci * mrows, mrows), mrows)
        a0, a1, a2 = l0[rows, :], l1[rows, :], l2[rows, :]
        mx = jnp.maximum(jnp.maximum(a0, a1), a2)
        e0, e1, e2 = jnp.exp(a0 - mx), jnp.exp(a1 - mx), jnp.exp(a2 - mx)
        y = (e0 * o0[rows, :] + e1 * o1[rows, :] + e2 * o2[rows, :]) / (e0 + e1 + e2)
        o_ref[rows, :] = (y * z_ref[rows, :].astype(F32)).astype(BF16)
        return carry

    lax.fori_loop(0, seq // mrows, merge, 0)


def _attention(proj, slopes, batch, seq):
    m = proj.shape[0]
    hd = ATT_HD
    spec = lambda off: pl.BlockSpec((seq, hd), lambda b, h: (b, off // hd + h))
    f32_rows = pltpu.VMEM((seq, hd), F32)
    return pl.pallas_call(
        _att_kernel,
        out_shape=jax.ShapeDtypeStruct((m, ATT_WIDTH), BF16),
        grid=(batch, ATT_HEADS),
        in_specs=[
            pl.BlockSpec(memory_space=pltpu.SMEM),
            spec(OFF_AQ), spec(OFF_AK), spec(OFF_AV), spec(OFF_Z + RET_WIDTH),
        ],
        out_specs=pl.BlockSpec((seq, hd), lambda b, h: (b, h)),
        scratch_shapes=[
            f32_rows, f32_rows, f32_rows, f32_rows, f32_rows, f32_rows,
            pltpu.VMEM((seq, hd), BF16),
            pltpu.VMEM((seq + ATT_BLOCK, hd), BF16),
            pltpu.VMEM((seq + ATT_BLOCK, hd), BF16),
            pltpu.VMEM((2, ATT_BLOCK, 2 * ATT_BLOCK), F32),
            pltpu.VMEM((2, ATT_GROUP, ATT_BLOCK, 2 * ATT_BLOCK), F32),
            pltpu.VMEM((2, ATT_GROUP, ATT_BLOCK, 2 * ATT_BLOCK), BF16),
            f32_rows, f32_rows, f32_rows, f32_rows, f32_rows, f32_rows,
        ],
        compiler_params=pltpu.CompilerParams(
            dimension_semantics=("arbitrary", "arbitrary"), vmem_limit_bytes=VMEM_LIMIT_BYTES),
        name="dilated_attention",
    )(slopes, proj, proj, proj, proj)


def _post_tail(x1, p_ref, pg_ref, wg_ref, wp_ref, o_ref):
    h2 = _rms_rows(x1, pg_ref[...]).astype(BF16)
    pb = p_ref[...].astype(BF16)
    for cc in range(D_MODEL // POST_NC):
        sl = slice(cc * POST_NC, (cc + 1) * POST_NC)
        gate = jax.nn.sigmoid(jnp.dot(h2, wg_ref[:, sl], preferred_element_type=F32))
        pp = jnp.dot(pb, wp_ref[:, sl], preferred_element_type=F32)
        o_ref[:, sl] = x1[:, sl] + gate * pp


def _post_even_kernel(x_ref, ya_ref, yb_ref, p_ref, wa_ref, wb_ref, pg_ref, wg_ref, wp_ref, o_ref):
    x1 = x_ref[...] + jnp.dot(ya_ref[...], wa_ref[...], preferred_element_type=F32)
    x1 = x1 + jnp.dot(yb_ref[...], wb_ref[...], preferred_element_type=F32)
    _post_tail(x1, p_ref, pg_ref, wg_ref, wp_ref, o_ref)


def _post_odd_kernel(x_ref, cu_ref, halo_ref, gz_ref, cw_ref, p_ref, wo_ref, pg_ref, wg_ref, wp_ref,
                     o_ref, ext_ref, *, tiles_per_seq):
    tm = x_ref.shape[0]
    first = (pl.program_id(0) % tiles_per_seq) == 0
    halo = halo_ref[...].astype(F32)
    ext_ref[0:HALO, :] = jnp.where(first, 0.0, halo)
    ext_ref[HALO:HALO + tm, :] = cu_ref[...].astype(F32)
    conv = (cw_ref[0:1, :] * ext_ref[HALO - 2:HALO - 2 + tm, :]
            + cw_ref[1:2, :] * ext_ref[HALO - 1:HALO - 1 + tm, :]
            + cw_ref[2:3, :] * ext_ref[HALO:HALO + tm, :])
    y = (gz_ref[...].astype(F32) * conv).astype(BF16)
    x1 = x_ref[...] + jnp.dot(y, wo_ref[...], preferred_element_type=F32)
    _post_tail(x1, p_ref, pg_ref, wg_ref, wp_ref, o_ref)


def _const_spec(shape):
    return pl.BlockSpec(shape, lambda i: (0,) * len(shape), pipeline_mode=pl.Buffered(1))


def _post_even(x2, y_ret, y_att, p2, w_out_bf, pg, wg_bf, wp_bf):
    m = x2.shape[0]
    tm = POST_TM
    half = D_MODEL // 2
    rows = lambda w: pl.BlockSpec((tm, w), lambda i: (i, 0))
    return pl.pallas_call(
        _post_even_kernel,
        out_shape=jax.ShapeDtypeStruct((m, D_MODEL), F32),
        grid=(m // tm,),
        in_specs=[
            rows(D_MODEL), rows(half), rows(half), rows(PLE_DIM),
            pl.BlockSpec((half, D_MODEL), lambda i: (0, 0), pipeline_mode=pl.Buffered(1)),
            pl.BlockSpec((half, D_MODEL), lambda i: (1, 0), pipeline_mode=pl.Buffered(1)),
            _const_spec((1, D_MODEL)), _const_spec((D_MODEL, D_MODEL)), _const_spec((PLE_DIM, D_MODEL)),
        ],
        out_specs=rows(D_MODEL),
        compiler_params=pltpu.CompilerParams(
            dimension_semantics=("arbitrary",), vmem_limit_bytes=VMEM_LIMIT_BYTES),
        name="post_even",
    )(x2, y_ret, y_att, p2, w_out_bf, w_out_bf, pg, wg_bf, wp_bf)


def _post_odd(x2, cu, gz, conv_w, p2, w_out_bf, pg, wg_bf, wp_bf, seq):
    m = x2.shape[0]
    tm = POST_TM
    rows = lambda w: pl.BlockSpec((tm, w), lambda i: (i, 0))
    halo_spec = pl.BlockSpec((HALO, D_MODEL), lambda i: (jnp.maximum(i * (tm // HALO) - 1, 0), 0))
    return pl.pallas_call(
        functools.partial(_post_odd_kernel, tiles_per_seq=seq // tm),
        out_shape=jax.ShapeDtypeStruct((m, D_MODEL), F32),
        grid=(m // tm,),
        in_specs=[
            rows(D_MODEL), rows(D_MODEL), halo_spec, rows(D_MODEL),
            _const_spec((CONV_WIDTH, D_MODEL)), rows(PLE_DIM),
            _const_spec((D_MODEL, D_MODEL)),
            _const_spec((1, D_MODEL)), _const_spec((D_MODEL, D_MODEL)), _const_spec((PLE_DIM, D_MODEL)),
        ],
        out_specs=rows(D_MODEL),
        scratch_shapes=[pltpu.VMEM((HALO + tm, D_MODEL), F32)],
        compiler_params=pltpu.CompilerParams(
            dimension_semantics=("arbitrary",), vmem_limit_bytes=VMEM_LIMIT_BYTES),
        name="post_odd",
    )(x2, cu, cu, gz, conv_w, p2, w_out_bf, pg, wg_bf, wp_bf)


def _in_odd_kernel(x_ref, g_ref, wb_ref, wc_ref, wu_ref, wz_ref, cu_ref, gz_ref, hn_ref):
    @pl.when(pl.program_id(1) == 0)
    def _():
        hn_ref[...] = _rms_rows(x_ref[...], g_ref[...]).astype(BF16)

    hn = hn_ref[...]
    cg = jnp.dot(hn, wc_ref[...], preferred_element_type=F32)
    u = jnp.dot(hn, wu_ref[...], preferred_element_type=F32)
    cu_ref[...] = (cg * u).astype(BF16)
    bg = jnp.dot(hn, wb_ref[...], preferred_element_type=F32)
    z = jnp.dot(hn, wz_ref[...], preferred_element_type=F32)
    gz_ref[...] = (bg * _silu(z)).astype(BF16)


def _in_odd(x2, g, w_bf):
    m = x2.shape[0]
    tm, tc = IN_TM, IN_ODD_TC
    nb = D_MODEL // tc
    wspec = lambda seg: pl.BlockSpec((D_MODEL, tc), lambda i, j: (0, seg * nb + j))
    out = jax.ShapeDtypeStruct((m, D_MODEL), BF16)
    return pl.pallas_call(
        _in_odd_kernel,
        out_shape=(out, out),
        grid=(m // tm, nb),
        in_specs=[
            pl.BlockSpec((tm, D_MODEL), lambda i, j: (i, 0)),
            pl.BlockSpec((1, D_MODEL), lambda i, j: (0, 0)),
            wspec(0), wspec(1), wspec(2), wspec(3),
        ],
        out_specs=(pl.BlockSpec((tm, tc), lambda i, j: (i, j)), pl.BlockSpec((tm, tc), lambda i, j: (i, j))),
        scratch_shapes=[pltpu.VMEM((tm, D_MODEL), BF16)],
        compiler_params=pltpu.CompilerParams(
            dimension_semantics=("arbitrary", "arbitrary"), vmem_limit_bytes=VMEM_LIMIT_BYTES),
        name="in_odd",
    )(x2, g, w_bf, w_bf, w_bf, w_bf)


def _even_gain_row(q_g, k_g):
    ones = jnp.ones((OFF_AQ,), F32)
    return jnp.concatenate([
        ones,
        jnp.tile(q_g * ATT_HD ** -0.5, ATT_HEADS),
        jnp.tile(k_g, ATT_HEADS),
        jnp.ones((IN_WIDTH_EVEN - OFF_AV,), F32),
    ]).reshape(1, IN_WIDTH_EVEN)


def kernel(x, p, pre_norm_g, w_in_even, q_norm_g, k_norm_g, ret_gn_g, w_out_even, w_in_odd, conv_w_odd,
           w_out_odd, ple_norm_g, w_ple_gate, w_ple_proj):
    batch, seq, d = x.shape
    depth = p.shape[0]
    assert d == D_MODEL and seq % (ATT_BLOCK * DILATED_PATTERNS[-1][1]) == 0 and seq % IN_TM == 0
    m = batch * seq
    slopes = jnp.asarray([(2.0 ** (-8.0 / ATT_HEADS)) ** (i + 1) for i in range(ATT_HEADS)], F32)
    x2 = x.reshape(m, d)
    for i in range(depth):
        j = i // 2
        p2 = p[i].reshape(m, PLE_DIM)
        pg = ple_norm_g[i].reshape(1, d)
        wg_bf = w_ple_gate[i].astype(BF16)
        wp_bf = w_ple_proj[i].astype(BF16)
        g = pre_norm_g[i].reshape(1, d)
        if i % 2 == 0:
            proj = _in_even(x2, g, w_in_even[j].astype(BF16), _even_gain_row(q_norm_g[j], k_norm_g[j]))
            y_ret = _retention(proj, ret_gn_g[j], batch, seq)
            y_att = _attention(proj, slopes, batch, seq)
            x2 = _post_even(x2, y_ret, y_att, p2, w_out_even[j].astype(BF16), pg, wg_bf, wp_bf)
        else:
            cu, gz = _in_odd(x2, g, w_in_odd[j].astype(BF16))
            x2 = _post_odd(x2, cu, gz, conv_w_odd[j], p2, w_out_odd[j].astype(BF16), pg, wg_bf, wp_bf, seq)
    return x2.reshape(batch, seq, d)
```

```python
import functools
import math

import jax
import jax.numpy as jnp
from jax import lax
from jax.experimental import pallas as pl
from jax.experimental.pallas import tpu as pltpu

F32 = jnp.float32
BF16 = jnp.bfloat16

D_MODEL = 2048
PLE_DIM = 256
EPS = 1e-6
NEG = -1e30
RET_DK = 128
RET_DV = 256
RET_HEADS = 4
RET_CHUNK = 256
ATT_HD = 128
ATT_HEADS = 8
ATT_BLOCK = 128
DILATED_PATTERNS = ((128, 1), (512, 4), (2048, 16))
RET_QK_WIDTH = RET_HEADS * RET_DK
RET_WIDTH = RET_HEADS * RET_DV
ATT_WIDTH = ATT_HEADS * ATT_HD
IN_WIDTH_EVEN = 2 * RET_QK_WIDTH + RET_WIDTH + 3 * ATT_WIDTH + RET_WIDTH + ATT_WIDTH
CONV_WIDTH = 3

OFF_RQ = 0
OFF_RK = RET_QK_WIDTH
OFF_RV = 2 * RET_QK_WIDTH
OFF_AQ = OFF_RV + RET_WIDTH
OFF_AK = OFF_AQ + ATT_WIDTH
OFF_AV = OFF_AK + ATT_WIDTH
OFF_Z = OFF_AV + ATT_WIDTH

VMEM_LIMIT_BYTES = 56 * 1024 * 1024

IN_TM = 1024
IN_EVEN_TN = 1024
IN_EVEN_NC = 256
IN_ODD_TC = 256
RET_TQ = 1024
ATT_GROUP = 4
POST_TM = 256
POST_NC = 512
POST_KC = 512
HALO = 8


def _rms_rows(x, g):
    ms = jnp.mean(x * x, axis=-1, keepdims=True)
    return x * lax.rsqrt(ms + EPS) * g


def _silu(z):
    return z * jax.nn.sigmoid(z)


def _silu_cols(z, cols):
    del cols
    return _silu(z)


def _in_even_kernel(x_ref, g_ref, w_ref, gain_ref, o_ref, hn_ref):
    j = pl.program_id(1)

    @pl.when(j == 0)
    def _():
        hn_ref[...] = _rms_rows(x_ref[...], g_ref[...]).astype(BF16)

    tn = o_ref.shape[1]
    j_aq, j_av, j_z = OFF_AQ // tn, OFF_AV // tn, OFF_Z // tn

    def tile(epilogue):
        hn = hn_ref[...]
        for cc in range(tn // IN_EVEN_NC):
            acc = jnp.dot(hn, w_ref[:, cc * IN_EVEN_NC:(cc + 1) * IN_EVEN_NC], preferred_element_type=F32)
            for hh in range(IN_EVEN_NC // ATT_HD):
                sl = slice(cc * IN_EVEN_NC + hh * ATT_HD, cc * IN_EVEN_NC + (hh + 1) * ATT_HD)
                o_ref[:, sl] = epilogue(acc[:, hh * ATT_HD:(hh + 1) * ATT_HD], sl).astype(BF16)

    @pl.when((j < j_aq) | (j == j_av))
    def _():
        tile(lambda a, sl: a)

    @pl.when((j >= j_aq) & (j < j_av))
    def _():
        tile(lambda a, sl: _rms_rows(a, gain_ref[:, sl]))

    @pl.when(j >= j_z)
    def _():
        tile(_silu_cols)


def _in_even(x2, g, w_bf, gain_row):
    m = x2.shape[0]
    tm, tn = IN_TM, IN_EVEN_TN
    return pl.pallas_call(
        _in_even_kernel,
        out_shape=jax.ShapeDtypeStruct((m, IN_WIDTH_EVEN), BF16),
        grid=(m // tm, IN_WIDTH_EVEN // tn),
        in_specs=[
            pl.BlockSpec((tm, D_MODEL), lambda i, j: (i, 0)),
            pl.BlockSpec((1, D_MODEL), lambda i, j: (0, 0)),
            pl.BlockSpec((D_MODEL, tn), lambda i, j: (0, j)),
            pl.BlockSpec((1, tn), lambda i, j: (0, j)),
        ],
        out_specs=pl.BlockSpec((tm, tn), lambda i, j: (i, j)),
        scratch_shapes=[pltpu.VMEM((tm, D_MODEL), BF16)],
        compiler_params=pltpu.CompilerParams(
            dimension_semantics=("arbitrary", "arbitrary"), vmem_limit_bytes=VMEM_LIMIT_BYTES),
        name="in_even",
    )(x2, g, w_bf, gain_row)


def _ret_kernel(q_ref, k_ref, v_ref, z_ref, gn_ref, o_ref, r_ref):
    c = RET_CHUNK
    tq = q_ref.shape[0]

    @pl.when(pl.program_id(1) == 0)
    def _():
        r_ref[...] = jnp.zeros(r_ref.shape, F32)

    row = lax.broadcasted_iota(jnp.int32, (c, c), 0)
    col = lax.broadcasted_iota(jnp.int32, (c, c), 1)
    diff = (row - col).astype(F32)
    pos = lax.broadcasted_iota(jnp.int32, (c, 1), 0).astype(F32)
    kscale = RET_DK ** -0.5

    for h in range(RET_HEADS):
        lg = math.log(1.0 - 2.0 ** (-5.0 - h))
        decay = jnp.where(diff >= 0, jnp.exp(lg * jnp.maximum(diff, 0.0)), 0.0) * kscale
        xi = jnp.exp(lg * (pos + 1.0))
        zeta = jnp.exp(lg * (c - 1.0 - pos)) * kscale
        g_chunk = math.exp(lg * c)
        qk_sl = slice(h * RET_DK, (h + 1) * RET_DK)
        v_sl = slice(h * RET_DV, (h + 1) * RET_DV)
        gn = gn_ref[h:h + 1, :]
        chunks = [slice(ci * c, (ci + 1) * c) for ci in range(tq // c)]

        contrib = []
        for rows in chunks:
            kz_t = (k_ref[rows, qk_sl].astype(F32) * zeta).T.astype(BF16)
            contrib.append(jnp.dot(kz_t, v_ref[rows, v_sl], preferred_element_type=F32))

        state = r_ref[h]
        states = []
        for s_c in contrib:
            states.append(state)
            state = g_chunk * state + s_c
        r_ref[h] = state

        for rows, r_prev in zip(chunks, states):
            q = q_ref[rows, qk_sl]
            v = v_ref[rows, v_sl]
            s = lax.dot_general(q, k_ref[rows, qk_sl], (((1,), (1,)), ((), ())),
                                preferred_element_type=F32) * decay
            y = jnp.dot(s.astype(BF16), v, preferred_element_type=F32)
            y = y + jnp.dot(q, r_prev.astype(BF16), preferred_element_type=F32) * xi
            mu = jnp.mean(y, axis=-1, keepdims=True)
            yc = y - mu
            var = jnp.mean(yc * yc, axis=-1, keepdims=True)
            yn = yc * lax.rsqrt(var + EPS) * gn
            o_ref[rows, v_sl] = (yn * z_ref[rows, v_sl].astype(F32)).astype(BF16)


def _retention(proj, gn_g, batch, seq):
    m = proj.shape[0]
    tq = RET_TQ
    nt = seq // tq
    row = lambda b, t: b * nt + t
    return pl.pallas_call(
        _ret_kernel,
        out_shape=jax.ShapeDtypeStruct((m, RET_WIDTH), BF16),
        grid=(batch, nt),
        in_specs=[
            pl.BlockSpec((tq, RET_QK_WIDTH), lambda b, t: (row(b, t), OFF_RQ // RET_QK_WIDTH)),
            pl.BlockSpec((tq, RET_QK_WIDTH), lambda b, t: (row(b, t), OFF_RK // RET_QK_WIDTH)),
            pl.BlockSpec((tq, RET_WIDTH), lambda b, t: (row(b, t), OFF_RV // RET_WIDTH)),
            pl.BlockSpec((tq, RET_WIDTH), lambda b, t: (row(b, t), OFF_Z // RET_WIDTH)),
            pl.BlockSpec((RET_HEADS, RET_DV), lambda b, t: (0, 0)),
        ],
        out_specs=pl.BlockSpec((tq, RET_WIDTH), lambda b, t: (row(b, t), 0)),
        scratch_shapes=[pltpu.VMEM((RET_HEADS, RET_DK, RET_DV), F32)],
        compiler_params=pltpu.CompilerParams(
            dimension_semantics=("arbitrary", "arbitrary"), vmem_limit_bytes=VMEM_LIMIT_BYTES),
        name="retention",
    )(proj, proj, proj, proj, gn_g)


def _att_kernel(slopes_ref, q_ref, k_ref, v_ref, z_ref, o_ref, qf, kf, vf, q4, k4, v4, qp, kp, vp, bias_ref,
                s_ref, pn_ref, o0, o1, o2, l0, l1, l2):
    blk = ATT_BLOCK
    grp = ATT_GROUP
    seq = q_ref.shape[0]
    nblk = seq // blk
    n_it = nblk // grp
    slope = slopes_ref[pl.program_id(1)]
    qf[...] = q_ref[...].astype(F32)
    kf[...] = k_ref[...].astype(F32)
    vf[...] = v_ref[...].astype(F32)
    kp[0:blk, :] = jnp.zeros((blk, ATT_HD), BF16)
    vp[0:blk, :] = jnp.zeros((blk, ATT_HD), BF16)

    qi = lax.broadcasted_iota(jnp.int32, (blk, 2 * blk), 0)
    kj = lax.broadcasted_iota(jnp.int32, (blk, 2 * blk), 1)
    rel = qi - kj + blk
    nt = (((1,), (1,)), ((), ()))

    def natural_rows(bi, dil):
        nb = nblk // dil
        start = (bi // nb) + (bi % nb) * (blk * dil)
        if dil == 1:
            return pl.ds(pl.multiple_of(start, blk), blk)
        return pl.ds(start, blk, stride=dil)

    def permute_1():
        qp[...] = q_ref[...]
        kp[blk:, :] = k_ref[...]
        vp[blk:, :] = v_ref[...]

    def permute_4():
        def step(bi, carry):
            dst = pl.ds(pl.multiple_of(bi * blk, blk), blk)
            dst_pad = pl.ds(pl.multiple_of(bi * blk + blk, blk), blk)
            src = natural_rows(bi, 4)
            q, k, v = qf[src, :], kf[src, :], vf[src, :]
            q4[dst, :], k4[dst, :], v4[dst, :] = q, k, v
            qp[dst, :] = q.astype(BF16)
            kp[dst_pad, :] = k.astype(BF16)
            vp[dst_pad, :] = v.astype(BF16)
            return carry
        lax.fori_loop(0, nblk, step, 0)

    def permute_16():
        nb = nblk // 16

        def step(bi, carry):
            dst = pl.ds(pl.multiple_of(bi * blk, blk), blk)
            dst_pad = pl.ds(pl.multiple_of(bi * blk + blk, blk), blk)
            r16 = bi // nb
            start = (r16 % 4) * (seq // 4) + r16 // 4 + (bi % nb) * (4 * blk)
            src = pl.ds(start, blk, stride=4)
            qp[dst, :] = q4[src, :].astype(BF16)
            kp[dst_pad, :] = k4[src, :].astype(BF16)
            vp[dst_pad, :] = v4[src, :].astype(BF16)
            return carry
        lax.fori_loop(0, nblk, step, 0)

    for (window, dil), permute, o_g, l_g in zip(DILATED_PATTERNS, (permute_1, permute_4, permute_16),
                                                (o0, o1, o2), (l0, l1, l2)):
        steps = window // dil
        nb = nblk // dil
        valid = (rel >= 0) & (rel <= steps)
        alibi = -slope * (rel * dil).astype(F32)
        bias_ref[0] = jnp.where(valid & (kj >= blk), alibi, NEG)
        bias_ref[1] = jnp.where(valid, alibi, NEG)
        permute()

        def scores(it):
            for g in range(grp):
                off = pl.multiple_of((it * grp + g) * blk, blk)
                kb = kp[pl.ds(off, 2 * blk), :]
                s_ref[it % 2, g] = lax.dot_general(qp[pl.ds(off, blk), :], kb, nt, preferred_element_type=F32)

        def probs(it, dil=dil, nb=nb, l_g=l_g):
            for g in range(grp):
                bi = it * grp + g
                s = s_ref[it % 2, g] + bias_ref[jnp.minimum(bi % nb, 1)]
                mx = jnp.max(jnp.maximum(s[:, :blk], s[:, blk:]), axis=-1, keepdims=True)
                pr = jnp.exp(s - mx)
                den = jnp.sum(pr[:, :blk] + pr[:, blk:], axis=-1, keepdims=True)
                pn_ref[it % 2, g] = (pr * (1.0 / den)).astype(BF16)
                l_g[natural_rows(bi, dil), :] = jnp.broadcast_to(mx + jnp.log(den), (blk, ATT_HD))

        def values(it, dil=dil, o_g=o_g):
            for g in range(grp):
                bi = it * grp + g
                vb = vp[pl.ds(pl.multiple_of(bi * blk, blk), 2 * blk), :]
                o_g[natural_rows(bi, dil), :] = jnp.dot(pn_ref[it % 2, g], vb, preferred_element_type=F32)

        scores(0)
        probs(0)
        scores(1)

        def body(it, carry, scores=scores, probs=probs, values=values):
            values(it - 2)
            probs(it - 1)
            scores(it)
            return carry

        lax.fori_loop(2, n_it, body, 0)
        values(n_it - 2)
        probs(n_it - 1)
        values(n_it - 1)

    mrows = 256

    def merge(ci, carry):
        rows = pl.ds(pl.multiple_of(ci * mrows, mrows), mrows)
        a0, a1, a2 = l0[rows, :], l1[rows, :], l2[rows, :]
        mx = jnp.maximum(jnp.maximum(a0, a1), a2)
        e0, e1, e2 = jnp.exp(a0 - mx), jnp.exp(a1 - mx), jnp.exp(a2 - mx)
        y = (e0 * o0[rows, :] + e1 * o1[rows, :] + e2 * o2[rows, :]) / (e0 + e1 + e2)
        o_ref[rows, :] = (y * z_ref[rows, :].astype(F32)).astype(BF16)
        return carry

    lax.fori_loop(0, seq // mrows, merge, 0)


def _attention(proj, slopes, batch, seq):
    m = proj.shape[0]
    hd = ATT_HD
    spec = lambda off: pl.BlockSpec((seq, hd), lambda b, h: (b, off // hd + h))
    f32_rows = pltpu.VMEM((seq, hd), F32)
    return pl.pallas_call(
        _att_kernel,
        out_shape=jax.ShapeDtypeStruct((m, ATT_WIDTH), BF16),
        grid=(batch, ATT_HEADS),
        in_specs=[
            pl.BlockSpec(memory_space=pltpu.SMEM),
            spec(OFF_AQ), spec(OFF_AK), spec(OFF_AV), spec(OFF_Z + RET_WIDTH),
        ],
        out_specs=pl.BlockSpec((seq, hd), lambda b, h: (b, h)),
        scratch_shapes=[
            f32_rows, f32_rows, f32_rows, f32_rows, f32_rows, f32_rows,
            pltpu.VMEM((seq, hd), BF16),
            pltpu.VMEM((seq + ATT_BLOCK, hd), BF16),
            pltpu.VMEM((seq + ATT_BLOCK, hd), BF16),
            pltpu.VMEM((2, ATT_BLOCK, 2 * ATT_BLOCK), F32),
            pltpu.VMEM((2, ATT_GROUP, ATT_BLOCK, 2 * ATT_BLOCK), F32),
            pltpu.VMEM((2, ATT_GROUP, ATT_BLOCK, 2 * ATT_BLOCK), BF16),
            f32_rows, f32_rows, f32_rows, f32_rows, f32_rows, f32_rows,
        ],
        compiler_params=pltpu.CompilerParams(
            dimension_semantics=("arbitrary", "arbitrary"), vmem_limit_bytes=VMEM_LIMIT_BYTES),
        name="dilated_attention",
    )(slopes, proj, proj, proj, proj)


def _post_tail(x1, p_ref, pg_ref, wg_ref, wp_ref, o_ref):
    h2 = _rms_rows(x1, pg_ref[...]).astype(BF16)
    pb = p_ref[...].astype(BF16)
    for cc in range(D_MODEL // POST_NC):
        sl = slice(cc * POST_NC, (cc + 1) * POST_NC)
        gate = jax.nn.sigmoid(jnp.dot(h2, wg_ref[:, sl], preferred_element_type=F32))
        pp = jnp.dot(pb, wp_ref[:, sl], preferred_element_type=F32)
        o_ref[:, sl] = x1[:, sl] + gate * pp


def _post_even_kernel(x_ref, ya_ref, yb_ref, p_ref, wa_ref, wb_ref, pg_ref, wg_ref, wp_ref, o_ref):
    x1 = x_ref[...] + jnp.dot(ya_ref[...], wa_ref[...], preferred_element_type=F32)
    x1 = x1 + jnp.dot(yb_ref[...], wb_ref[...], preferred_element_type=F32)
    _post_tail(x1, p_ref, pg_ref, wg_ref, wp_ref, o_ref)


def _post_odd_kernel(x_ref, cu_ref, halo_ref, gz_ref, cw_ref, p_ref, wo_ref, pg_ref, wg_ref, wp_ref,
                     o_ref, ext_ref, *, tiles_per_seq):
    tm = x_ref.shape[0]
    first = (pl.program_id(0) % tiles_per_seq) == 0
    x1 = x_ref[...]
    for kc in range(D_MODEL // POST_KC):
        sl = slice(kc * POST_KC, (kc + 1) * POST_KC)
        ext_ref[0:HALO, sl] = jnp.where(first, 0.0, halo_ref[:, sl].astype(F32))
        ext_ref[HALO:HALO + tm, sl] = cu_ref[:, sl].astype(F32)
        conv = (cw_ref[0:1, sl] * ext_ref[HALO - 2:HALO - 2 + tm, sl]
                + cw_ref[1:2, sl] * ext_ref[HALO - 1:HALO - 1 + tm, sl]
                + cw_ref[2:3, sl] * ext_ref[HALO:HALO + tm, sl])
        y = (gz_ref[:, sl].astype(F32) * conv).astype(BF16)
        x1 = x1 + jnp.dot(y, wo_ref[sl, :], preferred_element_type=F32)
    _post_tail(x1, p_ref, pg_ref, wg_ref, wp_ref, o_ref)


def _const_spec(shape):
    return pl.BlockSpec(shape, lambda i: (0,) * len(shape), pipeline_mode=pl.Buffered(1))


def _ple_rows_spec(layer, m, tm):
    return pl.BlockSpec((tm, PLE_DIM), lambda i: (layer * (m // tm) + i, 0))


def _post_even(x2, y_ret, y_att, p_all, layer, w_out_bf, pg, wg_bf, wp_bf):
    m = x2.shape[0]
    tm = POST_TM
    half = D_MODEL // 2
    rows = lambda w: pl.BlockSpec((tm, w), lambda i: (i, 0))
    return pl.pallas_call(
        _post_even_kernel,
        out_shape=jax.ShapeDtypeStruct((m, D_MODEL), F32),
        grid=(m // tm,),
        in_specs=[
            rows(D_MODEL), rows(half), rows(half), _ple_rows_spec(layer, m, tm),
            pl.BlockSpec((half, D_MODEL), lambda i: (0, 0), pipeline_mode=pl.Buffered(1)),
            pl.BlockSpec((half, D_MODEL), lambda i: (1, 0), pipeline_mode=pl.Buffered(1)),
            _const_spec((1, D_MODEL)), _const_spec((D_MODEL, D_MODEL)), _const_spec((PLE_DIM, D_MODEL)),
        ],
        out_specs=rows(D_MODEL),
        compiler_params=pltpu.CompilerParams(
            dimension_semantics=("arbitrary",), vmem_limit_bytes=VMEM_LIMIT_BYTES),
        name="post_even",
    )(x2, y_ret, y_att, p_all, w_out_bf, w_out_bf, pg, wg_bf, wp_bf)


def _post_odd(x2, cu, gz, conv_w, p_all, layer, w_out_bf, pg, wg_bf, wp_bf, seq):
    m = x2.shape[0]
    tm = POST_TM
    rows = lambda w: pl.BlockSpec((tm, w), lambda i: (i, 0))
    halo_spec = pl.BlockSpec((HALO, D_MODEL), lambda i: (jnp.maximum(i * (tm // HALO) - 1, 0), 0))
    return pl.pallas_call(
        functools.partial(_post_odd_kernel, tiles_per_seq=seq // tm),
        out_shape=jax.ShapeDtypeStruct((m, D_MODEL), F32),
        grid=(m // tm,),
        in_specs=[
            rows(D_MODEL), rows(D_MODEL), halo_spec, rows(D_MODEL),
            _const_spec((CONV_WIDTH, D_MODEL)), _ple_rows_spec(layer, m, tm),
            _const_spec((D_MODEL, D_MODEL)),
            _const_spec((1, D_MODEL)), _const_spec((D_MODEL, D_MODEL)), _const_spec((PLE_DIM, D_MODEL)),
        ],
        out_specs=rows(D_MODEL),
        scratch_shapes=[pltpu.VMEM((HALO + tm, D_MODEL), F32)],
        compiler_params=pltpu.CompilerParams(
            dimension_semantics=("arbitrary",), vmem_limit_bytes=VMEM_LIMIT_BYTES),
        name="post_odd",
    )(x2, cu, cu, gz, conv_w, p_all, w_out_bf, pg, wg_bf, wp_bf)


def _in_odd_kernel(x_ref, g_ref, wb_ref, wc_ref, wu_ref, wz_ref, cu_ref, gz_ref, hn_ref):
    @pl.when(pl.program_id(1) == 0)
    def _():
        hn_ref[...] = _rms_rows(x_ref[...], g_ref[...]).astype(BF16)

    hn = hn_ref[...]
    cg = jnp.dot(hn, wc_ref[...], preferred_element_type=F32)
    u = jnp.dot(hn, wu_ref[...], preferred_element_type=F32)
    cu_ref[...] = (cg * u).astype(BF16)
    bg = jnp.dot(hn, wb_ref[...], preferred_element_type=F32)
    z = jnp.dot(hn, wz_ref[...], preferred_element_type=F32)
    gz_ref[...] = (bg * _silu(z)).astype(BF16)


def _in_odd(x2, g, w_bf):
    m = x2.shape[0]
    tm, tc = IN_TM, IN_ODD_TC
    nb = D_MODEL // tc
    wspec = lambda seg: pl.BlockSpec((D_MODEL, tc), lambda i, j: (0, seg * nb + j))
    out = jax.ShapeDtypeStruct((m, D_MODEL), BF16)
    return pl.pallas_call(
        _in_odd_kernel,
        out_shape=(out, out),
        grid=(m // tm, nb),
        in_specs=[
            pl.BlockSpec((tm, D_MODEL), lambda i, j: (i, 0)),
            pl.BlockSpec((1, D_MODEL), lambda i, j: (0, 0)),
            wspec(0), wspec(1), wspec(2), wspec(3),
        ],
        out_specs=(pl.BlockSpec((tm, tc), lambda i, j: (i, j)), pl.BlockSpec((tm, tc), lambda i, j: (i, j))),
        scratch_shapes=[pltpu.VMEM((tm, D_MODEL), BF16)],
        compiler_params=pltpu.CompilerParams(
            dimension_semantics=("arbitrary", "arbitrary"), vmem_limit_bytes=VMEM_LIMIT_BYTES),
        name="in_odd",
    )(x2, g, w_bf, w_bf, w_bf, w_bf)


def _even_gain_row(q_g, k_g):
    ones = jnp.ones((OFF_AQ,), F32)
    return jnp.concatenate([
        ones,
        jnp.tile(q_g * ATT_HD ** -0.5, ATT_HEADS),
        jnp.tile(k_g, ATT_HEADS),
        jnp.ones((IN_WIDTH_EVEN - OFF_AV,), F32),
    ]).reshape(1, IN_WIDTH_EVEN)


def kernel(x, p, pre_norm_g, w_in_even, q_norm_g, k_norm_g, ret_gn_g, w_out_even, w_in_odd, conv_w_odd,
           w_out_odd, ple_norm_g, w_ple_gate, w_ple_proj):
    batch, seq, d = x.shape
    depth = p.shape[0]
    assert d == D_MODEL and seq % (ATT_BLOCK * DILATED_PATTERNS[-1][1]) == 0 and seq % IN_TM == 0
    m = batch * seq
    slopes = jnp.asarray([(2.0 ** (-8.0 / ATT_HEADS)) ** (i + 1) for i in range(ATT_HEADS)], F32)
    x2 = x.reshape(m, d)
    p_all = p.reshape(depth * m, PLE_DIM)
    for i in range(depth):
        j = i // 2
        pg = ple_norm_g[i].reshape(1, d)
        wg_bf = w_ple_gate[i].astype(BF16)
        wp_bf = w_ple_proj[i].astype(BF16)
        g = pre_norm_g[i].reshape(1, d)
        if i % 2 == 0:
            proj = _in_even(x2, g, w_in_even[j].astype(BF16), _even_gain_row(q_norm_g[j], k_norm_g[j]))
            y_ret = _retention(proj, ret_gn_g[j], batch, seq)
            y_att = _attention(proj, slopes, batch, seq)
            x2 = _post_even(x2, y_ret, y_att, p_all, i, w_out_even[j].astype(BF16), pg, wg_bf, wp_bf)
        else:
            cu, gz = _in_odd(x2, g, w_in_odd[j].astype(BF16))
            x2 = _post_odd(x2, cu, gz, conv_w_odd[j], p_all, i, w_out_odd[j].astype(BF16), pg, wg_bf, wp_bf,
                           seq)
    return x2.reshape(batch, seq, d)
```

```python
import functools
import math

import jax
import jax.numpy as jnp
from jax import lax
from jax.experimental import pallas as pl
from jax.experimental.pallas import tpu as pltpu

F32 = jnp.float32
BF16 = jnp.bfloat16

D_MODEL = 2048
PLE_DIM = 256
EPS = 1e-6
NEG = -1e30
RET_DK = 128
RET_DV = 256
RET_HEADS = 4
RET_CHUNK = 256
ATT_HD = 128
ATT_HEADS = 8
ATT_BLOCK = 128
DILATED_PATTERNS = ((128, 1), (512, 4), (2048, 16))
RET_QK_WIDTH = RET_HEADS * RET_DK
RET_WIDTH = RET_HEADS * RET_DV
ATT_WIDTH = ATT_HEADS * ATT_HD
IN_WIDTH_EVEN = 2 * RET_QK_WIDTH + RET_WIDTH + 3 * ATT_WIDTH + RET_WIDTH + ATT_WIDTH
CONV_WIDTH = 3

OFF_RQ = 0
OFF_RK = RET_QK_WIDTH
OFF_RV = 2 * RET_QK_WIDTH
OFF_AQ = OFF_RV + RET_WIDTH
OFF_AK = OFF_AQ + ATT_WIDTH
OFF_AV = OFF_AK + ATT_WIDTH
OFF_Z = OFF_AV + ATT_WIDTH

VMEM_LIMIT_BYTES = 56 * 1024 * 1024

IN_TM = 1024
IN_EVEN_TN = 1024
IN_EVEN_NC = 256
IN_ODD_TC = 256
RET_TQ = 1024
ATT_GROUP = 4
POST_TM = 256
POST_NC = 512
HALO = 8


def _rms_rows(x, g):
    ms = jnp.mean(x * x, axis=-1, keepdims=True)
    return x * lax.rsqrt(ms + EPS) * g


def _silu(z):
    return z * jax.nn.sigmoid(z)


def _silu_cols(z, cols):
    del cols
    return _silu(z)


def _in_even_kernel(x_ref, g_ref, w_ref, gain_ref, o_ref, hn_ref):
    j = pl.program_id(1)

    @pl.when(j == 0)
    def _():
        hn_ref[...] = _rms_rows(x_ref[...], g_ref[...]).astype(BF16)

    tn = o_ref.shape[1]
    j_aq, j_av, j_z = OFF_AQ // tn, OFF_AV // tn, OFF_Z // tn

    def tile(epilogue):
        hn = hn_ref[...]
        for cc in range(tn // IN_EVEN_NC):
            acc = jnp.dot(hn, w_ref[:, cc * IN_EVEN_NC:(cc + 1) * IN_EVEN_NC], preferred_element_type=F32)
            for hh in range(IN_EVEN_NC // ATT_HD):
                sl = slice(cc * IN_EVEN_NC + hh * ATT_HD, cc * IN_EVEN_NC + (hh + 1) * ATT_HD)
                o_ref[:, sl] = epilogue(acc[:, hh * ATT_HD:(hh + 1) * ATT_HD], sl).astype(BF16)

    @pl.when((j < j_aq) | (j == j_av))
    def _():
        tile(lambda a, sl: a)

    @pl.when((j >= j_aq) & (j < j_av))
    def _():
        tile(lambda a, sl: _rms_rows(a, gain_ref[:, sl]))

    @pl.when(j >= j_z)
    def _():
        tile(_silu_cols)


def _in_even(x2, g, w_bf, gain_row):
    m = x2.shape[0]
    tm, tn = IN_TM, IN_EVEN_TN
    return pl.pallas_call(
        _in_even_kernel,
        out_shape=jax.ShapeDtypeStruct((m, IN_WIDTH_EVEN), BF16),
        grid=(m // tm, IN_WIDTH_EVEN // tn),
        in_specs=[
            pl.BlockSpec((tm, D_MODEL), lambda i, j: (i, 0)),
            pl.BlockSpec((1, D_MODEL), lambda i, j: (0, 0)),
            pl.BlockSpec((D_MODEL, tn), lambda i, j: (0, j)),
            pl.BlockSpec((1, tn), lambda i, j: (0, j)),
        ],
        out_specs=pl.BlockSpec((tm, tn), lambda i, j: (i, j)),
        scratch_shapes=[pltpu.VMEM((tm, D_MODEL), BF16)],
        compiler_params=pltpu.CompilerParams(
            dimension_semantics=("arbitrary", "arbitrary"), vmem_limit_bytes=VMEM_LIMIT_BYTES),
        name="in_even",
    )(x2, g, w_bf, gain_row)


def _ret_kernel(q_ref, k_ref, v_ref, z_ref, gn_ref, o_ref, r_ref):
    c = RET_CHUNK
    tq = q_ref.shape[0]

    @pl.when(pl.program_id(1) == 0)
    def _():
        r_ref[...] = jnp.zeros(r_ref.shape, F32)

    row = lax.broadcasted_iota(jnp.int32, (c, c), 0)
    col = lax.broadcasted_iota(jnp.int32, (c, c), 1)
    diff = (row - col).astype(F32)
    pos = lax.broadcasted_iota(jnp.int32, (c, 1), 0).astype(F32)
    kscale = RET_DK ** -0.5

    for h in range(RET_HEADS):
        lg = math.log(1.0 - 2.0 ** (-5.0 - h))
        decay = jnp.where(diff >= 0, jnp.exp(lg * jnp.maximum(diff, 0.0)), 0.0) * kscale
        xi = jnp.exp(lg * (pos + 1.0))
        zeta = jnp.exp(lg * (c - 1.0 - pos)) * kscale
        g_chunk = math.exp(lg * c)
        qk_sl = slice(h * RET_DK, (h + 1) * RET_DK)
        v_sl = slice(h * RET_DV, (h + 1) * RET_DV)
        gn = gn_ref[h:h + 1, :]
        chunks = [slice(ci * c, (ci + 1) * c) for ci in range(tq // c)]

        contrib = []
        for rows in chunks:
            kz_t = (k_ref[rows, qk_sl].astype(F32) * zeta).T.astype(BF16)
            contrib.append(jnp.dot(kz_t, v_ref[rows, v_sl], preferred_element_type=F32))

        state = r_ref[h]
        states = []
        for s_c in contrib:
            states.append(state)
            state = g_chunk * state + s_c
        r_ref[h] = state

        for rows, r_prev in zip(chunks, states):
            q = q_ref[rows, qk_sl]
            v = v_ref[rows, v_sl]
            s = lax.dot_general(q, k_ref[rows, qk_sl], (((1,), (1,)), ((), ())),
                                preferred_element_type=F32) * decay
            y = jnp.dot(s.astype(BF16), v, preferred_element_type=F32)
            y = y + jnp.dot(q, r_prev.astype(BF16), preferred_element_type=F32) * xi
            mu = jnp.mean(y, axis=-1, keepdims=True)
            yc = y - mu
            var = jnp.mean(yc * yc, axis=-1, keepdims=True)
            yn = yc * lax.rsqrt(var + EPS) * gn
            o_ref[rows, v_sl] = (yn * z_ref[rows, v_sl].astype(F32)).astype(BF16)


def _retention(proj, gn_g, batch, seq):
    m = proj.shape[0]
    tq = RET_TQ
    nt = seq // tq
    row = lambda b, t: b * nt + t
    return pl.pallas_call(
        _ret_kernel,
        out_shape=jax.ShapeDtypeStruct((m, RET_WIDTH), BF16),
        grid=(batch, nt),
        in_specs=[
            pl.BlockSpec((tq, RET_QK_WIDTH), lambda b, t: (row(b, t), OFF_RQ // RET_QK_WIDTH)),
            pl.BlockSpec((tq, RET_QK_WIDTH), lambda b, t: (row(b, t), OFF_RK // RET_QK_WIDTH)),
            pl.BlockSpec((tq, RET_WIDTH), lambda b, t: (row(b, t), OFF_RV // RET_WIDTH)),
            pl.BlockSpec((tq, RET_WIDTH), lambda b, t: (row(b, t), OFF_Z // RET_WIDTH)),
            pl.BlockSpec((RET_HEADS, RET_DV), lambda b, t: (0, 0)),
        ],
        out_specs=pl.BlockSpec((tq, RET_WIDTH), lambda b, t: (row(b, t), 0)),
        scratch_shapes=[pltpu.VMEM((RET_HEADS, RET_DK, RET_DV), F32)],
        compiler_params=pltpu.CompilerParams(
            dimension_semantics=("arbitrary", "arbitrary"), vmem_limit_bytes=VMEM_LIMIT_BYTES),
        name="retention",
    )(proj, proj, proj, proj, gn_g)


def _att_kernel(slopes_ref, q_ref, k_ref, v_ref, z_ref, o_ref, qf, kf, vf, q4, k4, v4, qp, kp, vp, bias_ref,
                s_ref, pn_ref, o0, o1, o2, l0, l1, l2):
    blk = ATT_BLOCK
    grp = ATT_GROUP
    seq = q_ref.shape[0]
    nblk = seq // blk
    n_it = nblk // grp
    slope = slopes_ref[pl.program_id(1)]
    qf[...] = q_ref[...].astype(F32)
    kf[...] = k_ref[...].astype(F32)
    vf[...] = v_ref[...].astype(F32)
    kp[0:blk, :] = jnp.zeros((blk, ATT_HD), BF16)
    vp[0:blk, :] = jnp.zeros((blk, ATT_HD), BF16)

    qi = lax.broadcasted_iota(jnp.int32, (blk, 2 * blk), 0)
    kj = lax.broadcasted_iota(jnp.int32, (blk, 2 * blk), 1)
    rel = qi - kj + blk
    nt = (((1,), (1,)), ((), ()))

    def natural_rows(bi, dil):
        nb = nblk // dil
        start = (bi // nb) + (bi % nb) * (blk * dil)
        if dil == 1:
            return pl.ds(pl.multiple_of(start, blk), blk)
        return pl.ds(start, blk, stride=dil)

    def permute_1():
        qp[...] = q_ref[...]
        kp[blk:, :] = k_ref[...]
        vp[blk:, :] = v_ref[...]

    def permute_4():
        def step(bi, carry):
            dst = pl.ds(pl.multiple_of(bi * blk, blk), blk)
            dst_pad = pl.ds(pl.multiple_of(bi * blk + blk, blk), blk)
            src = natural_rows(bi, 4)
            q, k, v = qf[src, :], kf[src, :], vf[src, :]
            q4[dst, :], k4[dst, :], v4[dst, :] = q, k, v
            qp[dst, :] = q.astype(BF16)
            kp[dst_pad, :] = k.astype(BF16)
            vp[dst_pad, :] = v.astype(BF16)
            return carry
        lax.fori_loop(0, nblk, step, 0)

    def permute_16():
        nb = nblk // 16

        def step(bi, carry):
            dst = pl.ds(pl.multiple_of(bi * blk, blk), blk)
            dst_pad = pl.ds(pl.multiple_of(bi * blk + blk, blk), blk)
            r16 = bi // nb
            start = (r16 % 4) * (seq // 4) + r16 // 4 + (bi % nb) * (4 * blk)
            src = pl.ds(start, blk, stride=4)
            qp[dst, :] = q4[src, :].astype(BF16)
            kp[dst_pad, :] = k4[src, :].astype(BF16)
            vp[dst_pad, :] = v4[src, :].astype(BF16)
            return carry
        lax.fori_loop(0, nblk, step, 0)

    for (window, dil), permute, o_g, l_g in zip(DILATED_PATTERNS, (permute_1, permute_4, permute_16),
                                                (o0, o1, o2), (l0, l1, l2)):
        steps = window // dil
        nb = nblk // dil
        valid = (rel >= 0) & (rel <= steps)
        alibi = -slope * (rel * dil).astype(F32)
        bias_ref[0] = jnp.where(valid & (kj >= blk), alibi, NEG)
        bias_ref[1] = jnp.where(valid, alibi, NEG)
        permute()

        def scores(it):
            for g in range(grp):
                off = pl.multiple_of((it * grp + g) * blk, blk)
                kb = kp[pl.ds(off, 2 * blk), :]
                s_ref[it % 2, g] = lax.dot_general(qp[pl.ds(off, blk), :], kb, nt, preferred_element_type=F32)

        def probs(it, dil=dil, nb=nb, l_g=l_g):
            for g in range(grp):
                bi = it * grp + g
                s = s_ref[it % 2, g] + bias_ref[jnp.minimum(bi % nb, 1)]
                mx = jnp.max(jnp.maximum(s[:, :blk], s[:, blk:]), axis=-1, keepdims=True)
                pr = jnp.exp(s - mx)
                den = jnp.sum(pr[:, :blk] + pr[:, blk:], axis=-1, keepdims=True)
                pn_ref[it % 2, g] = (pr * (1.0 / den)).astype(BF16)
                l_g[natural_rows(bi, dil), :] = jnp.broadcast_to(mx + jnp.log(den), (blk, ATT_HD))

        def values(it, dil=dil, o_g=o_g):
            for g in range(grp):
                bi = it * grp + g
                vb = vp[pl.ds(pl.multiple_of(bi * blk, blk), 2 * blk), :]
                o_g[natural_rows(bi, dil), :] = jnp.dot(pn_ref[it % 2, g], vb, preferred_element_type=F32)

        scores(0)
        probs(0)
        scores(1)

        def body(it, carry, scores=scores, probs=probs, values=values):
            values(it - 2)
            probs(it - 1)
            scores(it)
            return carry

        lax.fori_loop(2, n_it, body, 0)
        values(n_it - 2)
        probs(n_it - 1)
        values(n_it - 1)

    mrows = 256

    def merge(ci, carry):
        rows = pl.ds(pl.multiple_of(ci * mrows, mrows), mrows)
        a0, a1, a2 = l0[rows, :], l1[rows, :], l2[rows, :]
        mx = jnp.maximum(jnp.maximum(a0, a1), a2)
        e0, e1, e2 = jnp.exp(a0 - mx), jnp.exp(a1 - mx), jnp.exp(a2 - mx)
        y = (e0 * o0[rows, :] + e1 * o1[rows, :] + e2 * o2[rows, :]) / (e0 + e1 + e2)
        o_ref[rows, :] = (y * z_ref[rows, :].astype(F32)).astype(BF16)
        return carry

    lax.fori_loop(0, seq // mrows, merge, 0)


def _attention(proj, slopes, batch, seq):
    m = proj.shape[0]
    hd = ATT_HD
    spec = lambda off: pl.BlockSpec((seq, hd), lambda b, h: (b, off // hd + h))
    f32_rows = pltpu.VMEM((seq, hd), F32)
    return pl.pallas_call(
        _att_kernel,
        out_shape=jax.ShapeDtypeStruct((m, ATT_WIDTH), BF16),
        grid=(batch, ATT_HEADS),
        in_specs=[
            pl.BlockSpec(memory_space=pltpu.SMEM),
            spec(OFF_AQ), spec(OFF_AK), spec(OFF_AV), spec(OFF_Z + RET_WIDTH),
        ],
        out_specs=pl.BlockSpec((seq, hd), lambda b, h: (b, h)),
        scratch_shapes=[
            f32_rows, f32_rows, f32_rows, f32_rows, f32_rows, f32_rows,
            pltpu.VMEM((seq, hd), BF16),
            pltpu.VMEM((seq + ATT_BLOCK, hd), BF16),
            pltpu.VMEM((seq + ATT_BLOCK, hd), BF16),
            pltpu.VMEM((2, ATT_BLOCK, 2 * ATT_BLOCK), F32),
            pltpu.VMEM((2, ATT_GROUP, ATT_BLOCK, 2 * ATT_BLOCK), F32),
            pltpu.VMEM((2, ATT_GROUP, ATT_BLOCK, 2 * ATT_BLOCK), BF16),
            f32_rows, f32_rows, f32_rows, f32_rows, f32_rows, f32_rows,
        ],
        compiler_params=pltpu.CompilerParams(
            dimension_semantics=("arbitrary", "arbitrary"), vmem_limit_bytes=VMEM_LIMIT_BYTES),
        name="dilated_attention",
    )(slopes, proj, proj, proj, proj)


def _post_tail(x1, p_ref, pg_ref, wg_ref, wp_ref, o_ref):
    h2 = _rms_rows(x1, pg_ref[...]).astype(BF16)
    pb = p_ref[...].astype(BF16)
    for cc in range(D_MODEL // POST_NC):
        sl = slice(cc * POST_NC, (cc + 1) * POST_NC)
        gate = jax.nn.sigmoid(jnp.dot(h2, wg_ref[:, sl], preferred_element_type=F32))
        pp = jnp.dot(pb, wp_ref[:, sl], preferred_element_type=F32)
        o_ref[:, sl] = x1[:, sl] + gate * pp


def _post_kernel(x_ref, ya_ref, yb_ref, p_ref, wa_ref, wb_ref, pg_ref, wg_ref, wp_ref, o_ref):
    x1 = x_ref[...] + jnp.dot(ya_ref[...], wa_ref[...], preferred_element_type=F32)
    x1 = x1 + jnp.dot(yb_ref[...], wb_ref[...], preferred_element_type=F32)
    _post_tail(x1, p_ref, pg_ref, wg_ref, wp_ref, o_ref)


def _const_spec(shape):
    return pl.BlockSpec(shape, lambda i: (0,) * len(shape), pipeline_mode=pl.Buffered(1))


def _ple_rows_spec(layer, m, tm):
    return pl.BlockSpec((tm, PLE_DIM), lambda i: (layer * (m // tm) + i, 0))


def _post(x2, ya, ya_blk, yb, yb_blk, p_all, layer, w_out_bf, pg, wg_bf, wp_bf, name):
    m = x2.shape[0]
    tm = POST_TM
    half = D_MODEL // 2
    rows = lambda w, blk=0: pl.BlockSpec((tm, w), lambda i: (i, blk))
    return pl.pallas_call(
        _post_kernel,
        out_shape=jax.ShapeDtypeStruct((m, D_MODEL), F32),
        grid=(m // tm,),
        in_specs=[
            rows(D_MODEL), rows(half, ya_blk), rows(half, yb_blk), _ple_rows_spec(layer, m, tm),
            pl.BlockSpec((half, D_MODEL), lambda i: (0, 0), pipeline_mode=pl.Buffered(1)),
            pl.BlockSpec((half, D_MODEL), lambda i: (1, 0), pipeline_mode=pl.Buffered(1)),
            _const_spec((1, D_MODEL)), _const_spec((D_MODEL, D_MODEL)), _const_spec((PLE_DIM, D_MODEL)),
        ],
        out_specs=rows(D_MODEL),
        compiler_params=pltpu.CompilerParams(
            dimension_semantics=("arbitrary",), vmem_limit_bytes=VMEM_LIMIT_BYTES),
        name=name,
    )(x2, ya, yb, p_all, w_out_bf, w_out_bf, pg, wg_bf, wp_bf)


def _in_odd_kernel(x_ref, g_ref, wb_ref, wc_ref, wu_ref, wz_ref, cw_ref, y_ref, hn_ref, ext_ref, halo_ref, *,
                   tiles_per_seq):
    i, j = pl.program_id(0), pl.program_id(1)
    tm = x_ref.shape[0]

    @pl.when(j == 0)
    def _():
        hn_ref[...] = _rms_rows(x_ref[...], g_ref[...]).astype(BF16)

    @pl.when((i == 0) & (j == 0))
    def _():
        halo_ref[...] = jnp.zeros(halo_ref.shape, F32)

    hn = hn_ref[...]
    cu = (jnp.dot(hn, wc_ref[...], preferred_element_type=F32)
          * jnp.dot(hn, wu_ref[...], preferred_element_type=F32))
    first = (i % tiles_per_seq) == 0
    ext_ref[0:HALO, :] = jnp.where(first, 0.0, halo_ref[j])
    ext_ref[HALO:HALO + tm, :] = cu
    halo_ref[j] = cu[tm - HALO:, :]
    conv = (cw_ref[0:1, :] * ext_ref[HALO - 2:HALO - 2 + tm, :]
            + cw_ref[1:2, :] * ext_ref[HALO - 1:HALO - 1 + tm, :]
            + cw_ref[2:3, :] * cu)
    bg = jnp.dot(hn, wb_ref[...], preferred_element_type=F32)
    z = jnp.dot(hn, wz_ref[...], preferred_element_type=F32)
    y_ref[...] = (bg * conv * _silu(z)).astype(BF16)


def _in_odd(x2, g, w_bf, conv_w, seq):
    m = x2.shape[0]
    tm, tc = IN_TM, IN_ODD_TC
    nb = D_MODEL // tc
    wspec = lambda seg: pl.BlockSpec((D_MODEL, tc), lambda i, j: (0, seg * nb + j))
    return pl.pallas_call(
        functools.partial(_in_odd_kernel, tiles_per_seq=seq // tm),
        out_shape=jax.ShapeDtypeStruct((m, D_MODEL), BF16),
        grid=(m // tm, nb),
        in_specs=[
            pl.BlockSpec((tm, D_MODEL), lambda i, j: (i, 0)),
            pl.BlockSpec((1, D_MODEL), lambda i, j: (0, 0)),
            wspec(0), wspec(1), wspec(2), wspec(3),
            pl.BlockSpec((CONV_WIDTH, tc), lambda i, j: (0, j)),
        ],
        out_specs=pl.BlockSpec((tm, tc), lambda i, j: (i, j)),
        scratch_shapes=[
            pltpu.VMEM((tm, D_MODEL), BF16),
            pltpu.VMEM((HALO + tm, tc), F32),
            pltpu.VMEM((nb, HALO, tc), F32),
        ],
        compiler_params=pltpu.CompilerParams(
            dimension_semantics=("arbitrary", "arbitrary"), vmem_limit_bytes=VMEM_LIMIT_BYTES),
        name="in_odd",
    )(x2, g, w_bf, w_bf, w_bf, w_bf, conv_w)


def _even_gain_row(q_g, k_g):
    ones = jnp.ones((OFF_AQ,), F32)
    return jnp.concatenate([
        ones,
        jnp.tile(q_g * ATT_HD ** -0.5, ATT_HEADS),
        jnp.tile(k_g, ATT_HEADS),
        jnp.ones((IN_WIDTH_EVEN - OFF_AV,), F32),
    ]).reshape(1, IN_WIDTH_EVEN)


def kernel(x, p, pre_norm_g, w_in_even, q_norm_g, k_norm_g, ret_gn_g, w_out_even, w_in_odd, conv_w_odd,
           w_out_odd, ple_norm_g, w_ple_gate, w_ple_proj):
    batch, seq, d = x.shape
    depth = p.shape[0]
    assert d == D_MODEL and seq % (ATT_BLOCK * DILATED_PATTERNS[-1][1]) == 0 and seq % IN_TM == 0
    m = batch * seq
    slopes = jnp.asarray([(2.0 ** (-8.0 / ATT_HEADS)) ** (i + 1) for i in range(ATT_HEADS)], F32)
    x2 = x.reshape(m, d)
    p_all = p.reshape(depth * m, PLE_DIM)
    for i in range(depth):
        j = i // 2
        pg = ple_norm_g[i].reshape(1, d)
        wg_bf = w_ple_gate[i].astype(BF16)
        wp_bf = w_ple_proj[i].astype(BF16)
        g = pre_norm_g[i].reshape(1, d)
        if i % 2 == 0:
            proj = _in_even(x2, g, w_in_even[j].astype(BF16), _even_gain_row(q_norm_g[j], k_norm_g[j]))
            y_ret = _retention(proj, ret_gn_g[j], batch, seq)
            y_att = _attention(proj, slopes, batch, seq)
            x2 = _post(x2, y_ret, 0, y_att, 0, p_all, i, w_out_even[j].astype(BF16), pg, wg_bf, wp_bf,
                       "post_even")
        else:
            y = _in_odd(x2, g, w_in_odd[j].astype(BF16), conv_w_odd[j], seq)
            x2 = _post(x2, y, 0, y, 1, p_all, i, w_out_odd[j].astype(BF16), pg, wg_bf, wp_bf, "post_odd")
    return x2.reshape(batch, seq, d)
```

```python
import functools
import math

import jax
import jax.numpy as jnp
from jax import lax
from jax.experimental import pallas as pl
from jax.experimental.pallas import tpu as pltpu

F32 = jnp.float32
BF16 = jnp.bfloat16

D_MODEL = 2048
PLE_DIM = 256
EPS = 1e-6
NEG = -1e30
RET_DK = 128
RET_DV = 256
RET_HEADS = 4
RET_CHUNK = 256
ATT_HD = 128
ATT_HEADS = 8
ATT_BLOCK = 128
DILATED_PATTERNS = ((128, 1), (512, 4), (2048, 16))
RET_QK_WIDTH = RET_HEADS * RET_DK
RET_WIDTH = RET_HEADS * RET_DV
ATT_WIDTH = ATT_HEADS * ATT_HD
IN_WIDTH_EVEN = 2 * RET_QK_WIDTH + RET_WIDTH + 3 * ATT_WIDTH + RET_WIDTH + ATT_WIDTH
CONV_WIDTH = 3

OFF_RQ = 0
OFF_RK = RET_QK_WIDTH
OFF_RV = 2 * RET_QK_WIDTH
OFF_AQ = OFF_RV + RET_WIDTH
OFF_AK = OFF_AQ + ATT_WIDTH
OFF_AV = OFF_AK + ATT_WIDTH
OFF_Z = OFF_AV + ATT_WIDTH

VMEM_LIMIT_BYTES = 56 * 1024 * 1024

IN_TM = 1024
IN_EVEN_TN = 1024
IN_EVEN_NC = 256
IN_ODD_TC = 256
RET_TQ = 1024
ATT_GROUP = 4
POST_TM = 512
POST_NC = 512
HALO = 8


def _rms_rows(x, g):
    ms = jnp.mean(x * x, axis=-1, keepdims=True)
    return x * lax.rsqrt(ms + EPS) * g


def _silu(z):
    return z * jax.nn.sigmoid(z)


def _silu_cols(z, cols):
    del cols
    return _silu(z)


def _in_even_kernel(x_ref, g_ref, w_ref, gain_ref, o_ref, hn_ref):
    j = pl.program_id(1)

    @pl.when(j == 0)
    def _():
        hn_ref[...] = _rms_rows(x_ref[...], g_ref[...]).astype(BF16)

    tn = o_ref.shape[1]
    j_aq, j_av, j_z = OFF_AQ // tn, OFF_AV // tn, OFF_Z // tn

    def tile(epilogue):
        hn = hn_ref[...]
        for cc in range(tn // IN_EVEN_NC):
            acc = jnp.dot(hn, w_ref[:, cc * IN_EVEN_NC:(cc + 1) * IN_EVEN_NC], preferred_element_type=F32)
            for hh in range(IN_EVEN_NC // ATT_HD):
                sl = slice(cc * IN_EVEN_NC + hh * ATT_HD, cc * IN_EVEN_NC + (hh + 1) * ATT_HD)
                o_ref[:, sl] = epilogue(acc[:, hh * ATT_HD:(hh + 1) * ATT_HD], sl).astype(BF16)

    @pl.when((j < j_aq) | (j == j_av))
    def _():
        tile(lambda a, sl: a)

    @pl.when((j >= j_aq) & (j < j_av))
    def _():
        tile(lambda a, sl: _rms_rows(a, gain_ref[:, sl]))

    @pl.when(j >= j_z)
    def _():
        tile(_silu_cols)


def _in_even(x2, g, w_bf, gain_row):
    m = x2.shape[0]
    tm, tn = IN_TM, IN_EVEN_TN
    return pl.pallas_call(
        _in_even_kernel,
        out_shape=jax.ShapeDtypeStruct((m, IN_WIDTH_EVEN), BF16),
        grid=(m // tm, IN_WIDTH_EVEN // tn),
        in_specs=[
            pl.BlockSpec((tm, D_MODEL), lambda i, j: (i, 0)),
            pl.BlockSpec((1, D_MODEL), lambda i, j: (0, 0)),
            pl.BlockSpec((D_MODEL, tn), lambda i, j: (0, j)),
            pl.BlockSpec((1, tn), lambda i, j: (0, j)),
        ],
        out_specs=pl.BlockSpec((tm, tn), lambda i, j: (i, j)),
        scratch_shapes=[pltpu.VMEM((tm, D_MODEL), BF16)],
        compiler_params=pltpu.CompilerParams(
            dimension_semantics=("arbitrary", "arbitrary"), vmem_limit_bytes=VMEM_LIMIT_BYTES),
        name="in_even",
    )(x2, g, w_bf, gain_row)


def _ret_kernel(q_ref, k_ref, v_ref, z_ref, gn_ref, o_ref, r_ref):
    c = RET_CHUNK
    tq = q_ref.shape[0]

    @pl.when(pl.program_id(1) == 0)
    def _():
        r_ref[...] = jnp.zeros(r_ref.shape, F32)

    row = lax.broadcasted_iota(jnp.int32, (c, c), 0)
    col = lax.broadcasted_iota(jnp.int32, (c, c), 1)
    diff = (row - col).astype(F32)
    pos = lax.broadcasted_iota(jnp.int32, (c, 1), 0).astype(F32)
    kscale = RET_DK ** -0.5

    for h in range(RET_HEADS):
        lg = math.log(1.0 - 2.0 ** (-5.0 - h))
        decay = jnp.where(diff >= 0, jnp.exp(lg * jnp.maximum(diff, 0.0)), 0.0) * kscale
        xi = jnp.exp(lg * (pos + 1.0))
        zeta = jnp.exp(lg * (c - 1.0 - pos)) * kscale
        g_chunk = math.exp(lg * c)
        qk_sl = slice(h * RET_DK, (h + 1) * RET_DK)
        v_sl = slice(h * RET_DV, (h + 1) * RET_DV)
        gn = gn_ref[h:h + 1, :]
        chunks = [slice(ci * c, (ci + 1) * c) for ci in range(tq // c)]

        contrib = []
        for rows in chunks:
            kz_t = (k_ref[rows, qk_sl].astype(F32) * zeta).T.astype(BF16)
            contrib.append(jnp.dot(kz_t, v_ref[rows, v_sl], preferred_element_type=F32))

        state = r_ref[h]
        states = []
        for s_c in contrib:
            states.append(state)
            state = g_chunk * state + s_c
        r_ref[h] = state

        for rows, r_prev in zip(chunks, states):
            q = q_ref[rows, qk_sl]
            v = v_ref[rows, v_sl]
            s = lax.dot_general(q, k_ref[rows, qk_sl], (((1,), (1,)), ((), ())),
                                preferred_element_type=F32) * decay
            y = jnp.dot(s.astype(BF16), v, preferred_element_type=F32)
            y = y + jnp.dot(q, r_prev.astype(BF16), preferred_element_type=F32) * xi
            mu = jnp.mean(y, axis=-1, keepdims=True)
            yc = y - mu
            var = jnp.mean(yc * yc, axis=-1, keepdims=True)
            yn = yc * lax.rsqrt(var + EPS) * gn
            o_ref[rows, v_sl] = (yn * z_ref[rows, v_sl].astype(F32)).astype(BF16)


def _retention(proj, gn_g, batch, seq):
    m = proj.shape[0]
    tq = RET_TQ
    nt = seq // tq
    row = lambda b, t: b * nt + t
    return pl.pallas_call(
        _ret_kernel,
        out_shape=jax.ShapeDtypeStruct((m, RET_WIDTH), BF16),
        grid=(batch, nt),
        in_specs=[
            pl.BlockSpec((tq, RET_QK_WIDTH), lambda b, t: (row(b, t), OFF_RQ // RET_QK_WIDTH)),
            pl.BlockSpec((tq, RET_QK_WIDTH), lambda b, t: (row(b, t), OFF_RK // RET_QK_WIDTH)),
            pl.BlockSpec((tq, RET_WIDTH), lambda b, t: (row(b, t), OFF_RV // RET_WIDTH)),
            pl.BlockSpec((tq, RET_WIDTH), lambda b, t: (row(b, t), OFF_Z // RET_WIDTH)),
            pl.BlockSpec((RET_HEADS, RET_DV), lambda b, t: (0, 0)),
        ],
        out_specs=pl.BlockSpec((tq, RET_WIDTH), lambda b, t: (row(b, t), 0)),
        scratch_shapes=[pltpu.VMEM((RET_HEADS, RET_DK, RET_DV), F32)],
        compiler_params=pltpu.CompilerParams(
            dimension_semantics=("arbitrary", "arbitrary"), vmem_limit_bytes=VMEM_LIMIT_BYTES),
        name="retention",
    )(proj, proj, proj, proj, gn_g)


def _att_kernel(slopes_ref, q_ref, k_ref, v_ref, z_ref, o_ref, stage_q, stage_k, stage_v, q4, k4, v4, qp4, qp16,
                kp1, kp4, kp16, vp1, vp4, vp16, bias_ref, s_ref, pn_ref, o0, o1, o2, l0, l1, l2):
    blk = ATT_BLOCK
    grp = ATT_GROUP
    seq = q_ref.shape[0]
    nblk = seq // blk
    n_it = nblk // grp
    assert grp == 4 and DILATED_PATTERNS == ((128, 1), (512, 4), (2048, 16))
    assert n_it % 2 == 0 and n_it >= 4 and nblk % 16 == 0
    slope = slopes_ref[pl.program_id(1)]
    nt = (((1,), (1,)), ((), ()))

    def natural_rows(bi, dil):
        nb = nblk // dil
        start = (bi // nb) + (bi % nb) * (blk * dil)
        if dil == 1:
            return pl.ds(pl.multiple_of(start, blk), blk)
        return pl.ds(start, blk, stride=dil)

    def block_rows(bi, lead=0):
        return pl.ds(pl.multiple_of(bi * blk + lead, blk), blk)

    for pad in (kp1, kp4, kp16, vp1, vp4, vp16):
        pad[0:blk, :] = jnp.zeros((blk, ATT_HD), BF16)
    kp1[blk:, :] = k_ref[...]
    vp1[blk:, :] = v_ref[...]

    def copy_4(it):
        rows = pl.ds(pl.multiple_of(it * (grp * blk), grp * blk), grp * blk)
        stage_q[...] = q_ref[rows, :].astype(F32)
        stage_k[...] = k_ref[rows, :].astype(F32)
        stage_v[...] = v_ref[rows, :].astype(F32)
        for r in range(4):
            bi = r * (nblk // 4) + it
            src = pl.ds(r, blk, stride=4)
            q, k, v = stage_q[src, :], stage_k[src, :], stage_v[src, :]
            q4[block_rows(bi), :], k4[block_rows(bi), :], v4[block_rows(bi), :] = q, k, v
            qp4[block_rows(bi), :] = q.astype(BF16)
            kp4[block_rows(bi, blk), :] = k.astype(BF16)
            vp4[block_rows(bi, blk), :] = v.astype(BF16)

    def copy_16(it):
        halves = (nblk // 4) // grp
        r4, n16 = it // halves, it % halves
        for c in range(4):
            bi = (r4 + 4 * c) * (nblk // 16) + n16
            src = pl.ds(r4 * (seq // 4) + n16 * (grp * blk) + c, blk, stride=4)
            qp16[block_rows(bi), :] = q4[src, :].astype(BF16)
            kp16[block_rows(bi, blk), :] = k4[src, :].astype(BF16)
            vp16[block_rows(bi, blk), :] = v4[src, :].astype(BF16)

    qi = lax.broadcasted_iota(jnp.int32, (blk, 2 * blk), 0)
    kj = lax.broadcasted_iota(jnp.int32, (blk, 2 * blk), 1)
    rel = qi - kj + blk
    for pi, (window, dil) in enumerate(DILATED_PATTERNS):
        valid = (rel >= 0) & (rel <= window // dil)
        alibi = -slope * (rel * dil).astype(F32)
        bias_ref[pi, 0] = jnp.where(valid & (kj >= blk), alibi, NEG)
        bias_ref[pi, 1] = jnp.where(valid, alibi, NEG)

    def make_stages(pi, dil, q_src, kp, vp, o_g, l_g, copy_next):
        nb = nblk // dil

        def scores(it):
            for g in range(grp):
                bi = it * grp + g
                kb = kp[pl.ds(pl.multiple_of(bi * blk, blk), 2 * blk), :]
                s_ref[it % 2, g] = lax.dot_general(q_src[block_rows(bi), :], kb, nt, preferred_element_type=F32)
            if copy_next is not None:
                copy_next(it)

        def probs(it):
            for g in range(grp):
                bi = it * grp + g
                s = s_ref[it % 2, g] + bias_ref[pi, jnp.minimum(bi % nb, 1)]
                mx = jnp.max(jnp.maximum(s[:, :blk], s[:, blk:]), axis=-1, keepdims=True)
                pr = jnp.exp(s - mx)
                den = jnp.sum(pr[:, :blk] + pr[:, blk:], axis=-1, keepdims=True)
                pn_ref[it % 2, g] = (pr * (1.0 / den)).astype(BF16)
                l_g[natural_rows(bi, dil), :] = jnp.broadcast_to(mx + jnp.log(den), (blk, ATT_HD))

        def values(it):
            for g in range(grp):
                bi = it * grp + g
                vb = vp[pl.ds(pl.multiple_of(bi * blk, blk), 2 * blk), :]
                o_g[natural_rows(bi, dil), :] = jnp.dot(pn_ref[it % 2, g], vb, preferred_element_type=F32)

        return scores, probs, values

    stages = [make_stages(pi, dil, *refs) for pi, ((_, dil), *refs) in enumerate(zip(
        DILATED_PATTERNS, (q_ref, qp4, qp16), (kp1, kp4, kp16), (vp1, vp4, vp16), (o0, o1, o2), (l0, l1, l2),
        (copy_4, copy_16, None)))]

    def fill(scores, probs, values):
        scores(0)
        probs(0)
        scores(1)

    def drain(scores, probs, values):
        values(n_it - 2)
        probs(n_it - 1)
        values(n_it - 1)

    fill(*stages[0])
    for pi, (scores, probs, values) in enumerate(stages):
        def body(it, carry, scores=scores, probs=probs, values=values):
            values(it - 2)
            probs(it - 1)
            scores(it)
            return carry

        lax.fori_loop(2, n_it, body, 0)
        drain(scores, probs, values)
        if pi + 1 < len(stages):
            fill(*stages[pi + 1])

    mrows = 256

    def merge(ci, carry):
        rows = pl.ds(pl.multiple_of(ci * mrows, mrows), mrows)
        a0, a1, a2 = l0[rows, :], l1[rows, :], l2[rows, :]
        mx = jnp.maximum(jnp.maximum(a0, a1), a2)
        e0, e1, e2 = jnp.exp(a0 - mx), jnp.exp(a1 - mx), jnp.exp(a2 - mx)
        y = (e0 * o0[rows, :] + e1 * o1[rows, :] + e2 * o2[rows, :]) / (e0 + e1 + e2)
        o_ref[rows, :] = (y * z_ref[rows, :].astype(F32)).astype(BF16)
        return carry

    lax.fori_loop(0, seq // mrows, merge, 0)


def _attention(proj, slopes, batch, seq):
    m = proj.shape[0]
    hd = ATT_HD
    spec = lambda off: pl.BlockSpec((seq, hd), lambda b, h: (b, off // hd + h))
    f32_rows = pltpu.VMEM((seq, hd), F32)
    bf16_rows = pltpu.VMEM((seq, hd), BF16)
    bf16_padded = pltpu.VMEM((seq + ATT_BLOCK, hd), BF16)
    f32_group = pltpu.VMEM((ATT_GROUP * ATT_BLOCK, hd), F32)
    return pl.pallas_call(
        _att_kernel,
        out_shape=jax.ShapeDtypeStruct((m, ATT_WIDTH), BF16),
        grid=(batch, ATT_HEADS),
        in_specs=[
            pl.BlockSpec(memory_space=pltpu.SMEM),
            spec(OFF_AQ), spec(OFF_AK), spec(OFF_AV), spec(OFF_Z + RET_WIDTH),
        ],
        out_specs=pl.BlockSpec((seq, hd), lambda b, h: (b, h)),
        scratch_shapes=[
            f32_group, f32_group, f32_group, f32_rows, f32_rows, f32_rows,
            bf16_rows, bf16_rows,
            bf16_padded, bf16_padded, bf16_padded, bf16_padded, bf16_padded, bf16_padded,
            pltpu.VMEM((len(DILATED_PATTERNS), 2, ATT_BLOCK, 2 * ATT_BLOCK), F32),
            pltpu.VMEM((2, ATT_GROUP, ATT_BLOCK, 2 * ATT_BLOCK), F32),
            pltpu.VMEM((2, ATT_GROUP, ATT_BLOCK, 2 * ATT_BLOCK), BF16),
            f32_rows, f32_rows, f32_rows, f32_rows, f32_rows, f32_rows,
        ],
        compiler_params=pltpu.CompilerParams(
            dimension_semantics=("arbitrary", "arbitrary"), vmem_limit_bytes=VMEM_LIMIT_BYTES),
        name="dilated_attention",
    )(slopes, proj, proj, proj, proj)


def _post_tail(x1, p_ref, pg_ref, wg_ref, wp_ref, o_ref):
    h2 = _rms_rows(x1, pg_ref[...]).astype(BF16)
    pb = p_ref[...].astype(BF16)
    for cc in range(D_MODEL // POST_NC):
        sl = slice(cc * POST_NC, (cc + 1) * POST_NC)
        gate = jax.nn.sigmoid(jnp.dot(h2, wg_ref[:, sl], preferred_element_type=F32))
        pp = jnp.dot(pb, wp_ref[:, sl], preferred_element_type=F32)
        o_ref[:, sl] = x1[:, sl] + gate * pp


def _post_kernel(x_ref, ya_ref, yb_ref, p_ref, wa_ref, wb_ref, pg_ref, wg_ref, wp_ref, o_ref):
    x1 = x_ref[...] + jnp.dot(ya_ref[...], wa_ref[...], preferred_element_type=F32)
    x1 = x1 + jnp.dot(yb_ref[...], wb_ref[...], preferred_element_type=F32)
    _post_tail(x1, p_ref, pg_ref, wg_ref, wp_ref, o_ref)


def _const_spec(shape):
    return pl.BlockSpec(shape, lambda i: (0,) * len(shape), pipeline_mode=pl.Buffered(1))


def _ple_rows_spec(layer, m, tm):
    return pl.BlockSpec((tm, PLE_DIM), lambda i: (layer * (m // tm) + i, 0))


def _post(x2, ya, ya_blk, yb, yb_blk, p_all, layer, w_out_bf, pg, wg_bf, wp_bf, name):
    m = x2.shape[0]
    tm = POST_TM
    half = D_MODEL // 2
    rows = lambda w, blk=0: pl.BlockSpec((tm, w), lambda i: (i, blk))
    return pl.pallas_call(
        _post_kernel,
        out_shape=jax.ShapeDtypeStruct((m, D_MODEL), F32),
        grid=(m // tm,),
        in_specs=[
            rows(D_MODEL), rows(half, ya_blk), rows(half, yb_blk), _ple_rows_spec(layer, m, tm),
            pl.BlockSpec((half, D_MODEL), lambda i: (0, 0), pipeline_mode=pl.Buffered(1)),
            pl.BlockSpec((half, D_MODEL), lambda i: (1, 0), pipeline_mode=pl.Buffered(1)),
            _const_spec((1, D_MODEL)), _const_spec((D_MODEL, D_MODEL)), _const_spec((PLE_DIM, D_MODEL)),
        ],
        out_specs=rows(D_MODEL),
        compiler_params=pltpu.CompilerParams(
            dimension_semantics=("arbitrary",), vmem_limit_bytes=VMEM_LIMIT_BYTES),
        name=name,
    )(x2, ya, yb, p_all, w_out_bf, w_out_bf, pg, wg_bf, wp_bf)


def _in_odd_kernel(x_ref, g_ref, wb_ref, wc_ref, wu_ref, wz_ref, cw_ref, y_ref, hn_ref, ext_ref, halo_ref, *,
                   tiles_per_seq):
    i, j = pl.program_id(0), pl.program_id(1)
    tm = x_ref.shape[0]

    @pl.when(j == 0)
    def _():
        hn_ref[...] = _rms_rows(x_ref[...], g_ref[...]).astype(BF16)

    @pl.when((i == 0) & (j == 0))
    def _():
        halo_ref[...] = jnp.zeros(halo_ref.shape, F32)

    hn = hn_ref[...]
    cu = (jnp.dot(hn, wc_ref[...], preferred_element_type=F32)
          * jnp.dot(hn, wu_ref[...], preferred_element_type=F32))
    first = (i % tiles_per_seq) == 0
    ext_ref[0:HALO, :] = jnp.where(first, 0.0, halo_ref[j])
    ext_ref[HALO:HALO + tm, :] = cu
    halo_ref[j] = cu[tm - HALO:, :]
    conv = (cw_ref[0:1, :] * ext_ref[HALO - 2:HALO - 2 + tm, :]
            + cw_ref[1:2, :] * ext_ref[HALO - 1:HALO - 1 + tm, :]
            + cw_ref[2:3, :] * cu)
    bg = jnp.dot(hn, wb_ref[...], preferred_element_type=F32)
    z = jnp.dot(hn, wz_ref[...], preferred_element_type=F32)
    y_ref[...] = (bg * conv * _silu(z)).astype(BF16)


def _in_odd(x2, g, w_bf, conv_w, seq):
    m = x2.shape[0]
    tm, tc = IN_TM, IN_ODD_TC
    nb = D_MODEL // tc
    wspec = lambda seg: pl.BlockSpec((D_MODEL, tc), lambda i, j: (0, seg * nb + j))
    return pl.pallas_call(
        functools.partial(_in_odd_kernel, tiles_per_seq=seq // tm),
        out_shape=jax.ShapeDtypeStruct((m, D_MODEL), BF16),
        grid=(m // tm, nb),
        in_specs=[
            pl.BlockSpec((tm, D_MODEL), lambda i, j: (i, 0)),
            pl.BlockSpec((1, D_MODEL), lambda i, j: (0, 0)),
            wspec(0), wspec(1), wspec(2), wspec(3),
            pl.BlockSpec((CONV_WIDTH, tc), lambda i, j: (0, j)),
        ],
        out_specs=pl.BlockSpec((tm, tc), lambda i, j: (i, j)),
        scratch_shapes=[
            pltpu.VMEM((tm, D_MODEL), BF16),
            pltpu.VMEM((HALO + tm, tc), F32),
            pltpu.VMEM((nb, HALO, tc), F32),
        ],
        compiler_params=pltpu.CompilerParams(
            dimension_semantics=("arbitrary", "arbitrary"), vmem_limit_bytes=VMEM_LIMIT_BYTES),
        name="in_odd",
    )(x2, g, w_bf, w_bf, w_bf, w_bf, conv_w)


def _even_gain_row(q_g, k_g):
    ones = jnp.ones((OFF_AQ,), F32)
    return jnp.concatenate([
        ones,
        jnp.tile(q_g * ATT_HD ** -0.5, ATT_HEADS),
        jnp.tile(k_g, ATT_HEADS),
        jnp.ones((IN_WIDTH_EVEN - OFF_AV,), F32),
    ]).reshape(1, IN_WIDTH_EVEN)


def kernel(x, p, pre_norm_g, w_in_even, q_norm_g, k_norm_g, ret_gn_g, w_out_even, w_in_odd, conv_w_odd,
           w_out_odd, ple_norm_g, w_ple_gate, w_ple_proj):
    batch, seq, d = x.shape
    depth = p.shape[0]
    assert d == D_MODEL and seq % (ATT_BLOCK * DILATED_PATTERNS[-1][1]) == 0 and seq % IN_TM == 0
    m = batch * seq
    slopes = jnp.asarray([(2.0 ** (-8.0 / ATT_HEADS)) ** (i + 1) for i in range(ATT_HEADS)], F32)
    x2 = x.reshape(m, d)
    p_all = p.reshape(depth * m, PLE_DIM)
    for i in range(depth):
        j = i // 2
        pg = ple_norm_g[i].reshape(1, d)
        wg_bf = w_ple_gate[i].astype(BF16)
        wp_bf = w_ple_proj[i].astype(BF16)
        g = pre_norm_g[i].reshape(1, d)
        if i % 2 == 0:
            proj = _in_even(x2, g, w_in_even[j].astype(BF16), _even_gain_row(q_norm_g[j], k_norm_g[j]))
            y_ret = _retention(proj, ret_gn_g[j], batch, seq)
            y_att = _attention(proj, slopes, batch, seq)
            x2 = _post(x2, y_ret, 0, y_att, 0, p_all, i, w_out_even[j].astype(BF16), pg, wg_bf, wp_bf,
                       "post_even")
        else:
            y = _in_odd(x2, g, w_in_odd[j].astype(BF16), conv_w_odd[j], seq)
            x2 = _post(x2, y, 0, y, 1, p_all, i, w_out_odd[j].astype(BF16), pg, wg_bf, wp_bf, "post_odd")
    return x2.reshape(batch, seq, d)
```

```python
import functools
import math

import jax
import jax.numpy as jnp
from jax import lax
from jax.experimental import pallas as pl
from jax.experimental.pallas import tpu as pltpu

F32 = jnp.float32
BF16 = jnp.bfloat16

D_MODEL = 2048
PLE_DIM = 256
EPS = 1e-6
NEG = -1e30
RET_DK = 128
RET_DV = 256
RET_HEADS = 4
RET_CHUNK = 256
ATT_HD = 128
ATT_HEADS = 8
ATT_BLOCK = 128
DILATED_PATTERNS = ((128, 1), (512, 4), (2048, 16))
RET_QK_WIDTH = RET_HEADS * RET_DK
RET_WIDTH = RET_HEADS * RET_DV
ATT_WIDTH = ATT_HEADS * ATT_HD
IN_WIDTH_EVEN = 2 * RET_QK_WIDTH + RET_WIDTH + 3 * ATT_WIDTH + RET_WIDTH + ATT_WIDTH
CONV_WIDTH = 3

OFF_RQ = 0
OFF_RK = RET_QK_WIDTH
OFF_RV = 2 * RET_QK_WIDTH
OFF_AQ = OFF_RV + RET_WIDTH
OFF_AK = OFF_AQ + ATT_WIDTH
OFF_AV = OFF_AK + ATT_WIDTH
OFF_Z = OFF_AV + ATT_WIDTH

VMEM_LIMIT_BYTES = 56 * 1024 * 1024

IN_TM = 1024
IN_EVEN_TN = 1024
IN_EVEN_NC = 256
IN_ODD_TC = 256
RET_TQ = 1024
ATT_GROUP = 4
POST_TM = 512
POST_NC = 512
HALO = 8


def _rms_rows(x, g):
    ms = jnp.mean(x * x, axis=-1, keepdims=True)
    return x * lax.rsqrt(ms + EPS) * g


def _silu(z):
    return z * jax.nn.sigmoid(z)


def _silu_cols(z, cols):
    del cols
    return _silu(z)


def _in_even_kernel(x_ref, g_ref, w_ref, gain_ref, o_ref, hn_ref):
    j = pl.program_id(1)

    @pl.when(j == 0)
    def _():
        hn_ref[...] = _rms_rows(x_ref[...], g_ref[...]).astype(BF16)

    tn = o_ref.shape[1]
    j_aq, j_av, j_z = OFF_AQ // tn, OFF_AV // tn, OFF_Z // tn

    def tile(epilogue):
        hn = hn_ref[...]
        for cc in range(tn // IN_EVEN_NC):
            acc = jnp.dot(hn, w_ref[:, cc * IN_EVEN_NC:(cc + 1) * IN_EVEN_NC], preferred_element_type=F32)
            for hh in range(IN_EVEN_NC // ATT_HD):
                sl = slice(cc * IN_EVEN_NC + hh * ATT_HD, cc * IN_EVEN_NC + (hh + 1) * ATT_HD)
                o_ref[:, sl] = epilogue(acc[:, hh * ATT_HD:(hh + 1) * ATT_HD], sl).astype(BF16)

    @pl.when((j < j_aq) | (j == j_av))
    def _():
        tile(lambda a, sl: a)

    @pl.when((j >= j_aq) & (j < j_av))
    def _():
        tile(lambda a, sl: _rms_rows(a, gain_ref[:, sl]))

    @pl.when(j >= j_z)
    def _():
        tile(_silu_cols)


def _in_even(x2, g, w_bf, gain_row):
    m = x2.shape[0]
    tm, tn = IN_TM, IN_EVEN_TN
    return pl.pallas_call(
        _in_even_kernel,
        out_shape=jax.ShapeDtypeStruct((m, IN_WIDTH_EVEN), BF16),
        grid=(m // tm, IN_WIDTH_EVEN // tn),
        in_specs=[
            pl.BlockSpec((tm, D_MODEL), lambda i, j: (i, 0)),
            pl.BlockSpec((1, D_MODEL), lambda i, j: (0, 0)),
            pl.BlockSpec((D_MODEL, tn), lambda i, j: (0, j)),
            pl.BlockSpec((1, tn), lambda i, j: (0, j)),
        ],
        out_specs=pl.BlockSpec((tm, tn), lambda i, j: (i, j)),
        scratch_shapes=[pltpu.VMEM((tm, D_MODEL), BF16)],
        compiler_params=pltpu.CompilerParams(
            dimension_semantics=("arbitrary", "arbitrary"), vmem_limit_bytes=VMEM_LIMIT_BYTES),
        name="in_even",
    )(x2, g, w_bf, gain_row)


def _ret_kernel(q_ref, k_ref, v_ref, z_ref, gn_ref, o_ref, r_ref):
    c = RET_CHUNK
    tq = q_ref.shape[0]

    @pl.when(pl.program_id(1) == 0)
    def _():
        r_ref[...] = jnp.zeros(r_ref.shape, F32)

    row = lax.broadcasted_iota(jnp.int32, (c, c), 0)
    col = lax.broadcasted_iota(jnp.int32, (c, c), 1)
    diff = (row - col).astype(F32)
    pos = lax.broadcasted_iota(jnp.int32, (c, 1), 0).astype(F32)
    kscale = RET_DK ** -0.5

    for h in range(RET_HEADS):
        lg = math.log(1.0 - 2.0 ** (-5.0 - h))
        decay = jnp.where(diff >= 0, jnp.exp(lg * jnp.maximum(diff, 0.0)), 0.0) * kscale
        xi = jnp.exp(lg * (pos + 1.0))
        zeta = jnp.exp(lg * (c - 1.0 - pos)) * kscale
        g_chunk = math.exp(lg * c)
        qk_sl = slice(h * RET_DK, (h + 1) * RET_DK)
        v_sl = slice(h * RET_DV, (h + 1) * RET_DV)
        gn = gn_ref[h:h + 1, :]
        chunks = [slice(ci * c, (ci + 1) * c) for ci in range(tq // c)]

        contrib = []
        for rows in chunks:
            kz_t = (k_ref[rows, qk_sl].astype(F32) * zeta).T.astype(BF16)
            contrib.append(jnp.dot(kz_t, v_ref[rows, v_sl], preferred_element_type=F32))

        state = r_ref[h]
        states = []
        for s_c in contrib:
            states.append(state)
            state = g_chunk * state + s_c
        r_ref[h] = state

        for rows, r_prev in zip(chunks, states):
            q = q_ref[rows, qk_sl]
            v = v_ref[rows, v_sl]
            s = lax.dot_general(q, k_ref[rows, qk_sl], (((1,), (1,)), ((), ())),
                                preferred_element_type=F32) * decay
            y = jnp.dot(s.astype(BF16), v, preferred_element_type=F32)
            y = y + jnp.dot(q, r_prev.astype(BF16), preferred_element_type=F32) * xi
            mu = jnp.mean(y, axis=-1, keepdims=True)
            yc = y - mu
            var = jnp.mean(yc * yc, axis=-1, keepdims=True)
            yn = yc * lax.rsqrt(var + EPS) * gn
            o_ref[rows, v_sl] = (yn * z_ref[rows, v_sl].astype(F32)).astype(BF16)


def _retention(proj, gn_g, batch, seq):
    m = proj.shape[0]
    tq = RET_TQ
    nt = seq // tq
    row = lambda b, t: b * nt + t
    return pl.pallas_call(
        _ret_kernel,
        out_shape=jax.ShapeDtypeStruct((m, RET_WIDTH), BF16),
        grid=(batch, nt),
        in_specs=[
            pl.BlockSpec((tq, RET_QK_WIDTH), lambda b, t: (row(b, t), OFF_RQ // RET_QK_WIDTH)),
            pl.BlockSpec((tq, RET_QK_WIDTH), lambda b, t: (row(b, t), OFF_RK // RET_QK_WIDTH)),
            pl.BlockSpec((tq, RET_WIDTH), lambda b, t: (row(b, t), OFF_RV // RET_WIDTH)),
            pl.BlockSpec((tq, RET_WIDTH), lambda b, t: (row(b, t), OFF_Z // RET_WIDTH)),
            pl.BlockSpec((RET_HEADS, RET_DV), lambda b, t: (0, 0)),
        ],
        out_specs=pl.BlockSpec((tq, RET_WIDTH), lambda b, t: (row(b, t), 0)),
        scratch_shapes=[pltpu.VMEM((RET_HEADS, RET_DK, RET_DV), F32)],
        compiler_params=pltpu.CompilerParams(
            dimension_semantics=("arbitrary", "arbitrary"), vmem_limit_bytes=VMEM_LIMIT_BYTES),
        name="retention",
    )(proj, proj, proj, proj, gn_g)


def _att_kernel(*refs, n_cast):
    slopes_ref, q_ref, k_ref, v_ref, z_ref = refs[:5]
    cast_in, o_ref, cast_out = refs[5:5 + n_cast], refs[5 + n_cast], refs[6 + n_cast:6 + 2 * n_cast]
    (stage_q, stage_k, stage_v, q4, k4, v4, qp4, qp16, kp1, kp4, kp16, vp1, vp4, vp16, bias_ref, s_ref, pn_ref,
     o0, o1, o2, l0, l1, l2) = refs[6 + 2 * n_cast:]
    for src, dst in zip(cast_in, cast_out):
        dst[...] = src[...].astype(BF16)
    blk = ATT_BLOCK
    grp = ATT_GROUP
    seq = q_ref.shape[0]
    nblk = seq // blk
    n_it = nblk // grp
    assert grp == 4 and DILATED_PATTERNS == ((128, 1), (512, 4), (2048, 16))
    assert n_it % 2 == 0 and n_it >= 4 and nblk % 16 == 0
    slope = slopes_ref[pl.program_id(1)]
    nt = (((1,), (1,)), ((), ()))

    def natural_rows(bi, dil):
        nb = nblk // dil
        start = (bi // nb) + (bi % nb) * (blk * dil)
        if dil == 1:
            return pl.ds(pl.multiple_of(start, blk), blk)
        return pl.ds(start, blk, stride=dil)

    def block_rows(bi, lead=0):
        return pl.ds(pl.multiple_of(bi * blk + lead, blk), blk)

    for pad in (kp1, kp4, kp16, vp1, vp4, vp16):
        pad[0:blk, :] = jnp.zeros((blk, ATT_HD), BF16)
    kp1[blk:, :] = k_ref[...]
    vp1[blk:, :] = v_ref[...]

    def copy_4(it):
        rows = pl.ds(pl.multiple_of(it * (grp * blk), grp * blk), grp * blk)
        stage_q[...] = q_ref[rows, :].astype(F32)
        stage_k[...] = k_ref[rows, :].astype(F32)
        stage_v[...] = v_ref[rows, :].astype(F32)
        for r in range(4):
            bi = r * (nblk // 4) + it
            src = pl.ds(r, blk, stride=4)
            q, k, v = stage_q[src, :], stage_k[src, :], stage_v[src, :]
            q4[block_rows(bi), :], k4[block_rows(bi), :], v4[block_rows(bi), :] = q, k, v
            qp4[block_rows(bi), :] = q.astype(BF16)
            kp4[block_rows(bi, blk), :] = k.astype(BF16)
            vp4[block_rows(bi, blk), :] = v.astype(BF16)

    def copy_16(it):
        halves = (nblk // 4) // grp
        r4, n16 = it // halves, it % halves
        for c in range(4):
            bi = (r4 + 4 * c) * (nblk // 16) + n16
            src = pl.ds(r4 * (seq // 4) + n16 * (grp * blk) + c, blk, stride=4)
            qp16[block_rows(bi), :] = q4[src, :].astype(BF16)
            kp16[block_rows(bi, blk), :] = k4[src, :].astype(BF16)
            vp16[block_rows(bi, blk), :] = v4[src, :].astype(BF16)

    qi = lax.broadcasted_iota(jnp.int32, (blk, 2 * blk), 0)
    kj = lax.broadcasted_iota(jnp.int32, (blk, 2 * blk), 1)
    rel = qi - kj + blk
    for pi, (window, dil) in enumerate(DILATED_PATTERNS):
        valid = (rel >= 0) & (rel <= window // dil)
        alibi = -slope * (rel * dil).astype(F32)
        bias_ref[pi, 0] = jnp.where(valid & (kj >= blk), alibi, NEG)
        bias_ref[pi, 1] = jnp.where(valid, alibi, NEG)

    def make_stages(pi, dil, q_src, kp, vp, o_g, l_g, copy_next):
        nb = nblk // dil

        def scores(it):
            for g in range(grp):
                bi = it * grp + g
                kb = kp[pl.ds(pl.multiple_of(bi * blk, blk), 2 * blk), :]
                s_ref[it % 2, g] = lax.dot_general(q_src[block_rows(bi), :], kb, nt, preferred_element_type=F32)
            if copy_next is not None:
                copy_next(it)

        def probs(it):
            for g in range(grp):
                bi = it * grp + g
                s = s_ref[it % 2, g] + bias_ref[pi, jnp.minimum(bi % nb, 1)]
                mx = jnp.max(jnp.maximum(s[:, :blk], s[:, blk:]), axis=-1, keepdims=True)
                pr = jnp.exp(s - mx)
                den = jnp.sum(pr[:, :blk] + pr[:, blk:], axis=-1, keepdims=True)
                pn_ref[it % 2, g] = (pr * (1.0 / den)).astype(BF16)
                l_g[natural_rows(bi, dil), :] = jnp.broadcast_to(mx + jnp.log(den), (blk, ATT_HD))

        def values(it):
            for g in range(grp):
                bi = it * grp + g
                vb = vp[pl.ds(pl.multiple_of(bi * blk, blk), 2 * blk), :]
                o_g[natural_rows(bi, dil), :] = jnp.dot(pn_ref[it % 2, g], vb, preferred_element_type=F32)

        return scores, probs, values

    stages = [make_stages(pi, dil, *refs) for pi, ((_, dil), *refs) in enumerate(zip(
        DILATED_PATTERNS, (q_ref, qp4, qp16), (kp1, kp4, kp16), (vp1, vp4, vp16), (o0, o1, o2), (l0, l1, l2),
        (copy_4, copy_16, None)))]

    def fill(scores, probs, values):
        scores(0)
        probs(0)
        scores(1)

    def drain(scores, probs, values):
        values(n_it - 2)
        probs(n_it - 1)
        values(n_it - 1)

    fill(*stages[0])
    for pi, (scores, probs, values) in enumerate(stages):
        def body(it, carry, scores=scores, probs=probs, values=values):
            values(it - 2)
            probs(it - 1)
            scores(it)
            return carry

        lax.fori_loop(2, n_it, body, 0)
        drain(scores, probs, values)
        if pi + 1 < len(stages):
            fill(*stages[pi + 1])

    mrows = 256

    def merge(ci, carry):
        rows = pl.ds(pl.multiple_of(ci * mrows, mrows), mrows)
        a0, a1, a2 = l0[rows, :], l1[rows, :], l2[rows, :]
        mx = jnp.maximum(jnp.maximum(a0, a1), a2)
        e0, e1, e2 = jnp.exp(a0 - mx), jnp.exp(a1 - mx), jnp.exp(a2 - mx)
        y = (e0 * o0[rows, :] + e1 * o1[rows, :] + e2 * o2[rows, :]) / (e0 + e1 + e2)
        o_ref[rows, :] = (y * z_ref[rows, :].astype(F32)).astype(BF16)
        return carry

    lax.fori_loop(0, seq // mrows, merge, 0)


def _attention(proj, slopes, batch, seq, to_cast):
    m = proj.shape[0]
    hd = ATT_HD
    n_steps = batch * ATT_HEADS
    spec = lambda off: pl.BlockSpec((seq, hd), lambda b, h: (b, off // hd + h))
    slab = lambda a: pl.BlockSpec((a.shape[0] // n_steps, a.shape[1]), lambda b, h: (b * ATT_HEADS + h, 0))
    assert all(a.shape[0] % (16 * n_steps) == 0 for a in to_cast)
    f32_rows = pltpu.VMEM((seq, hd), F32)
    bf16_rows = pltpu.VMEM((seq, hd), BF16)
    bf16_padded = pltpu.VMEM((seq + ATT_BLOCK, hd), BF16)
    f32_group = pltpu.VMEM((ATT_GROUP * ATT_BLOCK, hd), F32)
    y_att, *cast = pl.pallas_call(
        functools.partial(_att_kernel, n_cast=len(to_cast)),
        out_shape=[jax.ShapeDtypeStruct((m, ATT_WIDTH), BF16)]
        + [jax.ShapeDtypeStruct(a.shape, BF16) for a in to_cast],
        grid=(batch, ATT_HEADS),
        in_specs=[
            pl.BlockSpec(memory_space=pltpu.SMEM),
            spec(OFF_AQ), spec(OFF_AK), spec(OFF_AV), spec(OFF_Z + RET_WIDTH),
        ] + [slab(a) for a in to_cast],
        out_specs=[pl.BlockSpec((seq, hd), lambda b, h: (b, h))] + [slab(a) for a in to_cast],
        scratch_shapes=[
            f32_group, f32_group, f32_group, f32_rows, f32_rows, f32_rows,
            bf16_rows, bf16_rows,
            bf16_padded, bf16_padded, bf16_padded, bf16_padded, bf16_padded, bf16_padded,
            pltpu.VMEM((len(DILATED_PATTERNS), 2, ATT_BLOCK, 2 * ATT_BLOCK), F32),
            pltpu.VMEM((2, ATT_GROUP, ATT_BLOCK, 2 * ATT_BLOCK), F32),
            pltpu.VMEM((2, ATT_GROUP, ATT_BLOCK, 2 * ATT_BLOCK), BF16),
            f32_rows, f32_rows, f32_rows, f32_rows, f32_rows, f32_rows,
        ],
        compiler_params=pltpu.CompilerParams(
            dimension_semantics=("arbitrary", "arbitrary"), vmem_limit_bytes=VMEM_LIMIT_BYTES),
        name="dilated_attention",
    )(slopes, proj, proj, proj, proj, *to_cast)
    return y_att, cast


def _post_tail(x1, p_ref, pg_ref, wg_ref, wp_ref, o_ref):
    h2 = _rms_rows(x1, pg_ref[...]).astype(BF16)
    pb = p_ref[...].astype(BF16)
    for cc in range(D_MODEL // POST_NC):
        sl = slice(cc * POST_NC, (cc + 1) * POST_NC)
        gate = jax.nn.sigmoid(jnp.dot(h2, wg_ref[:, sl], preferred_element_type=F32))
        pp = jnp.dot(pb, wp_ref[:, sl], preferred_element_type=F32)
        o_ref[:, sl] = x1[:, sl] + gate * pp


def _post_kernel(x_ref, ya_ref, yb_ref, p_ref, wa_ref, wb_ref, pg_ref, wg_ref, wp_ref, o_ref):
    x1 = x_ref[...] + jnp.dot(ya_ref[...], wa_ref[...], preferred_element_type=F32)
    x1 = x1 + jnp.dot(yb_ref[...], wb_ref[...], preferred_element_type=F32)
    _post_tail(x1, p_ref, pg_ref, wg_ref, wp_ref, o_ref)


def _const_spec(shape):
    return pl.BlockSpec(shape, lambda i: (0,) * len(shape), pipeline_mode=pl.Buffered(1))


def _ple_rows_spec(layer, m, tm):
    return pl.BlockSpec((tm, PLE_DIM), lambda i: (layer * (m // tm) + i, 0))


def _post(x2, ya, ya_blk, yb, yb_blk, p_all, layer, w_out_bf, pg, wg_all_bf, wp_all_bf, name):
    m = x2.shape[0]
    tm = POST_TM
    half = D_MODEL // 2
    rows = lambda w, blk=0: pl.BlockSpec((tm, w), lambda i: (i, blk))
    layer_spec = lambda k: pl.BlockSpec((None, k, D_MODEL), lambda i: (layer, 0, 0), pipeline_mode=pl.Buffered(1))
    return pl.pallas_call(
        _post_kernel,
        out_shape=jax.ShapeDtypeStruct((m, D_MODEL), F32),
        grid=(m // tm,),
        in_specs=[
            rows(D_MODEL), rows(half, ya_blk), rows(half, yb_blk), _ple_rows_spec(layer, m, tm),
            pl.BlockSpec((half, D_MODEL), lambda i: (0, 0), pipeline_mode=pl.Buffered(1)),
            pl.BlockSpec((half, D_MODEL), lambda i: (1, 0), pipeline_mode=pl.Buffered(1)),
            _const_spec((1, D_MODEL)), layer_spec(D_MODEL), layer_spec(PLE_DIM),
        ],
        out_specs=rows(D_MODEL),
        compiler_params=pltpu.CompilerParams(
            dimension_semantics=("arbitrary",), vmem_limit_bytes=VMEM_LIMIT_BYTES),
        name=name,
    )(x2, ya, yb, p_all, w_out_bf, w_out_bf, pg, wg_all_bf, wp_all_bf)


def _in_odd_kernel(x_ref, g_ref, wb_ref, wc_ref, wu_ref, wz_ref, cw_ref, y_ref, hn_ref, ext_ref, halo_ref, *,
                   tiles_per_seq):
    i, j = pl.program_id(0), pl.program_id(1)
    tm = x_ref.shape[0]

    @pl.when(j == 0)
    def _():
        hn_ref[...] = _rms_rows(x_ref[...], g_ref[...]).astype(BF16)

    @pl.when((i == 0) & (j == 0))
    def _():
        halo_ref[...] = jnp.zeros(halo_ref.shape, F32)

    hn = hn_ref[...]
    cu = (jnp.dot(hn, wc_ref[...], preferred_element_type=F32)
          * jnp.dot(hn, wu_ref[...], preferred_element_type=F32))
    first = (i % tiles_per_seq) == 0
    ext_ref[0:HALO, :] = jnp.where(first, 0.0, halo_ref[j])
    ext_ref[HALO:HALO + tm, :] = cu
    halo_ref[j] = cu[tm - HALO:, :]
    conv = (cw_ref[0:1, :] * ext_ref[HALO - 2:HALO - 2 + tm, :]
            + cw_ref[1:2, :] * ext_ref[HALO - 1:HALO - 1 + tm, :]
            + cw_ref[2:3, :] * cu)
    bg = jnp.dot(hn, wb_ref[...], preferred_element_type=F32)
    z = jnp.dot(hn, wz_ref[...], preferred_element_type=F32)
    y_ref[...] = (bg * conv * _silu(z)).astype(BF16)


def _in_odd(x2, g, w_bf, conv_w, seq):
    m = x2.shape[0]
    tm, tc = IN_TM, IN_ODD_TC
    nb = D_MODEL // tc
    wspec = lambda seg: pl.BlockSpec((D_MODEL, tc), lambda i, j: (0, seg * nb + j))
    return pl.pallas_call(
        functools.partial(_in_odd_kernel, tiles_per_seq=seq // tm),
        out_shape=jax.ShapeDtypeStruct((m, D_MODEL), BF16),
        grid=(m // tm, nb),
        in_specs=[
            pl.BlockSpec((tm, D_MODEL), lambda i, j: (i, 0)),
            pl.BlockSpec((1, D_MODEL), lambda i, j: (0, 0)),
            wspec(0), wspec(1), wspec(2), wspec(3),
            pl.BlockSpec((CONV_WIDTH, tc), lambda i, j: (0, j)),
        ],
        out_specs=pl.BlockSpec((tm, tc), lambda i, j: (i, j)),
        scratch_shapes=[
            pltpu.VMEM((tm, D_MODEL), BF16),
            pltpu.VMEM((HALO + tm, tc), F32),
            pltpu.VMEM((nb, HALO, tc), F32),
        ],
        compiler_params=pltpu.CompilerParams(
            dimension_semantics=("arbitrary", "arbitrary"), vmem_limit_bytes=VMEM_LIMIT_BYTES),
        name="in_odd",
    )(x2, g, w_bf, w_bf, w_bf, w_bf, conv_w)


def _even_gain_row(q_g, k_g):
    ones = jnp.ones((OFF_AQ,), F32)
    return jnp.concatenate([
        ones,
        jnp.tile(q_g * ATT_HD ** -0.5, ATT_HEADS),
        jnp.tile(k_g, ATT_HEADS),
        jnp.ones((IN_WIDTH_EVEN - OFF_AV,), F32),
    ]).reshape(1, IN_WIDTH_EVEN)


def kernel(x, p, pre_norm_g, w_in_even, q_norm_g, k_norm_g, ret_gn_g, w_out_even, w_in_odd, conv_w_odd,
           w_out_odd, ple_norm_g, w_ple_gate, w_ple_proj):
    batch, seq, d = x.shape
    depth = p.shape[0]
    assert d == D_MODEL and seq % (ATT_BLOCK * DILATED_PATTERNS[-1][1]) == 0 and seq % IN_TM == 0
    m = batch * seq
    slopes = jnp.asarray([(2.0 ** (-8.0 / ATT_HEADS)) ** (i + 1) for i in range(ATT_HEADS)], F32)
    x2 = x.reshape(m, d)
    p_all = p.reshape(depth * m, PLE_DIM)
    late_f32 = dict(w_out_even=w_out_even, w_ple_gate=w_ple_gate, w_ple_proj=w_ple_proj, w_in_odd=w_in_odd,
                    w_out_odd=w_out_odd)
    if w_in_even.shape[0] > 1:
        late_f32["w_in_even_rest"] = w_in_even[1:]
    bf = None
    for i in range(depth):
        j = i // 2
        pg = ple_norm_g[i].reshape(1, d)
        g = pre_norm_g[i].reshape(1, d)
        if i % 2 == 0:
            w_in_bf = w_in_even[0].astype(BF16) if j == 0 else bf["w_in_even_rest"][j - 1]
            proj = _in_even(x2, g, w_in_bf, _even_gain_row(q_norm_g[j], k_norm_g[j]))
            y_ret = _retention(proj, ret_gn_g[j], batch, seq)
            to_cast = [a.reshape(-1, d) for a in late_f32.values()] if bf is None else []
            y_att, cast = _attention(proj, slopes, batch, seq, to_cast)
            if bf is None:
                bf = {name: c.reshape(a.shape) for (name, a), c in zip(late_f32.items(), cast)}
            x2 = _post(x2, y_ret, 0, y_att, 0, p_all, i, bf["w_out_even"][j], pg, bf["w_ple_gate"],
                       bf["w_ple_proj"], "post_even")
        else:
            y = _in_odd(x2, g, bf["w_in_odd"][j], conv_w_odd[j], seq)
            x2 = _post(x2, y, 0, y, 1, p_all, i, bf["w_out_odd"][j], pg, bf["w_ple_gate"], bf["w_ple_proj"],
                       "post_odd")
    return x2.reshape(batch, seq, d)
```

```python
import functools
import math

import jax
import jax.numpy as jnp
from jax import lax
from jax.experimental import pallas as pl
from jax.experimental.pallas import tpu as pltpu

F32 = jnp.float32
BF16 = jnp.bfloat16

D_MODEL = 2048
PLE_DIM = 256
EPS = 1e-6
NEG = -1e30
RET_DK = 128
RET_DV = 256
RET_HEADS = 4
RET_CHUNK = 256
ATT_HD = 128
ATT_HEADS = 8
ATT_BLOCK = 128
DILATED_PATTERNS = ((128, 1), (512, 4), (2048, 16))
RET_QK_WIDTH = RET_HEADS * RET_DK
RET_WIDTH = RET_HEADS * RET_DV
ATT_WIDTH = ATT_HEADS * ATT_HD
IN_WIDTH_EVEN = 2 * RET_QK_WIDTH + RET_WIDTH + 3 * ATT_WIDTH + RET_WIDTH + ATT_WIDTH
CONV_WIDTH = 3

OFF_RQ = 0
OFF_RK = RET_QK_WIDTH
OFF_RV = 2 * RET_QK_WIDTH
OFF_AQ = OFF_RV + RET_WIDTH
OFF_AK = OFF_AQ + ATT_WIDTH
OFF_AV = OFF_AK + ATT_WIDTH
OFF_Z = OFF_AV + ATT_WIDTH

VMEM_LIMIT_BYTES = 56 * 1024 * 1024

IN_TM = 1024
IN_EVEN_TN = 1024
IN_EVEN_NC = 256
IN_ODD_TC = 256
RET_TQ = 1024
ATT_GROUP = 4
POST_TM = 512
POST_NC = 512
HALO = 8


def _rms_rows(x, g):
    ms = jnp.mean(x * x, axis=-1, keepdims=True)
    return x * lax.rsqrt(ms + EPS) * g


def _silu(z):
    return z * jax.nn.sigmoid(z)


def _silu_cols(z, cols):
    del cols
    return _silu(z)


def _in_even_kernel(x_ref, g_ref, w_ref, gain_ref, o_ref, hn_ref):
    j = pl.program_id(1)

    @pl.when(j == 0)
    def _():
        hn_ref[...] = _rms_rows(x_ref[...], g_ref[...]).astype(BF16)

    tn = o_ref.shape[1]
    j_aq, j_av, j_z = OFF_AQ // tn, OFF_AV // tn, OFF_Z // tn

    def tile(epilogue):
        hn = hn_ref[...]
        for cc in range(tn // IN_EVEN_NC):
            acc = jnp.dot(hn, w_ref[:, cc * IN_EVEN_NC:(cc + 1) * IN_EVEN_NC], preferred_element_type=F32)
            for hh in range(IN_EVEN_NC // ATT_HD):
                sl = slice(cc * IN_EVEN_NC + hh * ATT_HD, cc * IN_EVEN_NC + (hh + 1) * ATT_HD)
                o_ref[:, sl] = epilogue(acc[:, hh * ATT_HD:(hh + 1) * ATT_HD], sl).astype(BF16)

    @pl.when((j < j_aq) | (j == j_av))
    def _():
        tile(lambda a, sl: a)

    @pl.when((j >= j_aq) & (j < j_av))
    def _():
        tile(lambda a, sl: _rms_rows(a, gain_ref[:, sl]))

    @pl.when(j >= j_z)
    def _():
        tile(_silu_cols)


def _in_even(x2, g, w_bf, gain_row):
    m = x2.shape[0]
    tm, tn = IN_TM, IN_EVEN_TN
    return pl.pallas_call(
        _in_even_kernel,
        out_shape=jax.ShapeDtypeStruct((m, IN_WIDTH_EVEN), BF16),
        grid=(m // tm, IN_WIDTH_EVEN // tn),
        in_specs=[
            pl.BlockSpec((tm, D_MODEL), lambda i, j: (i, 0)),
            pl.BlockSpec((1, D_MODEL), lambda i, j: (0, 0)),
            pl.BlockSpec((D_MODEL, tn), lambda i, j: (0, j)),
            pl.BlockSpec((1, tn), lambda i, j: (0, j)),
        ],
        out_specs=pl.BlockSpec((tm, tn), lambda i, j: (i, j)),
        scratch_shapes=[pltpu.VMEM((tm, D_MODEL), BF16)],
        compiler_params=pltpu.CompilerParams(
            dimension_semantics=("arbitrary", "arbitrary"), vmem_limit_bytes=VMEM_LIMIT_BYTES),
        name="in_even",
    )(x2, g, w_bf, gain_row)


def _ret_kernel(q_ref, k_ref, v_ref, z_ref, gn_ref, o_ref, r_ref):
    c = RET_CHUNK
    tq = q_ref.shape[0]

    @pl.when(pl.program_id(1) == 0)
    def _():
        r_ref[...] = jnp.zeros(r_ref.shape, F32)

    row = lax.broadcasted_iota(jnp.int32, (c, c), 0)
    col = lax.broadcasted_iota(jnp.int32, (c, c), 1)
    diff = (row - col).astype(F32)
    pos = lax.broadcasted_iota(jnp.int32, (c, 1), 0).astype(F32)
    kscale = RET_DK ** -0.5

    for h in range(RET_HEADS):
        lg = math.log(1.0 - 2.0 ** (-5.0 - h))
        decay = jnp.where(diff >= 0, jnp.exp(lg * jnp.maximum(diff, 0.0)), 0.0) * kscale
        xi = jnp.exp(lg * (pos + 1.0))
        zeta = jnp.exp(lg * (c - 1.0 - pos)) * kscale
        g_chunk = math.exp(lg * c)
        qk_sl = slice(h * RET_DK, (h + 1) * RET_DK)
        v_sl = slice(h * RET_DV, (h + 1) * RET_DV)
        gn = gn_ref[h:h + 1, :]
        chunks = [slice(ci * c, (ci + 1) * c) for ci in range(tq // c)]

        contrib = []
        for rows in chunks:
            kz_t = (k_ref[rows, qk_sl].astype(F32) * zeta).T.astype(BF16)
            contrib.append(jnp.dot(kz_t, v_ref[rows, v_sl], preferred_element_type=F32))

        state = r_ref[h]
        states = []
        for s_c in contrib:
            states.append(state)
            state = g_chunk * state + s_c
        r_ref[h] = state

        for rows, r_prev in zip(chunks, states):
            q = q_ref[rows, qk_sl]
            v = v_ref[rows, v_sl]
            s = lax.dot_general(q, k_ref[rows, qk_sl], (((1,), (1,)), ((), ())),
                                preferred_element_type=F32) * decay
            y = jnp.dot(s.astype(BF16), v, preferred_element_type=F32)
            y = y + jnp.dot(q, r_prev.astype(BF16), preferred_element_type=F32) * xi
            mu = jnp.mean(y, axis=-1, keepdims=True)
            yc = y - mu
            var = jnp.mean(yc * yc, axis=-1, keepdims=True)
            yn = yc * lax.rsqrt(var + EPS) * gn
            o_ref[rows, v_sl] = (yn * z_ref[rows, v_sl].astype(F32)).astype(BF16)


def _retention(proj, gn_g, batch, seq):
    m = proj.shape[0]
    tq = RET_TQ
    nt = seq // tq
    row = lambda b, t: b * nt + t
    return pl.pallas_call(
        _ret_kernel,
        out_shape=jax.ShapeDtypeStruct((m, RET_WIDTH), BF16),
        grid=(batch, nt),
        in_specs=[
            pl.BlockSpec((tq, RET_QK_WIDTH), lambda b, t: (row(b, t), OFF_RQ // RET_QK_WIDTH)),
            pl.BlockSpec((tq, RET_QK_WIDTH), lambda b, t: (row(b, t), OFF_RK // RET_QK_WIDTH)),
            pl.BlockSpec((tq, RET_WIDTH), lambda b, t: (row(b, t), OFF_RV // RET_WIDTH)),
            pl.BlockSpec((tq, RET_WIDTH), lambda b, t: (row(b, t), OFF_Z // RET_WIDTH)),
            pl.BlockSpec((RET_HEADS, RET_DV), lambda b, t: (0, 0)),
        ],
        out_specs=pl.BlockSpec((tq, RET_WIDTH), lambda b, t: (row(b, t), 0)),
        scratch_shapes=[pltpu.VMEM((RET_HEADS, RET_DK, RET_DV), F32)],
        compiler_params=pltpu.CompilerParams(
            dimension_semantics=("arbitrary", "arbitrary"), vmem_limit_bytes=VMEM_LIMIT_BYTES),
        name="retention",
    )(proj, proj, proj, proj, gn_g)


def _att_kernel(*refs, n_cast):
    slopes_ref, q_ref, k_ref, v_ref, z_ref = refs[:5]
    cast_in, o_ref, cast_out = refs[5:5 + n_cast], refs[5 + n_cast], refs[6 + n_cast:6 + 2 * n_cast]
    (stage_q, stage_k, stage_v, q4, k4, v4, qp4, qp16, kp1, kp4, kp16, vp1, vp4, vp16, bias_ref, s_ref, pn_ref,
     o0, o1, o2, l0, l1, l2) = refs[6 + 2 * n_cast:]
    for src, dst in zip(cast_in, cast_out):
        dst[...] = src[...].astype(BF16)
    blk = ATT_BLOCK
    grp = ATT_GROUP
    seq = q_ref.shape[0]
    nblk = seq // blk
    n_it = nblk // grp
    assert grp == 4 and DILATED_PATTERNS == ((128, 1), (512, 4), (2048, 16))
    assert n_it % 2 == 0 and n_it >= 4 and nblk % 16 == 0
    slope = slopes_ref[pl.program_id(1)]
    nt = (((1,), (1,)), ((), ()))

    def natural_rows(bi, dil):
        nb = nblk // dil
        start = (bi // nb) + (bi % nb) * (blk * dil)
        if dil == 1:
            return pl.ds(pl.multiple_of(start, blk), blk)
        return pl.ds(start, blk, stride=dil)

    def block_rows(bi, lead=0):
        return pl.ds(pl.multiple_of(bi * blk + lead, blk), blk)

    for pad in (kp1, kp4, kp16, vp1, vp4, vp16):
        pad[0:blk, :] = jnp.zeros((blk, ATT_HD), BF16)
    kp1[blk:, :] = k_ref[...]
    vp1[blk:, :] = v_ref[...]

    def copy_4(it):
        rows = pl.ds(pl.multiple_of(it * (grp * blk), grp * blk), grp * blk)
        stage_q[...] = q_ref[rows, :].astype(F32)
        stage_k[...] = k_ref[rows, :].astype(F32)
        stage_v[...] = v_ref[rows, :].astype(F32)
        for r in range(4):
            bi = r * (nblk // 4) + it
            src = pl.ds(r, blk, stride=4)
            q, k, v = stage_q[src, :], stage_k[src, :], stage_v[src, :]
            q4[block_rows(bi), :], k4[block_rows(bi), :], v4[block_rows(bi), :] = q, k, v
            qp4[block_rows(bi), :] = q.astype(BF16)
            kp4[block_rows(bi, blk), :] = k.astype(BF16)
            vp4[block_rows(bi, blk), :] = v.astype(BF16)

    def copy_16(it):
        halves = (nblk // 4) // grp
        r4, n16 = it // halves, it % halves
        for c in range(4):
            bi = (r4 + 4 * c) * (nblk // 16) + n16
            src = pl.ds(r4 * (seq // 4) + n16 * (grp * blk) + c, blk, stride=4)
            qp16[block_rows(bi), :] = q4[src, :].astype(BF16)
            kp16[block_rows(bi, blk), :] = k4[src, :].astype(BF16)
            vp16[block_rows(bi, blk), :] = v4[src, :].astype(BF16)

    qi = lax.broadcasted_iota(jnp.int32, (blk, 2 * blk), 0)
    kj = lax.broadcasted_iota(jnp.int32, (blk, 2 * blk), 1)
    rel = qi - kj + blk
    for pi, (window, dil) in enumerate(DILATED_PATTERNS):
        valid = (rel >= 0) & (rel <= window // dil)
        alibi = -slope * (rel * dil).astype(F32)
        bias_ref[pi, 0] = jnp.where(valid & (kj >= blk), alibi, NEG)
        bias_ref[pi, 1] = jnp.where(valid, alibi, NEG)

    def make_stages(pi, dil, q_src, kp, vp, o_g, l_g, copy_next):
        nb = nblk // dil

        def scores(it):
            for g in range(grp):
                bi = it * grp + g
                kb = kp[pl.ds(pl.multiple_of(bi * blk, blk), 2 * blk), :]
                s_ref[it % 2, g] = lax.dot_general(q_src[block_rows(bi), :], kb, nt, preferred_element_type=F32)
            if copy_next is not None:
                copy_next(it)

        def probs(it):
            for g in range(grp):
                bi = it * grp + g
                s = s_ref[it % 2, g] + bias_ref[pi, jnp.minimum(bi % nb, 1)]
                mx = jnp.max(jnp.maximum(s[:, :blk], s[:, blk:]), axis=-1, keepdims=True)
                pr = jnp.exp(s - mx)
                den = jnp.sum(pr[:, :blk] + pr[:, blk:], axis=-1, keepdims=True)
                pn_ref[it % 2, g] = (pr * (1.0 / den)).astype(BF16)
                l_g[natural_rows(bi, dil), :] = jnp.broadcast_to(mx + jnp.log(den), (blk, ATT_HD))

        def values(it):
            for g in range(grp):
                bi = it * grp + g
                vb = vp[pl.ds(pl.multiple_of(bi * blk, blk), 2 * blk), :]
                o_g[natural_rows(bi, dil), :] = jnp.dot(pn_ref[it % 2, g], vb, preferred_element_type=F32)

        return scores, probs, values

    stages = [make_stages(pi, dil, *refs) for pi, ((_, dil), *refs) in enumerate(zip(
        DILATED_PATTERNS, (q_ref, qp4, qp16), (kp1, kp4, kp16), (vp1, vp4, vp16), (o0, o1, o2), (l0, l1, l2),
        (copy_4, copy_16, None)))]

    def fill(scores, probs, values):
        scores(0)
        probs(0)
        scores(1)

    def drain(scores, probs, values):
        values(n_it - 2)
        probs(n_it - 1)
        values(n_it - 1)

    fill(*stages[0])
    for pi, (scores, probs, values) in enumerate(stages):
        def body(it, carry, scores=scores, probs=probs, values=values):
            values(it - 2)
            probs(it - 1)
            scores(it)
            return carry

        lax.fori_loop(2, n_it, body, 0)
        drain(scores, probs, values)
        if pi + 1 < len(stages):
            fill(*stages[pi + 1])

    mrows = 256

    def merge(ci, carry):
        rows = pl.ds(pl.multiple_of(ci * mrows, mrows), mrows)
        a0, a1, a2 = l0[rows, :], l1[rows, :], l2[rows, :]
        mx = jnp.maximum(jnp.maximum(a0, a1), a2)
        e0, e1, e2 = jnp.exp(a0 - mx), jnp.exp(a1 - mx), jnp.exp(a2 - mx)
        y = (e0 * o0[rows, :] + e1 * o1[rows, :] + e2 * o2[rows, :]) / (e0 + e1 + e2)
        o_ref[rows, :] = (y * z_ref[rows, :].astype(F32)).astype(BF16)
        return carry

    lax.fori_loop(0, seq // mrows, merge, 0)


def _attention(proj, slopes, batch, seq, to_cast):
    m = proj.shape[0]
    hd = ATT_HD
    n_steps = batch * ATT_HEADS
    spec = lambda off: pl.BlockSpec((seq, hd), lambda b, h: (b, off // hd + h))
    slab = lambda a: pl.BlockSpec((a.shape[0] // n_steps, a.shape[1]), lambda b, h: (b * ATT_HEADS + h, 0))
    assert all(a.shape[0] % (16 * n_steps) == 0 for a in to_cast)
    f32_rows = pltpu.VMEM((seq, hd), F32)
    bf16_rows = pltpu.VMEM((seq, hd), BF16)
    bf16_padded = pltpu.VMEM((seq + ATT_BLOCK, hd), BF16)
    f32_group = pltpu.VMEM((ATT_GROUP * ATT_BLOCK, hd), F32)
    y_att, *cast = pl.pallas_call(
        functools.partial(_att_kernel, n_cast=len(to_cast)),
        out_shape=[jax.ShapeDtypeStruct((m, ATT_WIDTH), BF16)]
        + [jax.ShapeDtypeStruct(a.shape, BF16) for a in to_cast],
        grid=(batch, ATT_HEADS),
        in_specs=[
            pl.BlockSpec(memory_space=pltpu.SMEM),
            spec(OFF_AQ), spec(OFF_AK), spec(OFF_AV), spec(OFF_Z + RET_WIDTH),
        ] + [slab(a) for a in to_cast],
        out_specs=[pl.BlockSpec((seq, hd), lambda b, h: (b, h))] + [slab(a) for a in to_cast],
        scratch_shapes=[
            f32_group, f32_group, f32_group, f32_rows, f32_rows, f32_rows,
            bf16_rows, bf16_rows,
            bf16_padded, bf16_padded, bf16_padded, bf16_padded, bf16_padded, bf16_padded,
            pltpu.VMEM((len(DILATED_PATTERNS), 2, ATT_BLOCK, 2 * ATT_BLOCK), F32),
            pltpu.VMEM((2, ATT_GROUP, ATT_BLOCK, 2 * ATT_BLOCK), F32),
            pltpu.VMEM((2, ATT_GROUP, ATT_BLOCK, 2 * ATT_BLOCK), BF16),
            f32_rows, f32_rows, f32_rows, f32_rows, f32_rows, f32_rows,
        ],
        compiler_params=pltpu.CompilerParams(
            dimension_semantics=("arbitrary", "arbitrary"), vmem_limit_bytes=VMEM_LIMIT_BYTES),
        name="dilated_attention",
    )(slopes, proj, proj, proj, proj, *to_cast)
    return y_att, cast


def _post_tail(x1, p_ref, pg_ref, wg_ref, wp_ref, o_ref):
    h2 = _rms_rows(x1, pg_ref[...]).astype(BF16)
    pb = p_ref[...].astype(BF16)
    for cc in range(D_MODEL // POST_NC):
        sl = slice(cc * POST_NC, (cc + 1) * POST_NC)
        gate = jax.nn.sigmoid(jnp.dot(h2, wg_ref[:, sl], preferred_element_type=F32))
        pp = jnp.dot(pb, wp_ref[:, sl], preferred_element_type=F32)
        o_ref[:, sl] = x1[:, sl] + gate * pp


def _post_kernel(x_ref, ya_ref, yb_ref, p_ref, wa_ref, wb_ref, pg_ref, wg_ref, wp_ref, o_ref):
    x1 = x_ref[...] + jnp.dot(ya_ref[...], wa_ref[...], preferred_element_type=F32)
    x1 = x1 + jnp.dot(yb_ref[...], wb_ref[...], preferred_element_type=F32)
    _post_tail(x1, p_ref, pg_ref, wg_ref, wp_ref, o_ref)


def _const_spec(shape):
    return pl.BlockSpec(shape, lambda i: (0,) * len(shape), pipeline_mode=pl.Buffered(1))


def _ple_rows_spec(layer, m, tm):
    return pl.BlockSpec((tm, PLE_DIM), lambda i: (layer * (m // tm) + i, 0))


def _post(x2, ya, ya_blk, yb, yb_blk, p_all, layer, w_out_bf, pg, wg_all_bf, wp_all_bf, name):
    m = x2.shape[0]
    tm = POST_TM
    half = D_MODEL // 2
    rows = lambda w, blk=0: pl.BlockSpec((tm, w), lambda i: (i, blk))
    layer_spec = lambda k: pl.BlockSpec((None, k, D_MODEL), lambda i: (layer, 0, 0), pipeline_mode=pl.Buffered(1))
    return pl.pallas_call(
        _post_kernel,
        out_shape=jax.ShapeDtypeStruct((m, D_MODEL), F32),
        grid=(m // tm,),
        in_specs=[
            rows(D_MODEL), rows(half, ya_blk), rows(half, yb_blk), _ple_rows_spec(layer, m, tm),
            pl.BlockSpec((half, D_MODEL), lambda i: (0, 0), pipeline_mode=pl.Buffered(1)),
            pl.BlockSpec((half, D_MODEL), lambda i: (1, 0), pipeline_mode=pl.Buffered(1)),
            _const_spec((1, D_MODEL)), layer_spec(D_MODEL), layer_spec(PLE_DIM),
        ],
        out_specs=rows(D_MODEL),
        compiler_params=pltpu.CompilerParams(
            dimension_semantics=("arbitrary",), vmem_limit_bytes=VMEM_LIMIT_BYTES),
        name=name,
    )(x2, ya, yb, p_all, w_out_bf, w_out_bf, pg, wg_all_bf, wp_all_bf)


def _in_odd_kernel(x_ref, g_ref, wb_ref, wc_ref, wu_ref, wz_ref, cw_ref, y_ref, hn_ref, ext_ref, halo_ref, *,
                   tiles_per_seq):
    i, j = pl.program_id(0), pl.program_id(1)
    tm = x_ref.shape[0]

    @pl.when(j == 0)
    def _():
        hn_ref[...] = _rms_rows(x_ref[...], g_ref[...]).astype(BF16)

    @pl.when((i == 0) & (j == 0))
    def _():
        halo_ref[...] = jnp.zeros(halo_ref.shape, F32)

    hn = hn_ref[...]
    cu = (jnp.dot(hn, wc_ref[...], preferred_element_type=F32)
          * jnp.dot(hn, wu_ref[...], preferred_element_type=F32))
    first = (i % tiles_per_seq) == 0
    ext_ref[0:HALO, :] = jnp.where(first, 0.0, halo_ref[j])
    ext_ref[HALO:HALO + tm, :] = cu
    halo_ref[j] = cu[tm - HALO:, :]
    conv = (cw_ref[0:1, :] * ext_ref[HALO - 2:HALO - 2 + tm, :]
            + cw_ref[1:2, :] * ext_ref[HALO - 1:HALO - 1 + tm, :]
            + cw_ref[2:3, :] * cu)
    bg = jnp.dot(hn, wb_ref[...], preferred_element_type=F32)
    z = jnp.dot(hn, wz_ref[...], preferred_element_type=F32)
    y_ref[...] = (bg * conv * _silu(z)).astype(BF16)


def _in_odd(x2, g, w_bf, conv_w, seq):
    m = x2.shape[0]
    tm, tc = IN_TM, IN_ODD_TC
    nb = D_MODEL // tc
    wspec = lambda seg: pl.BlockSpec((D_MODEL, tc), lambda i, j: (0, seg * nb + j))
    return pl.pallas_call(
        functools.partial(_in_odd_kernel, tiles_per_seq=seq // tm),
        out_shape=jax.ShapeDtypeStruct((m, D_MODEL), BF16),
        grid=(m // tm, nb),
        in_specs=[
            pl.BlockSpec((tm, D_MODEL), lambda i, j: (i, 0)),
            pl.BlockSpec((1, D_MODEL), lambda i, j: (0, 0)),
            wspec(0), wspec(1), wspec(2), wspec(3),
            pl.BlockSpec((CONV_WIDTH, tc), lambda i, j: (0, j)),
        ],
        out_specs=pl.BlockSpec((tm, tc), lambda i, j: (i, j)),
        scratch_shapes=[
            pltpu.VMEM((tm, D_MODEL), BF16),
            pltpu.VMEM((HALO + tm, tc), F32),
            pltpu.VMEM((nb, HALO, tc), F32),
        ],
        compiler_params=pltpu.CompilerParams(
            dimension_semantics=("arbitrary", "arbitrary"), vmem_limit_bytes=VMEM_LIMIT_BYTES),
        name="in_odd",
    )(x2, g, w_bf, w_bf, w_bf, w_bf, conv_w)


def _even_gain_row(q_g, k_g):
    ones = jnp.ones((OFF_AQ,), F32)
    return jnp.concatenate([
        ones,
        jnp.tile(q_g * ATT_HD ** -0.5, ATT_HEADS),
        jnp.tile(k_g, ATT_HEADS),
        jnp.ones((IN_WIDTH_EVEN - OFF_AV,), F32),
    ]).reshape(1, IN_WIDTH_EVEN)


def kernel(x, p, pre_norm_g, w_in_even, q_norm_g, k_norm_g, ret_gn_g, w_out_even, w_in_odd, conv_w_odd,
           w_out_odd, ple_norm_g, w_ple_gate, w_ple_proj):
    batch, seq, d = x.shape
    depth = p.shape[0]
    assert d == D_MODEL and seq % (ATT_BLOCK * DILATED_PATTERNS[-1][1]) == 0 and seq % IN_TM == 0
    m = batch * seq
    slopes = jnp.asarray([(2.0 ** (-8.0 / ATT_HEADS)) ** (i + 1) for i in range(ATT_HEADS)], F32)
    x2 = x.reshape(m, d)
    p_all = p.reshape(depth * m, PLE_DIM)
    late_f32 = dict(w_out_even=w_out_even, w_ple_gate=w_ple_gate, w_ple_proj=w_ple_proj, w_in_odd=w_in_odd,
                    w_out_odd=w_out_odd)
    if w_in_even.shape[0] > 1:
        late_f32["w_in_even_rest"] = w_in_even[1:]
    bf = None
    for i in range(depth):
        j = i // 2
        pg = ple_norm_g[i].reshape(1, d)
        g = pre_norm_g[i].reshape(1, d)
        if i % 2 == 0:
            w_in_bf = w_in_even[0].astype(BF16) if j == 0 else bf["w_in_even_rest"][j - 1]
            proj = _in_even(x2, g, w_in_bf, _even_gain_row(q_norm_g[j], k_norm_g[j]))
            y_ret = _retention(proj, ret_gn_g[j], batch, seq)
            to_cast = [a.reshape(-1, a.shape[-1]) for a in late_f32.values()] if bf is None else []
            y_att, cast = _attention(proj, slopes, batch, seq, to_cast)
            if bf is None:
                bf = {name: c.reshape(a.shape) for (name, a), c in zip(late_f32.items(), cast)}
            x2 = _post(x2, y_ret, 0, y_att, 0, p_all, i, bf["w_out_even"][j], pg, bf["w_ple_gate"],
                       bf["w_ple_proj"], "post_even")
        else:
            y = _in_odd(x2, g, bf["w_in_odd"][j], conv_w_odd[j], seq)
            x2 = _post(x2, y, 0, y, 1, p_all, i, bf["w_out_odd"][j], pg, bf["w_ple_gate"], bf["w_ple_proj"],
                       "post_odd")
    return x2.reshape(batch, seq, d)
```

```python
import functools
import math

import jax
import jax.numpy as jnp
from jax import lax
from jax.experimental import pallas as pl
from jax.experimental.pallas import tpu as pltpu

F32 = jnp.float32
BF16 = jnp.bfloat16

D_MODEL = 2048
PLE_DIM = 256
EPS = 1e-6
NEG = -1e30
RET_DK = 128
RET_DV = 256
RET_HEADS = 4
RET_CHUNK = 256
ATT_HD = 128
ATT_HEADS = 8
ATT_BLOCK = 128
DILATED_PATTERNS = ((128, 1), (512, 4), (2048, 16))
RET_QK_WIDTH = RET_HEADS * RET_DK
RET_WIDTH = RET_HEADS * RET_DV
ATT_WIDTH = ATT_HEADS * ATT_HD
IN_WIDTH_EVEN = 2 * RET_QK_WIDTH + RET_WIDTH + 3 * ATT_WIDTH + RET_WIDTH + ATT_WIDTH
CONV_WIDTH = 3

OFF_RQ = 0
OFF_RK = RET_QK_WIDTH
OFF_RV = 2 * RET_QK_WIDTH
OFF_AQ = OFF_RV + RET_WIDTH
OFF_AK = OFF_AQ + ATT_WIDTH
OFF_AV = OFF_AK + ATT_WIDTH
OFF_Z = OFF_AV + ATT_WIDTH

VMEM_LIMIT_BYTES = 56 * 1024 * 1024
BF16_ROWS = 16

IN_TM = 1024
IN_EVEN_TN = 1024
IN_EVEN_NC = 256
IN_EVEN_LAST_ROW_SPLIT = 4
IN_ODD_TC = 256
RET_TQ = 1024
ATT_GROUP = 4
POST_TM = 512
POST_NC = 512
HALO = 8


def _rms_rows(x, g):
    ms = jnp.mean(x * x, axis=-1, keepdims=True)
    return x * lax.rsqrt(ms + EPS) * g


def _silu(z):
    return z * jax.nn.sigmoid(z)


def _silu_cols(z, cols):
    del cols
    return _silu(z)


def _in_even_kernel(x_ref, g_ref, w_ref, gain_ref, o_ref, hn_ref):
    j = pl.program_id(1)

    @pl.when(j == 0)
    def _():
        hn_ref[...] = _rms_rows(x_ref[...], g_ref[...]).astype(BF16)

    tn = o_ref.shape[1]
    j_aq, j_av, j_z = OFF_AQ // tn, OFF_AV // tn, OFF_Z // tn

    def tile(epilogue):
        tm = o_ref.shape[0]
        n_chunks = tn // IN_EVEN_NC
        for cc in range(n_chunks):
            n_row = IN_EVEN_LAST_ROW_SPLIT if cc == n_chunks - 1 else 1
            for rr in range(n_row):
                rows = slice(rr * (tm // n_row), (rr + 1) * (tm // n_row))
                acc = jnp.dot(hn_ref[rows, :], w_ref[:, cc * IN_EVEN_NC:(cc + 1) * IN_EVEN_NC],
                              preferred_element_type=F32)
                for hh in range(IN_EVEN_NC // ATT_HD):
                    sl = slice(cc * IN_EVEN_NC + hh * ATT_HD, cc * IN_EVEN_NC + (hh + 1) * ATT_HD)
                    o_ref[rows, sl] = epilogue(acc[:, hh * ATT_HD:(hh + 1) * ATT_HD], sl).astype(BF16)

    @pl.when((j < j_aq) | (j == j_av))
    def _():
        tile(lambda a, sl: a)

    @pl.when((j >= j_aq) & (j < j_av))
    def _():
        tile(lambda a, sl: _rms_rows(a, gain_ref[:, sl]))

    @pl.when(j >= j_z)
    def _():
        tile(_silu_cols)


def _in_even(x2, g, w_bf, gain_row):
    m = x2.shape[0]
    tm, tn = IN_TM, IN_EVEN_TN
    return pl.pallas_call(
        _in_even_kernel,
        out_shape=jax.ShapeDtypeStruct((m, IN_WIDTH_EVEN), BF16),
        grid=(m // tm, IN_WIDTH_EVEN // tn),
        in_specs=[
            pl.BlockSpec((tm, D_MODEL), lambda i, j: (i, 0)),
            pl.BlockSpec((1, D_MODEL), lambda i, j: (0, 0)),
            pl.BlockSpec((D_MODEL, tn), lambda i, j: (0, j)),
            pl.BlockSpec((1, tn), lambda i, j: (0, j)),
        ],
        out_specs=pl.BlockSpec((tm, tn), lambda i, j: (i, j)),
        scratch_shapes=[pltpu.VMEM((tm, D_MODEL), BF16)],
        compiler_params=pltpu.CompilerParams(
            dimension_semantics=("arbitrary", "arbitrary"), vmem_limit_bytes=VMEM_LIMIT_BYTES),
        name="in_even",
    )(x2, g, w_bf, gain_row)


def _ret_kernel(q_ref, k_ref, v_ref, z_ref, gn_ref, o_ref, r_ref):
    c = RET_CHUNK
    tq = q_ref.shape[0]

    @pl.when(pl.program_id(1) == 0)
    def _():
        r_ref[...] = jnp.zeros(r_ref.shape, F32)

    row = lax.broadcasted_iota(jnp.int32, (c, c), 0)
    col = lax.broadcasted_iota(jnp.int32, (c, c), 1)
    diff = (row - col).astype(F32)
    pos = lax.broadcasted_iota(jnp.int32, (c, 1), 0).astype(F32)
    kscale = RET_DK ** -0.5

    for h in range(RET_HEADS):
        lg = math.log(1.0 - 2.0 ** (-5.0 - h))
        decay = jnp.where(diff >= 0, jnp.exp(lg * jnp.maximum(diff, 0.0)), 0.0) * kscale
        xi = jnp.exp(lg * (pos + 1.0))
        zeta = jnp.exp(lg * (c - 1.0 - pos)) * kscale
        g_chunk = math.exp(lg * c)
        qk_sl = slice(h * RET_DK, (h + 1) * RET_DK)
        v_sl = slice(h * RET_DV, (h + 1) * RET_DV)
        gn = gn_ref[h:h + 1, :]
        chunks = [slice(ci * c, (ci + 1) * c) for ci in range(tq // c)]

        contrib = []
        for rows in chunks:
            kz_t = (k_ref[rows, qk_sl].astype(F32) * zeta).T.astype(BF16)
            contrib.append(jnp.dot(kz_t, v_ref[rows, v_sl], preferred_element_type=F32))

        state = r_ref[h]
        states = []
        for s_c in contrib:
            states.append(state)
            state = g_chunk * state + s_c
        r_ref[h] = state

        for rows, r_prev in zip(chunks, states):
            q = q_ref[rows, qk_sl]
            v = v_ref[rows, v_sl]
            s = lax.dot_general(q, k_ref[rows, qk_sl], (((1,), (1,)), ((), ())),
                                preferred_element_type=F32) * decay
            y = jnp.dot(s.astype(BF16), v, preferred_element_type=F32)
            y = y + jnp.dot(q, r_prev.astype(BF16), preferred_element_type=F32) * xi
            mu = jnp.mean(y, axis=-1, keepdims=True)
            yc = y - mu
            var = jnp.mean(yc * yc, axis=-1, keepdims=True)
            yn = yc * lax.rsqrt(var + EPS) * gn
            o_ref[rows, v_sl] = (yn * z_ref[rows, v_sl].astype(F32)).astype(BF16)


def _retention(proj, gn_g, batch, seq):
    m = proj.shape[0]
    tq = RET_TQ
    nt = seq // tq
    row = lambda b, t: b * nt + t
    return pl.pallas_call(
        _ret_kernel,
        out_shape=jax.ShapeDtypeStruct((m, RET_WIDTH), BF16),
        grid=(batch, nt),
        in_specs=[
            pl.BlockSpec((tq, RET_QK_WIDTH), lambda b, t: (row(b, t), OFF_RQ // RET_QK_WIDTH)),
            pl.BlockSpec((tq, RET_QK_WIDTH), lambda b, t: (row(b, t), OFF_RK // RET_QK_WIDTH)),
            pl.BlockSpec((tq, RET_WIDTH), lambda b, t: (row(b, t), OFF_RV // RET_WIDTH)),
            pl.BlockSpec((tq, RET_WIDTH), lambda b, t: (row(b, t), OFF_Z // RET_WIDTH)),
            pl.BlockSpec((RET_HEADS, RET_DV), lambda b, t: (0, 0)),
        ],
        out_specs=pl.BlockSpec((tq, RET_WIDTH), lambda b, t: (row(b, t), 0)),
        scratch_shapes=[pltpu.VMEM((RET_HEADS, RET_DK, RET_DV), F32)],
        compiler_params=pltpu.CompilerParams(
            dimension_semantics=("arbitrary", "arbitrary"), vmem_limit_bytes=VMEM_LIMIT_BYTES),
        name="retention",
    )(proj, proj, proj, proj, gn_g)


def _att_kernel(*refs, n_cast):
    slopes_ref, q_ref, k_ref, v_ref, z_ref = refs[:5]
    cast_in, o_ref, cast_out = refs[5:5 + n_cast], refs[5 + n_cast], refs[6 + n_cast:6 + 2 * n_cast]
    (stage_q, stage_k, stage_v, q4, k4, v4, qp4, qp16, kp1, kp4, kp16, vp1, vp4, vp16, bias_ref, s_ref, pn_ref,
     o0, o1, o2, l0, l1, l2) = refs[6 + 2 * n_cast:]
    for src, dst in zip(cast_in, cast_out):
        dst[...] = src[...].astype(BF16)
    blk = ATT_BLOCK
    grp = ATT_GROUP
    seq = q_ref.shape[0]
    nblk = seq // blk
    n_it = nblk // grp
    assert grp == 4 and DILATED_PATTERNS == ((128, 1), (512, 4), (2048, 16))
    assert n_it % 2 == 0 and n_it >= 4 and nblk % 16 == 0
    slope = slopes_ref[pl.program_id(1)]
    nt = (((1,), (1,)), ((), ()))

    def natural_rows(bi, dil):
        nb = nblk // dil
        start = (bi // nb) + (bi % nb) * (blk * dil)
        if dil == 1:
            return pl.ds(pl.multiple_of(start, blk), blk)
        return pl.ds(start, blk, stride=dil)

    def block_rows(bi, lead=0):
        return pl.ds(pl.multiple_of(bi * blk + lead, blk), blk)

    for pad in (kp1, kp4, kp16, vp1, vp4, vp16):
        pad[0:blk, :] = jnp.zeros((blk, ATT_HD), BF16)
    kp1[blk:, :] = k_ref[...]
    vp1[blk:, :] = v_ref[...]

    def copy_4(it):
        rows = pl.ds(pl.multiple_of(it * (grp * blk), grp * blk), grp * blk)
        stage_q[...] = q_ref[rows, :].astype(F32)
        stage_k[...] = k_ref[rows, :].astype(F32)
        stage_v[...] = v_ref[rows, :].astype(F32)
        for r in range(4):
            bi = r * (nblk // 4) + it
            src = pl.ds(r, blk, stride=4)
            q, k, v = stage_q[src, :], stage_k[src, :], stage_v[src, :]
            q4[block_rows(bi), :], k4[block_rows(bi), :], v4[block_rows(bi), :] = q, k, v
            qp4[block_rows(bi), :] = q.astype(BF16)
            kp4[block_rows(bi, blk), :] = k.astype(BF16)
            vp4[block_rows(bi, blk), :] = v.astype(BF16)

    def copy_16(it):
        halves = (nblk // 4) // grp
        r4, n16 = it // halves, it % halves
        for c in range(4):
            bi = (r4 + 4 * c) * (nblk // 16) + n16
            src = pl.ds(r4 * (seq // 4) + n16 * (grp * blk) + c, blk, stride=4)
            qp16[block_rows(bi), :] = q4[src, :].astype(BF16)
            kp16[block_rows(bi, blk), :] = k4[src, :].astype(BF16)
            vp16[block_rows(bi, blk), :] = v4[src, :].astype(BF16)

    qi = lax.broadcasted_iota(jnp.int32, (blk, 2 * blk), 0)
    kj = lax.broadcasted_iota(jnp.int32, (blk, 2 * blk), 1)
    rel = qi - kj + blk
    for pi, (window, dil) in enumerate(DILATED_PATTERNS):
        valid = (rel >= 0) & (rel <= window // dil)
        alibi = -slope * (rel * dil).astype(F32)
        bias_ref[pi, 0] = jnp.where(valid & (kj >= blk), alibi, NEG)
        bias_ref[pi, 1] = jnp.where(valid, alibi, NEG)

    def make_stages(pi, dil, q_src, kp, vp, o_g, l_g, copy_next):
        nb = nblk // dil

        def scores(it):
            for g in range(grp):
                bi = it * grp + g
                kb = kp[pl.ds(pl.multiple_of(bi * blk, blk), 2 * blk), :]
                s_ref[it % 2, g] = lax.dot_general(q_src[block_rows(bi), :], kb, nt, preferred_element_type=F32)
            if copy_next is not None:
                copy_next(it)

        def probs(it):
            for g in range(grp):
                bi = it * grp + g
                s = s_ref[it % 2, g] + bias_ref[pi, jnp.minimum(bi % nb, 1)]
                mx = jnp.max(jnp.maximum(s[:, :blk], s[:, blk:]), axis=-1, keepdims=True)
                pr = jnp.exp(s - mx)
                den = jnp.sum(pr[:, :blk] + pr[:, blk:], axis=-1, keepdims=True)
                pn_ref[it % 2, g] = (pr * (1.0 / den)).astype(BF16)
                l_g[natural_rows(bi, dil), :] = jnp.broadcast_to(mx + jnp.log(den), (blk, ATT_HD))

        def values(it):
            for g in range(grp):
                bi = it * grp + g
                vb = vp[pl.ds(pl.multiple_of(bi * blk, blk), 2 * blk), :]
                o_g[natural_rows(bi, dil), :] = jnp.dot(pn_ref[it % 2, g], vb, preferred_element_type=F32)

        return scores, probs, values

    stages = [make_stages(pi, dil, *refs) for pi, ((_, dil), *refs) in enumerate(zip(
        DILATED_PATTERNS, (q_ref, qp4, qp16), (kp1, kp4, kp16), (vp1, vp4, vp16), (o0, o1, o2), (l0, l1, l2),
        (copy_4, copy_16, None)))]

    def fill(scores, probs, values):
        scores(0)
        probs(0)
        scores(1)

    def drain(scores, probs, values):
        values(n_it - 2)
        probs(n_it - 1)
        values(n_it - 1)

    fill(*stages[0])
    for pi, (scores, probs, values) in enumerate(stages):
        def body(it, carry, scores=scores, probs=probs, values=values):
            values(it - 2)
            probs(it - 1)
            scores(it)
            return carry

        lax.fori_loop(2, n_it, body, 0)
        drain(scores, probs, values)
        if pi + 1 < len(stages):
            fill(*stages[pi + 1])

    mrows = 256

    def merge(ci, carry):
        rows = pl.ds(pl.multiple_of(ci * mrows, mrows), mrows)
        a0, a1, a2 = l0[rows, :], l1[rows, :], l2[rows, :]
        mx = jnp.maximum(jnp.maximum(a0, a1), a2)
        e0, e1, e2 = jnp.exp(a0 - mx), jnp.exp(a1 - mx), jnp.exp(a2 - mx)
        y = (e0 * o0[rows, :] + e1 * o1[rows, :] + e2 * o2[rows, :]) / (e0 + e1 + e2)
        o_ref[rows, :] = (y * z_ref[rows, :].astype(F32)).astype(BF16)
        return carry

    lax.fori_loop(0, seq // mrows, merge, 0)


def _attention(proj, slopes, batch, seq, to_cast):
    m = proj.shape[0]
    hd = ATT_HD
    n_steps = batch * ATT_HEADS
    spec = lambda off: pl.BlockSpec((seq, hd), lambda b, h: (b, off // hd + h))
    slab = lambda a: pl.BlockSpec((a.shape[0] // n_steps, a.shape[1]), lambda b, h: (b * ATT_HEADS + h, 0))
    assert all(a.shape[0] % (BF16_ROWS * n_steps) == 0 for a in to_cast)
    f32_rows = pltpu.VMEM((seq, hd), F32)
    bf16_rows = pltpu.VMEM((seq, hd), BF16)
    bf16_padded = pltpu.VMEM((seq + ATT_BLOCK, hd), BF16)
    f32_group = pltpu.VMEM((ATT_GROUP * ATT_BLOCK, hd), F32)
    y_att, *cast = pl.pallas_call(
        functools.partial(_att_kernel, n_cast=len(to_cast)),
        out_shape=[jax.ShapeDtypeStruct((m, ATT_WIDTH), BF16)]
        + [jax.ShapeDtypeStruct(a.shape, BF16) for a in to_cast],
        grid=(batch, ATT_HEADS),
        in_specs=[
            pl.BlockSpec(memory_space=pltpu.SMEM),
            spec(OFF_AQ), spec(OFF_AK), spec(OFF_AV), spec(OFF_Z + RET_WIDTH),
        ] + [slab(a) for a in to_cast],
        out_specs=[pl.BlockSpec((seq, hd), lambda b, h: (b, h))] + [slab(a) for a in to_cast],
        scratch_shapes=[
            f32_group, f32_group, f32_group, f32_rows, f32_rows, f32_rows,
            bf16_rows, bf16_rows,
            bf16_padded, bf16_padded, bf16_padded, bf16_padded, bf16_padded, bf16_padded,
            pltpu.VMEM((len(DILATED_PATTERNS), 2, ATT_BLOCK, 2 * ATT_BLOCK), F32),
            pltpu.VMEM((2, ATT_GROUP, ATT_BLOCK, 2 * ATT_BLOCK), F32),
            pltpu.VMEM((2, ATT_GROUP, ATT_BLOCK, 2 * ATT_BLOCK), BF16),
            f32_rows, f32_rows, f32_rows, f32_rows, f32_rows, f32_rows,
        ],
        compiler_params=pltpu.CompilerParams(
            dimension_semantics=("arbitrary", "arbitrary"), vmem_limit_bytes=VMEM_LIMIT_BYTES),
        name="dilated_attention",
    )(slopes, proj, proj, proj, proj, *to_cast)
    return y_att, cast


def _post_tail(x1, p_ref, pg_ref, wg_ref, wp_ref, o_ref):
    h2 = _rms_rows(x1, pg_ref[...]).astype(BF16)
    pb = p_ref[...].astype(BF16)
    for cc in range(D_MODEL // POST_NC):
        sl = slice(cc * POST_NC, (cc + 1) * POST_NC)
        gate = jax.nn.sigmoid(jnp.dot(h2, wg_ref[:, sl], preferred_element_type=F32))
        pp = jnp.dot(pb, wp_ref[:, sl], preferred_element_type=F32)
        o_ref[:, sl] = x1[:, sl] + gate * pp


def _post_kernel(x_ref, ya_ref, yb_ref, p_ref, wa_ref, wb_ref, pg_ref, wg_ref, wp_ref, o_ref):
    x1 = x_ref[...] + jnp.dot(ya_ref[...], wa_ref[...], preferred_element_type=F32)
    x1 = x1 + jnp.dot(yb_ref[...], wb_ref[...], preferred_element_type=F32)
    _post_tail(x1, p_ref, pg_ref, wg_ref, wp_ref, o_ref)


def _const_spec(shape):
    return pl.BlockSpec(shape, lambda i: (0,) * len(shape), pipeline_mode=pl.Buffered(1))


def _ple_rows_spec(layer, m, tm):
    return pl.BlockSpec((tm, PLE_DIM), lambda i: (layer * (m // tm) + i, 0))


def _post(x2, ya, ya_blk, yb, yb_blk, p_all, layer, w_out_bf, pg, wg_all_bf, wp_all_bf, name):
    m = x2.shape[0]
    tm = POST_TM
    half = D_MODEL // 2
    rows = lambda w, blk=0: pl.BlockSpec((tm, w), lambda i: (i, blk))
    layer_spec = lambda k: pl.BlockSpec((None, k, D_MODEL), lambda i: (layer, 0, 0), pipeline_mode=pl.Buffered(1))
    return pl.pallas_call(
        _post_kernel,
        out_shape=jax.ShapeDtypeStruct((m, D_MODEL), F32),
        grid=(m // tm,),
        in_specs=[
            rows(D_MODEL), rows(half, ya_blk), rows(half, yb_blk), _ple_rows_spec(layer, m, tm),
            pl.BlockSpec((half, D_MODEL), lambda i: (0, 0), pipeline_mode=pl.Buffered(1)),
            pl.BlockSpec((half, D_MODEL), lambda i: (1, 0), pipeline_mode=pl.Buffered(1)),
            _const_spec((1, D_MODEL)), layer_spec(D_MODEL), layer_spec(PLE_DIM),
        ],
        out_specs=rows(D_MODEL),
        compiler_params=pltpu.CompilerParams(
            dimension_semantics=("arbitrary",), vmem_limit_bytes=VMEM_LIMIT_BYTES),
        name=name,
    )(x2, ya, yb, p_all, w_out_bf, w_out_bf, pg, wg_all_bf, wp_all_bf)


def _in_odd_kernel(x_ref, g_ref, wb_ref, wc_ref, wu_ref, wz_ref, cw_ref, y_ref, hn_ref, ext_ref, halo_ref, *,
                   tiles_per_seq):
    i, j = pl.program_id(0), pl.program_id(1)
    tm = x_ref.shape[0]

    @pl.when(j == 0)
    def _():
        hn_ref[...] = _rms_rows(x_ref[...], g_ref[...]).astype(BF16)

    @pl.when((i == 0) & (j == 0))
    def _():
        halo_ref[...] = jnp.zeros(halo_ref.shape, F32)

    hn = hn_ref[...]
    cu = (jnp.dot(hn, wc_ref[...], preferred_element_type=F32)
          * jnp.dot(hn, wu_ref[...], preferred_element_type=F32))
    first = (i % tiles_per_seq) == 0
    ext_ref[0:HALO, :] = jnp.where(first, 0.0, halo_ref[j])
    ext_ref[HALO:HALO + tm, :] = cu
    halo_ref[j] = cu[tm - HALO:, :]
    conv = (cw_ref[0:1, :] * ext_ref[HALO - 2:HALO - 2 + tm, :]
            + cw_ref[1:2, :] * ext_ref[HALO - 1:HALO - 1 + tm, :]
            + cw_ref[2:3, :] * cu)
    bg = jnp.dot(hn, wb_ref[...], preferred_element_type=F32)
    z = jnp.dot(hn, wz_ref[...], preferred_element_type=F32)
    y_ref[...] = (bg * conv * _silu(z)).astype(BF16)


def _in_odd(x2, g, w_bf, conv_w, seq):
    m = x2.shape[0]
    tm, tc = IN_TM, IN_ODD_TC
    nb = D_MODEL // tc
    wspec = lambda seg: pl.BlockSpec((D_MODEL, tc), lambda i, j: (0, seg * nb + j))
    return pl.pallas_call(
        functools.partial(_in_odd_kernel, tiles_per_seq=seq // tm),
        out_shape=jax.ShapeDtypeStruct((m, D_MODEL), BF16),
        grid=(m // tm, nb),
        in_specs=[
            pl.BlockSpec((tm, D_MODEL), lambda i, j: (i, 0)),
            pl.BlockSpec((1, D_MODEL), lambda i, j: (0, 0)),
            wspec(0), wspec(1), wspec(2), wspec(3),
            pl.BlockSpec((CONV_WIDTH, tc), lambda i, j: (0, j)),
        ],
        out_specs=pl.BlockSpec((tm, tc), lambda i, j: (i, j)),
        scratch_shapes=[
            pltpu.VMEM((tm, D_MODEL), BF16),
            pltpu.VMEM((HALO + tm, tc), F32),
            pltpu.VMEM((nb, HALO, tc), F32),
        ],
        compiler_params=pltpu.CompilerParams(
            dimension_semantics=("arbitrary", "arbitrary"), vmem_limit_bytes=VMEM_LIMIT_BYTES),
        name="in_odd",
    )(x2, g, w_bf, w_bf, w_bf, w_bf, conv_w)


def _even_gain_row(q_g, k_g):
    ones = jnp.ones((OFF_AQ,), F32)
    return jnp.concatenate([
        ones,
        jnp.tile(q_g * ATT_HD ** -0.5, ATT_HEADS),
        jnp.tile(k_g, ATT_HEADS),
        jnp.ones((IN_WIDTH_EVEN - OFF_AV,), F32),
    ]).reshape(1, IN_WIDTH_EVEN)


def kernel(x, p, pre_norm_g, w_in_even, q_norm_g, k_norm_g, ret_gn_g, w_out_even, w_in_odd, conv_w_odd,
           w_out_odd, ple_norm_g, w_ple_gate, w_ple_proj):
    batch, seq, d = x.shape
    depth = p.shape[0]
    assert d == D_MODEL and seq % (ATT_BLOCK * DILATED_PATTERNS[-1][1]) == 0 and seq % IN_TM == 0
    m = batch * seq
    slopes = jnp.asarray([(2.0 ** (-8.0 / ATT_HEADS)) ** (i + 1) for i in range(ATT_HEADS)], F32)
    x2 = x.reshape(m, d)
    p_all = p.reshape(depth * m, PLE_DIM)
    late_f32 = dict(w_out_even=w_out_even, w_ple_gate=w_ple_gate, w_ple_proj=w_ple_proj, w_in_odd=w_in_odd,
                    w_out_odd=w_out_odd)
    if w_in_even.shape[0] > 1:
        late_f32["w_in_even_rest"] = w_in_even[1:]
    bf = None
    for i in range(depth):
        j = i // 2
        pg = ple_norm_g[i].reshape(1, d)
        g = pre_norm_g[i].reshape(1, d)
        if i % 2 == 0:
            w_in_bf = w_in_even[0].astype(BF16) if j == 0 else bf["w_in_even_rest"][j - 1]
            proj = _in_even(x2, g, w_in_bf, _even_gain_row(q_norm_g[j], k_norm_g[j]))
            y_ret = _retention(proj, ret_gn_g[j], batch, seq)
            to_cast = [a.reshape(-1, a.shape[-1]) for a in late_f32.values()] if bf is None else []
            y_att, cast = _attention(proj, slopes, batch, seq, to_cast)
            if bf is None:
                bf = {name: c.reshape(a.shape) for (name, a), c in zip(late_f32.items(), cast)}
            x2 = _post(x2, y_ret, 0, y_att, 0, p_all, i, bf["w_out_even"][j], pg, bf["w_ple_gate"],
                       bf["w_ple_proj"], "post_even")
        else:
            y = _in_odd(x2, g, bf["w_in_odd"][j], conv_w_odd[j], seq)
            x2 = _post(x2, y, 0, y, 1, p_all, i, bf["w_out_odd"][j], pg, bf["w_ple_gate"], bf["w_ple_proj"],
                       "post_odd")
    return x2.reshape(batch, seq, d)
```

```python
import functools
import math

import jax
import jax.numpy as jnp
from jax import lax
from jax.experimental import pallas as pl
from jax.experimental.pallas import tpu as pltpu

F32 = jnp.float32
BF16 = jnp.bfloat16

D_MODEL = 2048
PLE_DIM = 256
EPS = 1e-6
NEG = -1e30
RET_DK = 128
RET_DV = 256
RET_HEADS = 4
RET_CHUNK = 256
ATT_HD = 128
ATT_HEADS = 8
ATT_BLOCK = 128
DILATED_PATTERNS = ((128, 1), (512, 4), (2048, 16))
RET_QK_WIDTH = RET_HEADS * RET_DK
RET_WIDTH = RET_HEADS * RET_DV
ATT_WIDTH = ATT_HEADS * ATT_HD
IN_WIDTH_EVEN = 2 * RET_QK_WIDTH + RET_WIDTH + 3 * ATT_WIDTH + RET_WIDTH + ATT_WIDTH
CONV_WIDTH = 3

OFF_RQ = 0
OFF_RK = RET_QK_WIDTH
OFF_RV = 2 * RET_QK_WIDTH
OFF_AQ = OFF_RV + RET_WIDTH
OFF_AK = OFF_AQ + ATT_WIDTH
OFF_AV = OFF_AK + ATT_WIDTH
OFF_Z = OFF_AV + ATT_WIDTH

VMEM_LIMIT_BYTES = 56 * 1024 * 1024
BF16_ROWS = 16

IN_TM = 1024
IN_EVEN_TN = 1024
IN_EVEN_NC = 256
IN_EVEN_LAST_ROW_SPLIT = 4
IN_ODD_TC = 256
RET_TQ = 1024
ATT_GROUP = 4
ATT_MERGE_ROWS = 256
POST_TM = 512
POST_NC = 256
HALO = 8


def _rms_rows(x, g):
    ms = jnp.mean(x * x, axis=-1, keepdims=True)
    return x * lax.rsqrt(ms + EPS) * g


def _silu(z):
    return z * jax.nn.sigmoid(z)


def _silu_cols(z, cols):
    del cols
    return _silu(z)


def _in_even_kernel(x_ref, g_ref, w_ref, gain_ref, o_ref, hn_ref):
    j = pl.program_id(1)

    @pl.when(j == 0)
    def _():
        hn_ref[...] = _rms_rows(x_ref[...], g_ref[...]).astype(BF16)

    tn = o_ref.shape[1]
    j_aq, j_av, j_z = OFF_AQ // tn, OFF_AV // tn, OFF_Z // tn

    def tile(epilogue):
        tm = o_ref.shape[0]
        n_chunks = tn // IN_EVEN_NC
        for cc in range(n_chunks):
            n_row = IN_EVEN_LAST_ROW_SPLIT if cc == n_chunks - 1 else 1
            for rr in range(n_row):
                rows = slice(rr * (tm // n_row), (rr + 1) * (tm // n_row))
                acc = jnp.dot(hn_ref[rows, :], w_ref[:, cc * IN_EVEN_NC:(cc + 1) * IN_EVEN_NC],
                              preferred_element_type=F32)
                for hh in range(IN_EVEN_NC // ATT_HD):
                    sl = slice(cc * IN_EVEN_NC + hh * ATT_HD, cc * IN_EVEN_NC + (hh + 1) * ATT_HD)
                    o_ref[rows, sl] = epilogue(acc[:, hh * ATT_HD:(hh + 1) * ATT_HD], sl).astype(BF16)

    @pl.when((j < j_aq) | (j == j_av))
    def _():
        tile(lambda a, sl: a)

    @pl.when((j >= j_aq) & (j < j_av))
    def _():
        tile(lambda a, sl: _rms_rows(a, gain_ref[:, sl]))

    @pl.when(j >= j_z)
    def _():
        tile(_silu_cols)


def _in_even(x2, g, w_bf, gain_row):
    m = x2.shape[0]
    tm, tn = IN_TM, IN_EVEN_TN
    return pl.pallas_call(
        _in_even_kernel,
        out_shape=jax.ShapeDtypeStruct((m, IN_WIDTH_EVEN), BF16),
        grid=(m // tm, IN_WIDTH_EVEN // tn),
        in_specs=[
            pl.BlockSpec((tm, D_MODEL), lambda i, j: (i, 0)),
            pl.BlockSpec((1, D_MODEL), lambda i, j: (0, 0)),
            pl.BlockSpec((D_MODEL, tn), lambda i, j: (0, j)),
            pl.BlockSpec((1, tn), lambda i, j: (0, j)),
        ],
        out_specs=pl.BlockSpec((tm, tn), lambda i, j: (i, j)),
        scratch_shapes=[pltpu.VMEM((tm, D_MODEL), BF16)],
        compiler_params=pltpu.CompilerParams(
            dimension_semantics=("arbitrary", "arbitrary"), vmem_limit_bytes=VMEM_LIMIT_BYTES),
        name="in_even",
    )(x2, g, w_bf, gain_row)


def _ret_kernel(q_ref, k_ref, v_ref, z_ref, gn_ref, o_ref, r_ref):
    c = RET_CHUNK
    tq = q_ref.shape[0]

    @pl.when(pl.program_id(1) == 0)
    def _():
        r_ref[...] = jnp.zeros(r_ref.shape, F32)

    row = lax.broadcasted_iota(jnp.int32, (c, c), 0)
    col = lax.broadcasted_iota(jnp.int32, (c, c), 1)
    diff = (row - col).astype(F32)
    pos = lax.broadcasted_iota(jnp.int32, (c, 1), 0).astype(F32)
    kscale = RET_DK ** -0.5

    for h in range(RET_HEADS):
        lg = math.log(1.0 - 2.0 ** (-5.0 - h))
        decay = jnp.where(diff >= 0, jnp.exp(lg * jnp.maximum(diff, 0.0)), 0.0) * kscale
        xi = jnp.exp(lg * (pos + 1.0))
        zeta = jnp.exp(lg * (c - 1.0 - pos)) * kscale
        g_chunk = math.exp(lg * c)
        qk_sl = slice(h * RET_DK, (h + 1) * RET_DK)
        v_sl = slice(h * RET_DV, (h + 1) * RET_DV)
        gn = gn_ref[h:h + 1, :]
        chunks = [slice(ci * c, (ci + 1) * c) for ci in range(tq // c)]

        contrib = []
        for rows in chunks:
            kz_t = (k_ref[rows, qk_sl].astype(F32) * zeta).T.astype(BF16)
            contrib.append(jnp.dot(kz_t, v_ref[rows, v_sl], preferred_element_type=F32))

        state = r_ref[h]
        states = []
        for s_c in contrib:
            states.append(state)
            state = g_chunk * state + s_c
        r_ref[h] = state

        for rows, r_prev in zip(chunks, states):
            q = q_ref[rows, qk_sl]
            v = v_ref[rows, v_sl]
            s = lax.dot_general(q, k_ref[rows, qk_sl], (((1,), (1,)), ((), ())),
                                preferred_element_type=F32) * decay
            y = jnp.dot(s.astype(BF16), v, preferred_element_type=F32)
            y = y + jnp.dot(q, r_prev.astype(BF16), preferred_element_type=F32) * xi
            mu = jnp.mean(y, axis=-1, keepdims=True)
            yc = y - mu
            var = jnp.mean(yc * yc, axis=-1, keepdims=True)
            yn = yc * lax.rsqrt(var + EPS) * gn
            o_ref[rows, v_sl] = (yn * z_ref[rows, v_sl].astype(F32)).astype(BF16)


def _retention(proj, gn_g, batch, seq):
    m = proj.shape[0]
    tq = RET_TQ
    nt = seq // tq
    row = lambda b, t: b * nt + t
    return pl.pallas_call(
        _ret_kernel,
        out_shape=jax.ShapeDtypeStruct((m, RET_WIDTH), BF16),
        grid=(batch, nt),
        in_specs=[
            pl.BlockSpec((tq, RET_QK_WIDTH), lambda b, t: (row(b, t), OFF_RQ // RET_QK_WIDTH)),
            pl.BlockSpec((tq, RET_QK_WIDTH), lambda b, t: (row(b, t), OFF_RK // RET_QK_WIDTH)),
            pl.BlockSpec((tq, RET_WIDTH), lambda b, t: (row(b, t), OFF_RV // RET_WIDTH)),
            pl.BlockSpec((tq, RET_WIDTH), lambda b, t: (row(b, t), OFF_Z // RET_WIDTH)),
            pl.BlockSpec((RET_HEADS, RET_DV), lambda b, t: (0, 0)),
        ],
        out_specs=pl.BlockSpec((tq, RET_WIDTH), lambda b, t: (row(b, t), 0)),
        scratch_shapes=[pltpu.VMEM((RET_HEADS, RET_DK, RET_DV), F32)],
        compiler_params=pltpu.CompilerParams(
            dimension_semantics=("arbitrary", "arbitrary"), vmem_limit_bytes=VMEM_LIMIT_BYTES),
        name="retention",
    )(proj, proj, proj, proj, gn_g)


def _att_kernel(*refs, n_cast):
    slopes_ref, q_ref, k_ref, v_ref, z_ref = refs[:5]
    cast_in, o_ref, cast_out = refs[5:5 + n_cast], refs[5 + n_cast], refs[6 + n_cast:6 + 2 * n_cast]
    (stage_q, stage_k, stage_v, q4, k4, v4, qp4, qp16, kp1, kp4, kp16, vp1, vp4, vp16, bias_ref, s_ref, pn_ref,
     o0, o1, o2, l0, l1, l2) = refs[6 + 2 * n_cast:]
    for src, dst in zip(cast_in, cast_out):
        dst[...] = src[...].astype(BF16)
    blk = ATT_BLOCK
    grp = ATT_GROUP
    seq = q_ref.shape[0]
    nblk = seq // blk
    n_it = nblk // grp
    d4, d16 = DILATED_PATTERNS[1][1], DILATED_PATTERNS[2][1]
    assert DILATED_PATTERNS[0][1] == 1 and grp == d4 and d16 == d4 * d4
    assert n_it % 2 == 0 and n_it >= 4 and nblk % d16 == 0
    slope = slopes_ref[pl.program_id(1)]
    nt = (((1,), (1,)), ((), ()))

    def natural_rows(bi, dil):
        nb = nblk // dil
        start = (bi // nb) + (bi % nb) * (blk * dil)
        if dil == 1:
            return pl.ds(pl.multiple_of(start, blk), blk)
        return pl.ds(start, blk, stride=dil)

    def block_rows(bi, lead=0):
        return pl.ds(pl.multiple_of(bi * blk + lead, blk), blk)

    for pad in (kp1, kp4, kp16, vp1, vp4, vp16):
        pad[0:blk, :] = jnp.zeros((blk, ATT_HD), BF16)
    kp1[blk:, :] = k_ref[...]
    vp1[blk:, :] = v_ref[...]

    def copy_4(it):
        rows = pl.ds(pl.multiple_of(it * (grp * blk), grp * blk), grp * blk)
        stage_q[...] = q_ref[rows, :].astype(F32)
        stage_k[...] = k_ref[rows, :].astype(F32)
        stage_v[...] = v_ref[rows, :].astype(F32)
        for r in range(d4):
            bi = r * (nblk // d4) + it
            src = pl.ds(r, blk, stride=d4)
            q, k, v = stage_q[src, :], stage_k[src, :], stage_v[src, :]
            q4[block_rows(bi), :], k4[block_rows(bi), :], v4[block_rows(bi), :] = q, k, v
            qp4[block_rows(bi), :] = q.astype(BF16)
            kp4[block_rows(bi, blk), :] = k.astype(BF16)
            vp4[block_rows(bi, blk), :] = v.astype(BF16)

    def copy_16(it):
        halves = (nblk // d4) // grp
        r4, n16 = it // halves, it % halves
        for c in range(d4):
            bi = (r4 + d4 * c) * (nblk // d16) + n16
            src = pl.ds(r4 * (seq // d4) + n16 * (grp * blk) + c, blk, stride=d4)
            qp16[block_rows(bi), :] = q4[src, :].astype(BF16)
            kp16[block_rows(bi, blk), :] = k4[src, :].astype(BF16)
            vp16[block_rows(bi, blk), :] = v4[src, :].astype(BF16)

    qi = lax.broadcasted_iota(jnp.int32, (blk, 2 * blk), 0)
    kj = lax.broadcasted_iota(jnp.int32, (blk, 2 * blk), 1)
    rel = qi - kj + blk
    for pi, (window, dil) in enumerate(DILATED_PATTERNS):
        valid = (rel >= 0) & (rel <= window // dil)
        alibi = -slope * (rel * dil).astype(F32)
        bias_ref[pi, 0] = jnp.where(valid & (kj >= blk), alibi, NEG)
        bias_ref[pi, 1] = jnp.where(valid, alibi, NEG)

    o16p, l16p = q4, k4

    def rows_in_d4_order(bi16):
        r16, n16 = bi16 // (nblk // d16), bi16 % (nblk // d16)
        return pl.ds((r16 % d4) * (seq // d4) + n16 * (grp * blk) + r16 // d4, blk, stride=d4)

    def un_permute(it):
        for g in range(grp):
            bi4 = it * grp + g
            o2[natural_rows(bi4, d4), :] = o16p[block_rows(bi4), :]
            l2[natural_rows(bi4, d4), :] = l16p[block_rows(bi4), :]

    def make_stages(pi, dil, q_src, kp, vp, o_g, l_g, copy_next):
        nb = nblk // dil
        two_step = dil == d16
        block_of = (lambda it, g: it + n_it * g) if two_step else (lambda it, g: it * grp + g)
        out_rows = rows_in_d4_order if two_step else (lambda bi: natural_rows(bi, dil))
        o_dst, l_dst = (o16p, l16p) if two_step else (o_g, l_g)

        def scores(it):
            for g in range(grp):
                bi = block_of(it, g)
                kb = kp[pl.ds(pl.multiple_of(bi * blk, blk), 2 * blk), :]
                s_ref[it % 2, g] = lax.dot_general(q_src[block_rows(bi), :], kb, nt, preferred_element_type=F32)
            if copy_next is not None:
                copy_next(it)

        def probs(it):
            for g in range(grp):
                bi = block_of(it, g)
                s = s_ref[it % 2, g] + bias_ref[pi, jnp.minimum(bi % nb, 1)]
                mx = jnp.max(jnp.maximum(s[:, :blk], s[:, blk:]), axis=-1, keepdims=True)
                pr = jnp.exp(s - mx)
                den = jnp.sum(pr[:, :blk] + pr[:, blk:], axis=-1, keepdims=True)
                pn_ref[it % 2, g] = (pr * (1.0 / den)).astype(BF16)
                l_dst[out_rows(bi), :] = jnp.broadcast_to(mx + jnp.log(den), (blk, ATT_HD))

        def values(it):
            for g in range(grp):
                bi = block_of(it, g)
                vb = vp[pl.ds(pl.multiple_of(bi * blk, blk), 2 * blk), :]
                o_dst[out_rows(bi), :] = jnp.dot(pn_ref[it % 2, g], vb, preferred_element_type=F32)
            if two_step:
                un_permute(it)

        return scores, probs, values

    stages = [make_stages(pi, dil, *refs) for pi, ((_, dil), *refs) in enumerate(zip(
        DILATED_PATTERNS, (q_ref, qp4, qp16), (kp1, kp4, kp16), (vp1, vp4, vp16), (o0, o1, o2), (l0, l1, l2),
        (copy_4, copy_16, None)))]

    def fill(scores, probs, values):
        scores(0)
        probs(0)
        scores(1)

    def drain(scores, probs, values):
        values(n_it - 2)
        probs(n_it - 1)
        values(n_it - 1)

    fill(*stages[0])
    for pi, (scores, probs, values) in enumerate(stages):
        def body(it, carry, scores=scores, probs=probs, values=values):
            values(it - 2)
            probs(it - 1)
            scores(it)
            return carry

        lax.fori_loop(2, n_it, body, 0)
        drain(scores, probs, values)
        if pi + 1 < len(stages):
            fill(*stages[pi + 1])

    mrows = ATT_MERGE_ROWS

    def merge(ci, carry):
        rows = pl.ds(pl.multiple_of(ci * mrows, mrows), mrows)
        a0, a1, a2 = l0[rows, :], l1[rows, :], l2[rows, :]
        mx = jnp.maximum(jnp.maximum(a0, a1), a2)
        e0, e1, e2 = jnp.exp(a0 - mx), jnp.exp(a1 - mx), jnp.exp(a2 - mx)
        y = (e0 * o0[rows, :] + e1 * o1[rows, :] + e2 * o2[rows, :]) / (e0 + e1 + e2)
        o_ref[rows, :] = (y * z_ref[rows, :].astype(F32)).astype(BF16)
        return carry

    lax.fori_loop(0, seq // mrows, merge, 0)


def _attention(proj, slopes, batch, seq, to_cast):
    m = proj.shape[0]
    hd = ATT_HD
    n_steps = batch * ATT_HEADS
    spec = lambda off: pl.BlockSpec((seq, hd), lambda b, h: (b, off // hd + h))
    slab = lambda a: pl.BlockSpec((a.shape[0] // n_steps, a.shape[1]), lambda b, h: (b * ATT_HEADS + h, 0))
    assert all(a.shape[0] % (BF16_ROWS * n_steps) == 0 for a in to_cast)
    f32_rows = pltpu.VMEM((seq, hd), F32)
    bf16_rows = pltpu.VMEM((seq, hd), BF16)
    bf16_padded = pltpu.VMEM((seq + ATT_BLOCK, hd), BF16)
    f32_group = pltpu.VMEM((ATT_GROUP * ATT_BLOCK, hd), F32)
    y_att, *cast = pl.pallas_call(
        functools.partial(_att_kernel, n_cast=len(to_cast)),
        out_shape=[jax.ShapeDtypeStruct((m, ATT_WIDTH), BF16)]
        + [jax.ShapeDtypeStruct(a.shape, BF16) for a in to_cast],
        grid=(batch, ATT_HEADS),
        in_specs=[
            pl.BlockSpec(memory_space=pltpu.SMEM),
            spec(OFF_AQ), spec(OFF_AK), spec(OFF_AV), spec(OFF_Z + RET_WIDTH),
        ] + [slab(a) for a in to_cast],
        out_specs=[pl.BlockSpec((seq, hd), lambda b, h: (b, h))] + [slab(a) for a in to_cast],
        scratch_shapes=[
            f32_group, f32_group, f32_group, f32_rows, f32_rows, f32_rows,
            bf16_rows, bf16_rows,
            bf16_padded, bf16_padded, bf16_padded, bf16_padded, bf16_padded, bf16_padded,
            pltpu.VMEM((len(DILATED_PATTERNS), 2, ATT_BLOCK, 2 * ATT_BLOCK), F32),
            pltpu.VMEM((2, ATT_GROUP, ATT_BLOCK, 2 * ATT_BLOCK), F32),
            pltpu.VMEM((2, ATT_GROUP, ATT_BLOCK, 2 * ATT_BLOCK), BF16),
            f32_rows, f32_rows, f32_rows, f32_rows, f32_rows, f32_rows,
        ],
        compiler_params=pltpu.CompilerParams(
            dimension_semantics=("arbitrary", "arbitrary"), vmem_limit_bytes=VMEM_LIMIT_BYTES),
        name="dilated_attention",
    )(slopes, proj, proj, proj, proj, *to_cast)
    return y_att, cast


def _post_tail(x1, p_ref, pg_ref, wg_ref, wp_ref, o_ref):
    h2 = _rms_rows(x1, pg_ref[...]).astype(BF16)
    pb = p_ref[...].astype(BF16)
    for cc in range(D_MODEL // POST_NC):
        sl = slice(cc * POST_NC, (cc + 1) * POST_NC)
        gate = jax.nn.sigmoid(jnp.dot(h2, wg_ref[:, sl], preferred_element_type=F32))
        pp = jnp.dot(pb, wp_ref[:, sl], preferred_element_type=F32)
        o_ref[:, sl] = x1[:, sl] + gate * pp


def _post_kernel(x_ref, ya_ref, yb_ref, p_ref, wa_ref, wb_ref, pg_ref, wg_ref, wp_ref, o_ref):
    x1 = x_ref[...] + jnp.dot(ya_ref[...], wa_ref[...], preferred_element_type=F32)
    x1 = x1 + jnp.dot(yb_ref[...], wb_ref[...], preferred_element_type=F32)
    _post_tail(x1, p_ref, pg_ref, wg_ref, wp_ref, o_ref)


def _const_spec(shape):
    return pl.BlockSpec(shape, lambda i: (0,) * len(shape), pipeline_mode=pl.Buffered(1))


def _ple_rows_spec(layer, m, tm):
    return pl.BlockSpec((tm, PLE_DIM), lambda i: (layer * (m // tm) + i, 0))


def _post(x2, ya, ya_blk, yb, yb_blk, p_all, layer, w_out_bf, pg, wg_all_bf, wp_all_bf, name):
    m = x2.shape[0]
    tm = POST_TM
    half = D_MODEL // 2
    rows = lambda w, blk=0: pl.BlockSpec((tm, w), lambda i: (i, blk))
    layer_spec = lambda k: pl.BlockSpec((None, k, D_MODEL), lambda i: (layer, 0, 0), pipeline_mode=pl.Buffered(1))
    return pl.pallas_call(
        _post_kernel,
        out_shape=jax.ShapeDtypeStruct((m, D_MODEL), F32),
        grid=(m // tm,),
        in_specs=[
            rows(D_MODEL), rows(half, ya_blk), rows(half, yb_blk), _ple_rows_spec(layer, m, tm),
            pl.BlockSpec((half, D_MODEL), lambda i: (0, 0), pipeline_mode=pl.Buffered(1)),
            pl.BlockSpec((half, D_MODEL), lambda i: (1, 0), pipeline_mode=pl.Buffered(1)),
            _const_spec((1, D_MODEL)), layer_spec(D_MODEL), layer_spec(PLE_DIM),
        ],
        out_specs=rows(D_MODEL),
        compiler_params=pltpu.CompilerParams(
            dimension_semantics=("arbitrary",), vmem_limit_bytes=VMEM_LIMIT_BYTES),
        name=name,
    )(x2, ya, yb, p_all, w_out_bf, w_out_bf, pg, wg_all_bf, wp_all_bf)


def _in_odd_kernel(x_ref, g_ref, wb_ref, wc_ref, wu_ref, wz_ref, cw_ref, y_ref, hn_ref, ext_ref, halo_ref, *,
                   tiles_per_seq):
    i, j = pl.program_id(0), pl.program_id(1)
    tm = x_ref.shape[0]

    @pl.when(j == 0)
    def _():
        hn_ref[...] = _rms_rows(x_ref[...], g_ref[...]).astype(BF16)

    @pl.when((i == 0) & (j == 0))
    def _():
        halo_ref[...] = jnp.zeros(halo_ref.shape, F32)

    hn = hn_ref[...]
    cu = (jnp.dot(hn, wc_ref[...], preferred_element_type=F32)
          * jnp.dot(hn, wu_ref[...], preferred_element_type=F32))
    first = (i % tiles_per_seq) == 0
    ext_ref[0:HALO, :] = jnp.where(first, 0.0, halo_ref[j])
    ext_ref[HALO:HALO + tm, :] = cu
    halo_ref[j] = cu[tm - HALO:, :]
    conv = (cw_ref[0:1, :] * ext_ref[HALO - 2:HALO - 2 + tm, :]
            + cw_ref[1:2, :] * ext_ref[HALO - 1:HALO - 1 + tm, :]
            + cw_ref[2:3, :] * cu)
    bg = jnp.dot(hn, wb_ref[...], preferred_element_type=F32)
    z = jnp.dot(hn, wz_ref[...], preferred_element_type=F32)
    y_ref[...] = (bg * conv * _silu(z)).astype(BF16)


def _in_odd(x2, g, w_bf, conv_w, seq):
    m = x2.shape[0]
    tm, tc = IN_TM, IN_ODD_TC
    nb = D_MODEL // tc
    wspec = lambda seg: pl.BlockSpec((D_MODEL, tc), lambda i, j: (0, seg * nb + j))
    return pl.pallas_call(
        functools.partial(_in_odd_kernel, tiles_per_seq=seq // tm),
        out_shape=jax.ShapeDtypeStruct((m, D_MODEL), BF16),
        grid=(m // tm, nb),
        in_specs=[
            pl.BlockSpec((tm, D_MODEL), lambda i, j: (i, 0)),
            pl.BlockSpec((1, D_MODEL), lambda i, j: (0, 0)),
            wspec(0), wspec(1), wspec(2), wspec(3),
            pl.BlockSpec((CONV_WIDTH, tc), lambda i, j: (0, j)),
        ],
        out_specs=pl.BlockSpec((tm, tc), lambda i, j: (i, j)),
        scratch_shapes=[
            pltpu.VMEM((tm, D_MODEL), BF16),
            pltpu.VMEM((HALO + tm, tc), F32),
            pltpu.VMEM((nb, HALO, tc), F32),
        ],
        compiler_params=pltpu.CompilerParams(
            dimension_semantics=("arbitrary", "arbitrary"), vmem_limit_bytes=VMEM_LIMIT_BYTES),
        name="in_odd",
    )(x2, g, w_bf, w_bf, w_bf, w_bf, conv_w)


def _even_gain_row(q_g, k_g):
    ones = jnp.ones((OFF_AQ,), F32)
    return jnp.concatenate([
        ones,
        jnp.tile(q_g * ATT_HD ** -0.5, ATT_HEADS),
        jnp.tile(k_g, ATT_HEADS),
        jnp.ones((IN_WIDTH_EVEN - OFF_AV,), F32),
    ]).reshape(1, IN_WIDTH_EVEN)


def kernel(x, p, pre_norm_g, w_in_even, q_norm_g, k_norm_g, ret_gn_g, w_out_even, w_in_odd, conv_w_odd,
           w_out_odd, ple_norm_g, w_ple_gate, w_ple_proj):
    batch, seq, d = x.shape
    depth = p.shape[0]
    assert d == D_MODEL and seq % (ATT_BLOCK * DILATED_PATTERNS[-1][1]) == 0 and seq % IN_TM == 0
    m = batch * seq
    slopes = jnp.asarray([(2.0 ** (-8.0 / ATT_HEADS)) ** (i + 1) for i in range(ATT_HEADS)], F32)
    x2 = x.reshape(m, d)
    p_all = p.reshape(depth * m, PLE_DIM)
    late_f32 = dict(w_out_even=w_out_even, w_ple_gate=w_ple_gate, w_ple_proj=w_ple_proj, w_in_odd=w_in_odd,
                    w_out_odd=w_out_odd)
    if w_in_even.shape[0] > 1:
        late_f32["w_in_even_rest"] = w_in_even[1:]
    bf = None
    for i in range(depth):
        j = i // 2
        pg = ple_norm_g[i].reshape(1, d)
        g = pre_norm_g[i].reshape(1, d)
        if i % 2 == 0:
            w_in_bf = w_in_even[0].astype(BF16) if j == 0 else bf["w_in_even_rest"][j - 1]
            proj = _in_even(x2, g, w_in_bf, _even_gain_row(q_norm_g[j], k_norm_g[j]))
            y_ret = _retention(proj, ret_gn_g[j], batch, seq)
            to_cast = [a.reshape(-1, a.shape[-1]) for a in late_f32.values()] if bf is None else []
            y_att, cast = _attention(proj, slopes, batch, seq, to_cast)
            if bf is None:
                bf = {name: c.reshape(a.shape) for (name, a), c in zip(late_f32.items(), cast)}
            x2 = _post(x2, y_ret, 0, y_att, 0, p_all, i, bf["w_out_even"][j], pg, bf["w_ple_gate"],
                       bf["w_ple_proj"], "post_even")
        else:
            y = _in_odd(x2, g, bf["w_in_odd"][j], conv_w_odd[j], seq)
            x2 = _post(x2, y, 0, y, 1, p_all, i, bf["w_out_odd"][j], pg, bf["w_ple_gate"], bf["w_ple_proj"],
                       "post_odd")
    return x2.reshape(batch, seq, d)
```

```python
import functools
import math

import jax
import jax.numpy as jnp
from jax import lax
from jax.experimental import pallas as pl
from jax.experimental.pallas import tpu as pltpu

F32 = jnp.float32
BF16 = jnp.bfloat16

D_MODEL = 2048
PLE_DIM = 256
EPS = 1e-6
NEG = -1e30
RET_DK = 128
RET_DV = 256
RET_HEADS = 4
RET_CHUNK = 256
ATT_HD = 128
ATT_HEADS = 8
ATT_BLOCK = 128
DILATED_PATTERNS = ((128, 1), (512, 4), (2048, 16))
RET_QK_WIDTH = RET_HEADS * RET_DK
RET_WIDTH = RET_HEADS * RET_DV
ATT_WIDTH = ATT_HEADS * ATT_HD
IN_WIDTH_EVEN = 2 * RET_QK_WIDTH + RET_WIDTH + 3 * ATT_WIDTH + RET_WIDTH + ATT_WIDTH
CONV_WIDTH = 3

OFF_RQ = 0
OFF_RK = RET_QK_WIDTH
OFF_RV = 2 * RET_QK_WIDTH
OFF_AQ = OFF_RV + RET_WIDTH
OFF_AK = OFF_AQ + ATT_WIDTH
OFF_AV = OFF_AK + ATT_WIDTH
OFF_Z = OFF_AV + ATT_WIDTH

VMEM_LIMIT_BYTES = 56 * 1024 * 1024
BF16_ROWS = 16

IN_TM = 1024
IN_EVEN_TN = 1024
IN_EVEN_NC = 256
IN_EVEN_LAST_ROW_SPLIT = 4
IN_ODD_TC = 256
RET_TQ = 1024
ATT_GROUP = 4
ATT_MERGE_ROWS = 512
POST_TM = 512
POST_NC = 256
HALO = 8


def _rms_rows(x, g):
    ms = jnp.mean(x * x, axis=-1, keepdims=True)
    return x * lax.rsqrt(ms + EPS) * g


def _silu(z):
    return z * jax.nn.sigmoid(z)


def _silu_cols(z, cols):
    del cols
    return _silu(z)


def _in_even_kernel(x_ref, g_ref, w_ref, gain_ref, o_ref, hn_ref):
    j = pl.program_id(1)

    @pl.when(j == 0)
    def _():
        hn_ref[...] = _rms_rows(x_ref[...], g_ref[...]).astype(BF16)

    tn = o_ref.shape[1]
    j_aq, j_av, j_z = OFF_AQ // tn, OFF_AV // tn, OFF_Z // tn

    def tile(epilogue):
        tm = o_ref.shape[0]
        n_chunks = tn // IN_EVEN_NC
        for cc in range(n_chunks):
            n_row = IN_EVEN_LAST_ROW_SPLIT if cc == n_chunks - 1 else 1
            for rr in range(n_row):
                rows = slice(rr * (tm // n_row), (rr + 1) * (tm // n_row))
                acc = jnp.dot(hn_ref[rows, :], w_ref[:, cc * IN_EVEN_NC:(cc + 1) * IN_EVEN_NC],
                              preferred_element_type=F32)
                for hh in range(IN_EVEN_NC // ATT_HD):
                    sl = slice(cc * IN_EVEN_NC + hh * ATT_HD, cc * IN_EVEN_NC + (hh + 1) * ATT_HD)
                    o_ref[rows, sl] = epilogue(acc[:, hh * ATT_HD:(hh + 1) * ATT_HD], sl).astype(BF16)

    @pl.when((j < j_aq) | (j == j_av))
    def _():
        tile(lambda a, sl: a)

    @pl.when((j >= j_aq) & (j < j_av))
    def _():
        tile(lambda a, sl: _rms_rows(a, gain_ref[:, sl]))

    @pl.when(j >= j_z)
    def _():
        tile(_silu_cols)


def _in_even(x2, g, w_bf, gain_row):
    m = x2.shape[0]
    tm, tn = IN_TM, IN_EVEN_TN
    return pl.pallas_call(
        _in_even_kernel,
        out_shape=jax.ShapeDtypeStruct((m, IN_WIDTH_EVEN), BF16),
        grid=(m // tm, IN_WIDTH_EVEN // tn),
        in_specs=[
            pl.BlockSpec((tm, D_MODEL), lambda i, j: (i, 0)),
            pl.BlockSpec((1, D_MODEL), lambda i, j: (0, 0)),
            pl.BlockSpec((D_MODEL, tn), lambda i, j: (0, j)),
            pl.BlockSpec((1, tn), lambda i, j: (0, j)),
        ],
        out_specs=pl.BlockSpec((tm, tn), lambda i, j: (i, j)),
        scratch_shapes=[pltpu.VMEM((tm, D_MODEL), BF16)],
        compiler_params=pltpu.CompilerParams(
            dimension_semantics=("arbitrary", "arbitrary"), vmem_limit_bytes=VMEM_LIMIT_BYTES),
        name="in_even",
    )(x2, g, w_bf, gain_row)


def _ret_kernel(q_ref, k_ref, v_ref, z_ref, gn_ref, o_ref, r_ref, decay_ref):
    c = RET_CHUNK
    tq = q_ref.shape[0]
    kscale = RET_DK ** -0.5
    log_g = [math.log(1.0 - 2.0 ** (-5.0 - h)) for h in range(RET_HEADS)]

    @pl.when(pl.program_id(1) == 0)
    def _():
        r_ref[...] = jnp.zeros(r_ref.shape, F32)

    @pl.when((pl.program_id(0) == 0) & (pl.program_id(1) == 0))
    def _():
        row = lax.broadcasted_iota(jnp.int32, (c, c), 0)
        col = lax.broadcasted_iota(jnp.int32, (c, c), 1)
        diff = (row - col).astype(F32)
        for h in range(RET_HEADS):
            decay_ref[h] = jnp.where(diff >= 0, jnp.exp(log_g[h] * jnp.maximum(diff, 0.0)), 0.0) * kscale

    pos = lax.broadcasted_iota(jnp.int32, (c, 1), 0).astype(F32)

    for h in range(RET_HEADS):
        lg = log_g[h]
        decay = decay_ref[h]
        xi = jnp.exp(lg * (pos + 1.0))
        zeta = jnp.exp(lg * (c - 1.0 - pos)) * kscale
        g_chunk = math.exp(lg * c)
        qk_sl = slice(h * RET_DK, (h + 1) * RET_DK)
        v_sl = slice(h * RET_DV, (h + 1) * RET_DV)
        gn = gn_ref[h:h + 1, :]
        chunks = [slice(ci * c, (ci + 1) * c) for ci in range(tq // c)]

        contrib = []
        for rows in chunks:
            kz_t = (k_ref[rows, qk_sl].astype(F32) * zeta).T.astype(BF16)
            contrib.append(jnp.dot(kz_t, v_ref[rows, v_sl], preferred_element_type=F32))

        state = r_ref[h]
        states = []
        for s_c in contrib:
            states.append(state)
            state = g_chunk * state + s_c
        r_ref[h] = state

        for rows, r_prev in zip(chunks, states):
            q = q_ref[rows, qk_sl]
            v = v_ref[rows, v_sl]
            s = lax.dot_general(q, k_ref[rows, qk_sl], (((1,), (1,)), ((), ())),
                                preferred_element_type=F32) * decay
            y = jnp.dot(s.astype(BF16), v, preferred_element_type=F32)
            y = y + jnp.dot(q, r_prev.astype(BF16), preferred_element_type=F32) * xi
            mu = jnp.mean(y, axis=-1, keepdims=True)
            yc = y - mu
            var = jnp.mean(yc * yc, axis=-1, keepdims=True)
            yn = yc * lax.rsqrt(var + EPS) * gn
            o_ref[rows, v_sl] = (yn * z_ref[rows, v_sl].astype(F32)).astype(BF16)


def _retention(proj, gn_g, batch, seq):
    m = proj.shape[0]
    tq = RET_TQ
    nt = seq // tq
    row = lambda b, t: b * nt + t
    return pl.pallas_call(
        _ret_kernel,
        out_shape=jax.ShapeDtypeStruct((m, RET_WIDTH), BF16),
        grid=(batch, nt),
        in_specs=[
            pl.BlockSpec((tq, RET_QK_WIDTH), lambda b, t: (row(b, t), OFF_RQ // RET_QK_WIDTH)),
            pl.BlockSpec((tq, RET_QK_WIDTH), lambda b, t: (row(b, t), OFF_RK // RET_QK_WIDTH)),
            pl.BlockSpec((tq, RET_WIDTH), lambda b, t: (row(b, t), OFF_RV // RET_WIDTH)),
            pl.BlockSpec((tq, RET_WIDTH), lambda b, t: (row(b, t), OFF_Z // RET_WIDTH)),
            pl.BlockSpec((RET_HEADS, RET_DV), lambda b, t: (0, 0)),
        ],
        out_specs=pl.BlockSpec((tq, RET_WIDTH), lambda b, t: (row(b, t), 0)),
        scratch_shapes=[pltpu.VMEM((RET_HEADS, RET_DK, RET_DV), F32),
                        pltpu.VMEM((RET_HEADS, RET_CHUNK, RET_CHUNK), F32)],
        compiler_params=pltpu.CompilerParams(
            dimension_semantics=("arbitrary", "arbitrary"), vmem_limit_bytes=VMEM_LIMIT_BYTES),
        name="retention",
    )(proj, proj, proj, proj, gn_g)


def _att_kernel(*refs, n_cast):
    slopes_ref, q_ref, k_ref, v_ref, z_ref = refs[:5]
    cast_in, o_ref, cast_out = refs[5:5 + n_cast], refs[5 + n_cast], refs[6 + n_cast:6 + 2 * n_cast]
    (stage_q, stage_k, stage_v, q4, k4, v4, qp4, qp16, kp1, kp4, kp16, vp1, vp4, vp16, bias_ref, s_ref, pn_ref,
     o0, o1, o2, l0, l1, l2) = refs[6 + 2 * n_cast:]
    for src, dst in zip(cast_in, cast_out):
        dst[...] = src[...].astype(BF16)
    blk = ATT_BLOCK
    grp = ATT_GROUP
    seq = q_ref.shape[0]
    nblk = seq // blk
    n_it = nblk // grp
    d4, d16 = DILATED_PATTERNS[1][1], DILATED_PATTERNS[2][1]
    assert DILATED_PATTERNS[0][1] == 1 and grp == d4 and d16 == d4 * d4
    assert n_it % 2 == 0 and n_it >= 4 and nblk % d16 == 0
    slope = slopes_ref[pl.program_id(1)]
    nt = (((1,), (1,)), ((), ()))

    def natural_rows(bi, dil):
        nb = nblk // dil
        start = (bi // nb) + (bi % nb) * (blk * dil)
        if dil == 1:
            return pl.ds(pl.multiple_of(start, blk), blk)
        return pl.ds(start, blk, stride=dil)

    def block_rows(bi, lead=0):
        return pl.ds(pl.multiple_of(bi * blk + lead, blk), blk)

    for pad in (kp1, kp4, kp16, vp1, vp4, vp16):
        pad[0:blk, :] = jnp.zeros((blk, ATT_HD), BF16)
    kp1[blk:, :] = k_ref[...]
    vp1[blk:, :] = v_ref[...]

    def copy_4(it):
        rows = pl.ds(pl.multiple_of(it * (grp * blk), grp * blk), grp * blk)
        stage_q[...] = q_ref[rows, :].astype(F32)
        stage_k[...] = k_ref[rows, :].astype(F32)
        stage_v[...] = v_ref[rows, :].astype(F32)
        for r in range(d4):
            bi = r * (nblk // d4) + it
            src = pl.ds(r, blk, stride=d4)
            q, k, v = stage_q[src, :], stage_k[src, :], stage_v[src, :]
            q4[block_rows(bi), :], k4[block_rows(bi), :], v4[block_rows(bi), :] = q, k, v
            qp4[block_rows(bi), :] = q.astype(BF16)
            kp4[block_rows(bi, blk), :] = k.astype(BF16)
            vp4[block_rows(bi, blk), :] = v.astype(BF16)

    def copy_16(it):
        halves = (nblk // d4) // grp
        r4, n16 = it // halves, it % halves
        for c in range(d4):
            bi = (r4 + d4 * c) * (nblk // d16) + n16
            src = pl.ds(r4 * (seq // d4) + n16 * (grp * blk) + c, blk, stride=d4)
            qp16[block_rows(bi), :] = q4[src, :].astype(BF16)
            kp16[block_rows(bi, blk), :] = k4[src, :].astype(BF16)
            vp16[block_rows(bi, blk), :] = v4[src, :].astype(BF16)

    qi = lax.broadcasted_iota(jnp.int32, (blk, 2 * blk), 0)
    kj = lax.broadcasted_iota(jnp.int32, (blk, 2 * blk), 1)
    rel = qi - kj + blk
    for pi, (window, dil) in enumerate(DILATED_PATTERNS):
        valid = (rel >= 0) & (rel <= window // dil)
        alibi = -slope * (rel * dil).astype(F32)
        bias_ref[pi, 0] = jnp.where(valid & (kj >= blk), alibi, NEG)
        bias_ref[pi, 1] = jnp.where(valid, alibi, NEG)

    o16p, l16p = q4, k4

    def rows_in_d4_order(bi16):
        r16, n16 = bi16 // (nblk // d16), bi16 % (nblk // d16)
        return pl.ds((r16 % d4) * (seq // d4) + n16 * (grp * blk) + r16 // d4, blk, stride=d4)

    def un_permute(it):
        for g in range(grp):
            bi4 = it * grp + g
            o2[natural_rows(bi4, d4), :] = o16p[block_rows(bi4), :]
            l2[natural_rows(bi4, d4), :] = l16p[block_rows(bi4), :]

    def make_stages(pi, dil, q_src, kp, vp, o_g, l_g, copy_next):
        nb = nblk // dil
        two_step = dil == d16
        block_of = (lambda it, g: it + n_it * g) if two_step else (lambda it, g: it * grp + g)
        out_rows = rows_in_d4_order if two_step else (lambda bi: natural_rows(bi, dil))
        o_dst, l_dst = (o16p, l16p) if two_step else (o_g, l_g)

        def scores(it):
            for g in range(grp):
                bi = block_of(it, g)
                kb = kp[pl.ds(pl.multiple_of(bi * blk, blk), 2 * blk), :]
                s_ref[it % 2, g] = lax.dot_general(q_src[block_rows(bi), :], kb, nt, preferred_element_type=F32)
            if copy_next is not None:
                copy_next(it)

        def probs(it):
            for g in range(grp):
                bi = block_of(it, g)
                s = s_ref[it % 2, g] + bias_ref[pi, jnp.minimum(bi % nb, 1)]
                mx = jnp.max(jnp.maximum(s[:, :blk], s[:, blk:]), axis=-1, keepdims=True)
                pr = jnp.exp(s - mx)
                den = jnp.sum(pr[:, :blk] + pr[:, blk:], axis=-1, keepdims=True)
                pn_ref[it % 2, g] = (pr * (1.0 / den)).astype(BF16)
                l_dst[out_rows(bi), :] = jnp.broadcast_to(mx + jnp.log(den), (blk, ATT_HD))

        def values(it):
            for g in range(grp):
                bi = block_of(it, g)
                vb = vp[pl.ds(pl.multiple_of(bi * blk, blk), 2 * blk), :]
                o_dst[out_rows(bi), :] = jnp.dot(pn_ref[it % 2, g], vb, preferred_element_type=F32)
            if two_step:
                un_permute(it)

        return scores, probs, values

    stages = [make_stages(pi, dil, *refs) for pi, ((_, dil), *refs) in enumerate(zip(
        DILATED_PATTERNS, (q_ref, qp4, qp16), (kp1, kp4, kp16), (vp1, vp4, vp16), (o0, o1, o2), (l0, l1, l2),
        (copy_4, copy_16, None)))]

    def fill(scores, probs, values):
        scores(0)
        probs(0)
        scores(1)

    def drain(scores, probs, values):
        values(n_it - 2)
        probs(n_it - 1)
        values(n_it - 1)

    fill(*stages[0])
    for pi, (scores, probs, values) in enumerate(stages):
        def body(it, carry, scores=scores, probs=probs, values=values):
            values(it - 2)
            probs(it - 1)
            scores(it)
            return carry

        lax.fori_loop(2, n_it, body, 0)
        drain(scores, probs, values)
        if pi + 1 < len(stages):
            fill(*stages[pi + 1])

    mrows = ATT_MERGE_ROWS

    def merge(ci, carry):
        rows = pl.ds(pl.multiple_of(ci * mrows, mrows), mrows)
        a0, a1, a2 = l0[rows, :], l1[rows, :], l2[rows, :]
        mx = jnp.maximum(jnp.maximum(a0, a1), a2)
        e0, e1, e2 = jnp.exp(a0 - mx), jnp.exp(a1 - mx), jnp.exp(a2 - mx)
        y = (e0 * o0[rows, :] + e1 * o1[rows, :] + e2 * o2[rows, :]) / (e0 + e1 + e2)
        o_ref[rows, :] = (y * z_ref[rows, :].astype(F32)).astype(BF16)
        return carry

    lax.fori_loop(0, seq // mrows, merge, 0)


def _attention(proj, slopes, batch, seq, to_cast):
    m = proj.shape[0]
    hd = ATT_HD
    n_steps = batch * ATT_HEADS
    spec = lambda off: pl.BlockSpec((seq, hd), lambda b, h: (b, off // hd + h))
    slab = lambda a: pl.BlockSpec((a.shape[0] // n_steps, a.shape[1]), lambda b, h: (b * ATT_HEADS + h, 0))
    assert all(a.shape[0] % (BF16_ROWS * n_steps) == 0 for a in to_cast)
    f32_rows = pltpu.VMEM((seq, hd), F32)
    bf16_rows = pltpu.VMEM((seq, hd), BF16)
    bf16_padded = pltpu.VMEM((seq + ATT_BLOCK, hd), BF16)
    f32_group = pltpu.VMEM((ATT_GROUP * ATT_BLOCK, hd), F32)
    y_att, *cast = pl.pallas_call(
        functools.partial(_att_kernel, n_cast=len(to_cast)),
        out_shape=[jax.ShapeDtypeStruct((m, ATT_WIDTH), BF16)]
        + [jax.ShapeDtypeStruct(a.shape, BF16) for a in to_cast],
        grid=(batch, ATT_HEADS),
        in_specs=[
            pl.BlockSpec(memory_space=pltpu.SMEM),
            spec(OFF_AQ), spec(OFF_AK), spec(OFF_AV), spec(OFF_Z + RET_WIDTH),
        ] + [slab(a) for a in to_cast],
        out_specs=[pl.BlockSpec((seq, hd), lambda b, h: (b, h))] + [slab(a) for a in to_cast],
        scratch_shapes=[
            f32_group, f32_group, f32_group, f32_rows, f32_rows, f32_rows,
            bf16_rows, bf16_rows,
            bf16_padded, bf16_padded, bf16_padded, bf16_padded, bf16_padded, bf16_padded,
            pltpu.VMEM((len(DILATED_PATTERNS), 2, ATT_BLOCK, 2 * ATT_BLOCK), F32),
            pltpu.VMEM((2, ATT_GROUP, ATT_BLOCK, 2 * ATT_BLOCK), F32),
            pltpu.VMEM((2, ATT_GROUP, ATT_BLOCK, 2 * ATT_BLOCK), BF16),
            f32_rows, f32_rows, f32_rows, f32_rows, f32_rows, f32_rows,
        ],
        compiler_params=pltpu.CompilerParams(
            dimension_semantics=("arbitrary", "arbitrary"), vmem_limit_bytes=VMEM_LIMIT_BYTES),
        name="dilated_attention",
    )(slopes, proj, proj, proj, proj, *to_cast)
    return y_att, cast


def _post_tail(x1, p_ref, pg_ref, wg_ref, wp_ref, o_ref):
    h2 = _rms_rows(x1, pg_ref[...]).astype(BF16)
    pb = p_ref[...].astype(BF16)
    for cc in range(D_MODEL // POST_NC):
        sl = slice(cc * POST_NC, (cc + 1) * POST_NC)
        gate = jax.nn.sigmoid(jnp.dot(h2, wg_ref[:, sl], preferred_element_type=F32))
        pp = jnp.dot(pb, wp_ref[:, sl], preferred_element_type=F32)
        o_ref[:, sl] = x1[:, sl] + gate * pp


def _post_kernel(x_ref, ya_ref, yb_ref, p_ref, wa_ref, wb_ref, pg_ref, wg_ref, wp_ref, o_ref):
    x1 = x_ref[...] + jnp.dot(ya_ref[...], wa_ref[...], preferred_element_type=F32)
    x1 = x1 + jnp.dot(yb_ref[...], wb_ref[...], preferred_element_type=F32)
    _post_tail(x1, p_ref, pg_ref, wg_ref, wp_ref, o_ref)


def _const_spec(shape):
    return pl.BlockSpec(shape, lambda i: (0,) * len(shape), pipeline_mode=pl.Buffered(1))


def _ple_rows_spec(layer, m, tm):
    return pl.BlockSpec((tm, PLE_DIM), lambda i: (layer * (m // tm) + i, 0))


def _post(x2, ya, ya_blk, yb, yb_blk, p_all, layer, w_out_bf, pg, wg_all_bf, wp_all_bf, name):
    m = x2.shape[0]
    tm = POST_TM
    half = D_MODEL // 2
    rows = lambda w, blk=0: pl.BlockSpec((tm, w), lambda i: (i, blk))
    layer_spec = lambda k: pl.BlockSpec((None, k, D_MODEL), lambda i: (layer, 0, 0), pipeline_mode=pl.Buffered(1))
    return pl.pallas_call(
        _post_kernel,
        out_shape=jax.ShapeDtypeStruct((m, D_MODEL), F32),
        grid=(m // tm,),
        in_specs=[
            rows(D_MODEL), rows(half, ya_blk), rows(half, yb_blk), _ple_rows_spec(layer, m, tm),
            pl.BlockSpec((half, D_MODEL), lambda i: (0, 0), pipeline_mode=pl.Buffered(1)),
            pl.BlockSpec((half, D_MODEL), lambda i: (1, 0), pipeline_mode=pl.Buffered(1)),
            _const_spec((1, D_MODEL)), layer_spec(D_MODEL), layer_spec(PLE_DIM),
        ],
        out_specs=rows(D_MODEL),
        compiler_params=pltpu.CompilerParams(
            dimension_semantics=("arbitrary",), vmem_limit_bytes=VMEM_LIMIT_BYTES),
        name=name,
    )(x2, ya, yb, p_all, w_out_bf, w_out_bf, pg, wg_all_bf, wp_all_bf)


def _in_odd_kernel(x_ref, g_ref, wb_ref, wc_ref, wu_ref, wz_ref, cw_ref, y_ref, hn_ref, ext_ref, halo_ref, *,
                   tiles_per_seq):
    i, j = pl.program_id(0), pl.program_id(1)
    tm = x_ref.shape[0]

    @pl.when(j == 0)
    def _():
        hn_ref[...] = _rms_rows(x_ref[...], g_ref[...]).astype(BF16)

    @pl.when((i == 0) & (j == 0))
    def _():
        halo_ref[...] = jnp.zeros(halo_ref.shape, F32)

    hn = hn_ref[...]
    cu = (jnp.dot(hn, wc_ref[...], preferred_element_type=F32)
          * jnp.dot(hn, wu_ref[...], preferred_element_type=F32))
    first = (i % tiles_per_seq) == 0
    ext_ref[0:HALO, :] = jnp.where(first, 0.0, halo_ref[j])
    ext_ref[HALO:HALO + tm, :] = cu
    halo_ref[j] = cu[tm - HALO:, :]
    conv = (cw_ref[0:1, :] * ext_ref[HALO - 2:HALO - 2 + tm, :]
            + cw_ref[1:2, :] * ext_ref[HALO - 1:HALO - 1 + tm, :]
            + cw_ref[2:3, :] * cu)
    bg = jnp.dot(hn, wb_ref[...], preferred_element_type=F32)
    z = jnp.dot(hn, wz_ref[...], preferred_element_type=F32)
    y_ref[...] = (bg * conv * _silu(z)).astype(BF16)


def _in_odd(x2, g, w_bf, conv_w, seq):
    m = x2.shape[0]
    tm, tc = IN_TM, IN_ODD_TC
    nb = D_MODEL // tc
    wspec = lambda seg: pl.BlockSpec((D_MODEL, tc), lambda i, j: (0, seg * nb + j))
    return pl.pallas_call(
        functools.partial(_in_odd_kernel, tiles_per_seq=seq // tm),
        out_shape=jax.ShapeDtypeStruct((m, D_MODEL), BF16),
        grid=(m // tm, nb),
        in_specs=[
            pl.BlockSpec((tm, D_MODEL), lambda i, j: (i, 0)),
            pl.BlockSpec((1, D_MODEL), lambda i, j: (0, 0)),
            wspec(0), wspec(1), wspec(2), wspec(3),
            pl.BlockSpec((CONV_WIDTH, tc), lambda i, j: (0, j)),
        ],
        out_specs=pl.BlockSpec((tm, tc), lambda i, j: (i, j)),
        scratch_shapes=[
            pltpu.VMEM((tm, D_MODEL), BF16),
            pltpu.VMEM((HALO + tm, tc), F32),
            pltpu.VMEM((nb, HALO, tc), F32),
        ],
        compiler_params=pltpu.CompilerParams(
            dimension_semantics=("arbitrary", "arbitrary"), vmem_limit_bytes=VMEM_LIMIT_BYTES),
        name="in_odd",
    )(x2, g, w_bf, w_bf, w_bf, w_bf, conv_w)


def _even_gain_row(q_g, k_g):
    ones = jnp.ones((OFF_AQ,), F32)
    return jnp.concatenate([
        ones,
        jnp.tile(q_g * ATT_HD ** -0.5, ATT_HEADS),
        jnp.tile(k_g, ATT_HEADS),
        jnp.ones((IN_WIDTH_EVEN - OFF_AV,), F32),
    ]).reshape(1, IN_WIDTH_EVEN)


def kernel(x, p, pre_norm_g, w_in_even, q_norm_g, k_norm_g, ret_gn_g, w_out_even, w_in_odd, conv_w_odd,
           w_out_odd, ple_norm_g, w_ple_gate, w_ple_proj):
    batch, seq, d = x.shape
    depth = p.shape[0]
    assert d == D_MODEL and seq % (ATT_BLOCK * DILATED_PATTERNS[-1][1]) == 0 and seq % IN_TM == 0
    m = batch * seq
    slopes = jnp.asarray([(2.0 ** (-8.0 / ATT_HEADS)) ** (i + 1) for i in range(ATT_HEADS)], F32)
    x2 = x.reshape(m, d)
    p_all = p.reshape(depth * m, PLE_DIM)
    late_f32 = dict(w_out_even=w_out_even, w_ple_gate=w_ple_gate, w_ple_proj=w_ple_proj, w_in_odd=w_in_odd,
                    w_out_odd=w_out_odd)
    if w_in_even.shape[0] > 1:
        late_f32["w_in_even_rest"] = w_in_even[1:]
    bf = None
    for i in range(depth):
        j = i // 2
        pg = ple_norm_g[i].reshape(1, d)
        g = pre_norm_g[i].reshape(1, d)
        if i % 2 == 0:
            w_in_bf = w_in_even[0].astype(BF16) if j == 0 else bf["w_in_even_rest"][j - 1]
            proj = _in_even(x2, g, w_in_bf, _even_gain_row(q_norm_g[j], k_norm_g[j]))
            y_ret = _retention(proj, ret_gn_g[j], batch, seq)
            to_cast = [a.reshape(-1, a.shape[-1]) for a in late_f32.values()] if bf is None else []
            y_att, cast = _attention(proj, slopes, batch, seq, to_cast)
            if bf is None:
                bf = {name: c.reshape(a.shape) for (name, a), c in zip(late_f32.items(), cast)}
            x2 = _post(x2, y_ret, 0, y_att, 0, p_all, i, bf["w_out_even"][j], pg, bf["w_ple_gate"],
                       bf["w_ple_proj"], "post_even")
        else:
            y = _in_odd(x2, g, bf["w_in_odd"][j], conv_w_odd[j], seq)
            x2 = _post(x2, y, 0, y, 1, p_all, i, bf["w_out_odd"][j], pg, bf["w_ple_gate"], bf["w_ple_proj"],
                       "post_odd")
    return x2.reshape(batch, seq, d)
```

```python
import functools
import math

import jax
import jax.numpy as jnp
from jax import lax
from jax.experimental import pallas as pl
from jax.experimental.pallas import tpu as pltpu

F32 = jnp.float32
BF16 = jnp.bfloat16

D_MODEL = 2048
PLE_DIM = 256
EPS = 1e-6
NEG = -1e30
RET_DK = 128
RET_DV = 256
RET_HEADS = 4
RET_CHUNK = 256
ATT_HD = 128
ATT_HEADS = 8
ATT_BLOCK = 128
DILATED_PATTERNS = ((128, 1), (512, 4), (2048, 16))
RET_QK_WIDTH = RET_HEADS * RET_DK
RET_WIDTH = RET_HEADS * RET_DV
ATT_WIDTH = ATT_HEADS * ATT_HD
IN_WIDTH_EVEN = 2 * RET_QK_WIDTH + RET_WIDTH + 3 * ATT_WIDTH + RET_WIDTH + ATT_WIDTH
CONV_WIDTH = 3

OFF_RQ = 0
OFF_RK = RET_QK_WIDTH
OFF_RV = 2 * RET_QK_WIDTH
OFF_AQ = OFF_RV + RET_WIDTH
OFF_AK = OFF_AQ + ATT_WIDTH
OFF_AV = OFF_AK + ATT_WIDTH
OFF_Z = OFF_AV + ATT_WIDTH

VMEM_LIMIT_BYTES = 56 * 1024 * 1024
BF16_ROWS = 16

IN_TM = 1024
IN_EVEN_TN = 1024
IN_EVEN_NC = 256
IN_EVEN_LAST_ROW_SPLIT = 4
IN_ODD_TC = 256
RET_TQ = 1024
ATT_GROUP = 4
ATT_MERGE_ROWS = 512
POST_TM = 512
POST_NC = 256
HALO = 8


def _rms_rows(x, g):
    ms = jnp.mean(x * x, axis=-1, keepdims=True)
    return x * lax.rsqrt(ms + EPS) * g


def _silu(z):
    return z * jax.nn.sigmoid(z)


def _silu_cols(z, cols):
    del cols
    return _silu(z)


def _x_half_specs(tm, n_row_tiles, n_col_steps):
    half = D_MODEL // 2

    def spec(col, ahead):
        def index(i, j):
            return (jnp.minimum(i + (j >= n_col_steps - ahead).astype(jnp.int32), n_row_tiles - 1), col)
        return pl.BlockSpec((tm, half), index)

    return spec(0, 2), spec(1, 1)


def _pre_norm(xa_ref, xb_ref, g_ref, hn_ref):
    half = xa_ref.shape[1]
    xa, xb = xa_ref[...], xb_ref[...]
    ms = (jnp.sum(xa * xa, axis=-1, keepdims=True) + jnp.sum(xb * xb, axis=-1, keepdims=True)) / (2 * half)
    rs = lax.rsqrt(ms + EPS)
    hn_ref[:, :half] = (xa * rs * g_ref[:, :half]).astype(BF16)
    hn_ref[:, half:] = (xb * rs * g_ref[:, half:]).astype(BF16)


def _in_even_kernel(xa_ref, xb_ref, g_ref, w_ref, gain_ref, o_ref, hn_ref):
    j = pl.program_id(1)

    @pl.when(j == 0)
    def _():
        _pre_norm(xa_ref, xb_ref, g_ref, hn_ref)

    tn = o_ref.shape[1]
    j_aq, j_av, j_z = OFF_AQ // tn, OFF_AV // tn, OFF_Z // tn

    def tile(epilogue):
        tm = o_ref.shape[0]
        n_chunks = tn // IN_EVEN_NC
        for cc in range(n_chunks):
            n_row = IN_EVEN_LAST_ROW_SPLIT if cc == n_chunks - 1 else 1
            for rr in range(n_row):
                rows = slice(rr * (tm // n_row), (rr + 1) * (tm // n_row))
                acc = jnp.dot(hn_ref[rows, :], w_ref[:, cc * IN_EVEN_NC:(cc + 1) * IN_EVEN_NC],
                              preferred_element_type=F32)
                for hh in range(IN_EVEN_NC // ATT_HD):
                    sl = slice(cc * IN_EVEN_NC + hh * ATT_HD, cc * IN_EVEN_NC + (hh + 1) * ATT_HD)
                    o_ref[rows, sl] = epilogue(acc[:, hh * ATT_HD:(hh + 1) * ATT_HD], sl).astype(BF16)

    @pl.when((j < j_aq) | (j == j_av))
    def _():
        tile(lambda a, sl: a)

    @pl.when((j >= j_aq) & (j < j_av))
    def _():
        tile(lambda a, sl: _rms_rows(a, gain_ref[:, sl]))

    @pl.when(j >= j_z)
    def _():
        tile(_silu_cols)


def _in_even(x2, g, w_bf, gain_row):
    m = x2.shape[0]
    tm, tn = IN_TM, IN_EVEN_TN
    return pl.pallas_call(
        _in_even_kernel,
        out_shape=jax.ShapeDtypeStruct((m, IN_WIDTH_EVEN), BF16),
        grid=(m // tm, IN_WIDTH_EVEN // tn),
        in_specs=[
            *_x_half_specs(tm, m // tm, IN_WIDTH_EVEN // tn),
            pl.BlockSpec((1, D_MODEL), lambda i, j: (0, 0)),
            pl.BlockSpec((D_MODEL, tn), lambda i, j: (0, j)),
            pl.BlockSpec((1, tn), lambda i, j: (0, j)),
        ],
        out_specs=pl.BlockSpec((tm, tn), lambda i, j: (i, j)),
        scratch_shapes=[pltpu.VMEM((tm, D_MODEL), BF16)],
        compiler_params=pltpu.CompilerParams(
            dimension_semantics=("arbitrary", "arbitrary"), vmem_limit_bytes=VMEM_LIMIT_BYTES),
        name="in_even",
    )(x2, x2, g, w_bf, gain_row)


def _ret_kernel(q_ref, k_ref, v_ref, z_ref, gn_ref, o_ref, r_ref, decay_ref):
    c = RET_CHUNK
    tq = q_ref.shape[0]
    kscale = RET_DK ** -0.5
    log_g = [math.log(1.0 - 2.0 ** (-5.0 - h)) for h in range(RET_HEADS)]

    @pl.when(pl.program_id(1) == 0)
    def _():
        r_ref[...] = jnp.zeros(r_ref.shape, F32)

    @pl.when((pl.program_id(0) == 0) & (pl.program_id(1) == 0))
    def _():
        row = lax.broadcasted_iota(jnp.int32, (c, c), 0)
        col = lax.broadcasted_iota(jnp.int32, (c, c), 1)
        diff = (row - col).astype(F32)
        for h in range(RET_HEADS):
            decay_ref[h] = jnp.where(diff >= 0, jnp.exp(log_g[h] * jnp.maximum(diff, 0.0)), 0.0) * kscale

    pos = lax.broadcasted_iota(jnp.int32, (c, 1), 0).astype(F32)

    for h in range(RET_HEADS):
        lg = log_g[h]
        decay = decay_ref[h]
        xi = jnp.exp(lg * (pos + 1.0))
        zeta = jnp.exp(lg * (c - 1.0 - pos)) * kscale
        g_chunk = math.exp(lg * c)
        qk_sl = slice(h * RET_DK, (h + 1) * RET_DK)
        v_sl = slice(h * RET_DV, (h + 1) * RET_DV)
        gn = gn_ref[h:h + 1, :]
        chunks = [slice(ci * c, (ci + 1) * c) for ci in range(tq // c)]

        contrib = []
        for rows in chunks:
            kz_t = (k_ref[rows, qk_sl].astype(F32) * zeta).T.astype(BF16)
            contrib.append(jnp.dot(kz_t, v_ref[rows, v_sl], preferred_element_type=F32))

        state = r_ref[h]
        states = []
        for s_c in contrib:
            states.append(state)
            state = g_chunk * state + s_c
        r_ref[h] = state

        for rows, r_prev in zip(chunks, states):
            q = q_ref[rows, qk_sl]
            v = v_ref[rows, v_sl]
            s = lax.dot_general(q, k_ref[rows, qk_sl], (((1,), (1,)), ((), ())),
                                preferred_element_type=F32) * decay
            y = jnp.dot(s.astype(BF16), v, preferred_element_type=F32)
            y = y + jnp.dot(q, r_prev.astype(BF16), preferred_element_type=F32) * xi
            mu = jnp.mean(y, axis=-1, keepdims=True)
            yc = y - mu
            var = jnp.mean(yc * yc, axis=-1, keepdims=True)
            yn = yc * lax.rsqrt(var + EPS) * gn
            o_ref[rows, v_sl] = (yn * z_ref[rows, v_sl].astype(F32)).astype(BF16)


def _retention(proj, gn_g, batch, seq):
    m = proj.shape[0]
    tq = RET_TQ
    nt = seq // tq
    row = lambda b, t: b * nt + t
    return pl.pallas_call(
        _ret_kernel,
        out_shape=jax.ShapeDtypeStruct((m, RET_WIDTH), BF16),
        grid=(batch, nt),
        in_specs=[
            pl.BlockSpec((tq, RET_QK_WIDTH), lambda b, t: (row(b, t), OFF_RQ // RET_QK_WIDTH)),
            pl.BlockSpec((tq, RET_QK_WIDTH), lambda b, t: (row(b, t), OFF_RK // RET_QK_WIDTH)),
            pl.BlockSpec((tq, RET_WIDTH), lambda b, t: (row(b, t), OFF_RV // RET_WIDTH)),
            pl.BlockSpec((tq, RET_WIDTH), lambda b, t: (row(b, t), OFF_Z // RET_WIDTH)),
            pl.BlockSpec((RET_HEADS, RET_DV), lambda b, t: (0, 0)),
        ],
        out_specs=pl.BlockSpec((tq, RET_WIDTH), lambda b, t: (row(b, t), 0)),
        scratch_shapes=[pltpu.VMEM((RET_HEADS, RET_DK, RET_DV), F32),
                        pltpu.VMEM((RET_HEADS, RET_CHUNK, RET_CHUNK), F32)],
        compiler_params=pltpu.CompilerParams(
            dimension_semantics=("arbitrary", "arbitrary"), vmem_limit_bytes=VMEM_LIMIT_BYTES),
        name="retention",
    )(proj, proj, proj, proj, gn_g)


def _att_kernel(*refs, n_cast):
    slopes_ref, q_ref, k_ref, v_ref, z_ref = refs[:5]
    cast_in, o_ref, cast_out = refs[5:5 + n_cast], refs[5 + n_cast], refs[6 + n_cast:6 + 2 * n_cast]
    (stage_q, stage_k, stage_v, q4, k4, v4, qp4, qp16, kp1, kp4, kp16, vp1, vp4, vp16, bias_ref, s_ref, pn_ref,
     o0, o1, o2, l0, l1, l2) = refs[6 + 2 * n_cast:]
    for src, dst in zip(cast_in, cast_out):
        dst[...] = src[...].astype(BF16)
    blk = ATT_BLOCK
    grp = ATT_GROUP
    seq = q_ref.shape[0]
    nblk = seq // blk
    n_it = nblk // grp
    d4, d16 = DILATED_PATTERNS[1][1], DILATED_PATTERNS[2][1]
    assert DILATED_PATTERNS[0][1] == 1 and grp == d4 and d16 == d4 * d4
    assert n_it % 2 == 0 and n_it >= 4 and nblk % d16 == 0
    slope = slopes_ref[pl.program_id(1)]
    nt = (((1,), (1,)), ((), ()))

    def natural_rows(bi, dil):
        nb = nblk // dil
        start = (bi // nb) + (bi % nb) * (blk * dil)
        if dil == 1:
            return pl.ds(pl.multiple_of(start, blk), blk)
        return pl.ds(start, blk, stride=dil)

    def block_rows(bi, lead=0):
        return pl.ds(pl.multiple_of(bi * blk + lead, blk), blk)

    for pad in (kp1, kp4, kp16, vp1, vp4, vp16):
        pad[0:blk, :] = jnp.zeros((blk, ATT_HD), BF16)
    kp1[blk:, :] = k_ref[...]
    vp1[blk:, :] = v_ref[...]

    def copy_4(it):
        rows = pl.ds(pl.multiple_of(it * (grp * blk), grp * blk), grp * blk)
        stage_q[...] = q_ref[rows, :].astype(F32)
        stage_k[...] = k_ref[rows, :].astype(F32)
        stage_v[...] = v_ref[rows, :].astype(F32)
        for r in range(d4):
            bi = r * (nblk // d4) + it
            src = pl.ds(r, blk, stride=d4)
            q, k, v = stage_q[src, :], stage_k[src, :], stage_v[src, :]
            q4[block_rows(bi), :], k4[block_rows(bi), :], v4[block_rows(bi), :] = q, k, v
            qp4[block_rows(bi), :] = q.astype(BF16)
            kp4[block_rows(bi, blk), :] = k.astype(BF16)
            vp4[block_rows(bi, blk), :] = v.astype(BF16)

    def copy_16(it):
        halves = (nblk // d4) // grp
        r4, n16 = it // halves, it % halves
        for c in range(d4):
            bi = (r4 + d4 * c) * (nblk // d16) + n16
            src = pl.ds(r4 * (seq // d4) + n16 * (grp * blk) + c, blk, stride=d4)
            qp16[block_rows(bi), :] = q4[src, :].astype(BF16)
            kp16[block_rows(bi, blk), :] = k4[src, :].astype(BF16)
            vp16[block_rows(bi, blk), :] = v4[src, :].astype(BF16)

    qi = lax.broadcasted_iota(jnp.int32, (blk, 2 * blk), 0)
    kj = lax.broadcasted_iota(jnp.int32, (blk, 2 * blk), 1)
    rel = qi - kj + blk
    for pi, (window, dil) in enumerate(DILATED_PATTERNS):
        valid = (rel >= 0) & (rel <= window // dil)
        alibi = -slope * (rel * dil).astype(F32)
        bias_ref[pi, 0] = jnp.where(valid & (kj >= blk), alibi, NEG)
        bias_ref[pi, 1] = jnp.where(valid, alibi, NEG)

    o16p, l16p = q4, k4

    def rows_in_d4_order(bi16):
        r16, n16 = bi16 // (nblk // d16), bi16 % (nblk // d16)
        return pl.ds((r16 % d4) * (seq // d4) + n16 * (grp * blk) + r16 // d4, blk, stride=d4)

    def un_permute(it):
        for g in range(grp):
            bi4 = it * grp + g
            o2[natural_rows(bi4, d4), :] = o16p[block_rows(bi4), :]
            l2[natural_rows(bi4, d4), :] = l16p[block_rows(bi4), :]

    def make_stages(pi, dil, q_src, kp, vp, o_g, l_g, copy_next):
        nb = nblk // dil
        two_step = dil == d16
        block_of = (lambda it, g: it + n_it * g) if two_step else (lambda it, g: it * grp + g)
        out_rows = rows_in_d4_order if two_step else (lambda bi: natural_rows(bi, dil))
        o_dst, l_dst = (o16p, l16p) if two_step else (o_g, l_g)

        def scores(it):
            for g in range(grp):
                bi = block_of(it, g)
                kb = kp[pl.ds(pl.multiple_of(bi * blk, blk), 2 * blk), :]
                s_ref[it % 2, g] = lax.dot_general(q_src[block_rows(bi), :], kb, nt, preferred_element_type=F32)
            if copy_next is not None:
                copy_next(it)

        def probs(it):
            for g in range(grp):
                bi = block_of(it, g)
                s = s_ref[it % 2, g] + bias_ref[pi, jnp.minimum(bi % nb, 1)]
                mx = jnp.max(jnp.maximum(s[:, :blk], s[:, blk:]), axis=-1, keepdims=True)
                pr = jnp.exp(s - mx)
                den = jnp.sum(pr[:, :blk] + pr[:, blk:], axis=-1, keepdims=True)
                pn_ref[it % 2, g] = (pr * (1.0 / den)).astype(BF16)
                l_dst[out_rows(bi), :] = jnp.broadcast_to(mx + jnp.log(den), (blk, ATT_HD))

        def values(it):
            for g in range(grp):
                bi = block_of(it, g)
                vb = vp[pl.ds(pl.multiple_of(bi * blk, blk), 2 * blk), :]
                o_dst[out_rows(bi), :] = jnp.dot(pn_ref[it % 2, g], vb, preferred_element_type=F32)
            if two_step:
                un_permute(it)

        return scores, probs, values

    stages = [make_stages(pi, dil, *refs) for pi, ((_, dil), *refs) in enumerate(zip(
        DILATED_PATTERNS, (q_ref, qp4, qp16), (kp1, kp4, kp16), (vp1, vp4, vp16), (o0, o1, o2), (l0, l1, l2),
        (copy_4, copy_16, None)))]

    def fill(scores, probs, values):
        scores(0)
        probs(0)
        scores(1)

    def drain(scores, probs, values):
        values(n_it - 2)
        probs(n_it - 1)
        values(n_it - 1)

    fill(*stages[0])
    for pi, (scores, probs, values) in enumerate(stages):
        def body(it, carry, scores=scores, probs=probs, values=values):
            values(it - 2)
            probs(it - 1)
            scores(it)
            return carry

        lax.fori_loop(2, n_it, body, 0)
        drain(scores, probs, values)
        if pi + 1 < len(stages):
            fill(*stages[pi + 1])

    mrows = ATT_MERGE_ROWS

    def merge(ci, carry):
        rows = pl.ds(pl.multiple_of(ci * mrows, mrows), mrows)
        a0, a1, a2 = l0[rows, :], l1[rows, :], l2[rows, :]
        mx = jnp.maximum(jnp.maximum(a0, a1), a2)
        e0, e1, e2 = jnp.exp(a0 - mx), jnp.exp(a1 - mx), jnp.exp(a2 - mx)
        y = (e0 * o0[rows, :] + e1 * o1[rows, :] + e2 * o2[rows, :]) / (e0 + e1 + e2)
        o_ref[rows, :] = (y * z_ref[rows, :].astype(F32)).astype(BF16)
        return carry

    lax.fori_loop(0, seq // mrows, merge, 0)


def _attention(proj, slopes, batch, seq, to_cast):
    m = proj.shape[0]
    hd = ATT_HD
    n_steps = batch * ATT_HEADS
    spec = lambda off: pl.BlockSpec((seq, hd), lambda b, h: (b, off // hd + h))
    slab = lambda a: pl.BlockSpec((a.shape[0] // n_steps, a.shape[1]), lambda b, h: (b * ATT_HEADS + h, 0))
    assert all(a.shape[0] % (BF16_ROWS * n_steps) == 0 for a in to_cast)
    f32_rows = pltpu.VMEM((seq, hd), F32)
    bf16_rows = pltpu.VMEM((seq, hd), BF16)
    bf16_padded = pltpu.VMEM((seq + ATT_BLOCK, hd), BF16)
    f32_group = pltpu.VMEM((ATT_GROUP * ATT_BLOCK, hd), F32)
    y_att, *cast = pl.pallas_call(
        functools.partial(_att_kernel, n_cast=len(to_cast)),
        out_shape=[jax.ShapeDtypeStruct((m, ATT_WIDTH), BF16)]
        + [jax.ShapeDtypeStruct(a.shape, BF16) for a in to_cast],
        grid=(batch, ATT_HEADS),
        in_specs=[
            pl.BlockSpec(memory_space=pltpu.SMEM),
            spec(OFF_AQ), spec(OFF_AK), spec(OFF_AV), spec(OFF_Z + RET_WIDTH),
        ] + [slab(a) for a in to_cast],
        out_specs=[pl.BlockSpec((seq, hd), lambda b, h: (b, h))] + [slab(a) for a in to_cast],
        scratch_shapes=[
            f32_group, f32_group, f32_group, f32_rows, f32_rows, f32_rows,
            bf16_rows, bf16_rows,
            bf16_padded, bf16_padded, bf16_padded, bf16_padded, bf16_padded, bf16_padded,
            pltpu.VMEM((len(DILATED_PATTERNS), 2, ATT_BLOCK, 2 * ATT_BLOCK), F32),
            pltpu.VMEM((2, ATT_GROUP, ATT_BLOCK, 2 * ATT_BLOCK), F32),
            pltpu.VMEM((2, ATT_GROUP, ATT_BLOCK, 2 * ATT_BLOCK), BF16),
            f32_rows, f32_rows, f32_rows, f32_rows, f32_rows, f32_rows,
        ],
        compiler_params=pltpu.CompilerParams(
            dimension_semantics=("arbitrary", "arbitrary"), vmem_limit_bytes=VMEM_LIMIT_BYTES),
        name="dilated_attention",
    )(slopes, proj, proj, proj, proj, *to_cast)
    return y_att, cast


def _post_tail(x1, p_ref, pg_ref, wg_ref, wp_ref, o_ref):
    h2 = _rms_rows(x1, pg_ref[...]).astype(BF16)
    pb = p_ref[...].astype(BF16)
    for cc in range(D_MODEL // POST_NC):
        sl = slice(cc * POST_NC, (cc + 1) * POST_NC)
        gate = jax.nn.sigmoid(jnp.dot(h2, wg_ref[:, sl], preferred_element_type=F32))
        pp = jnp.dot(pb, wp_ref[:, sl], preferred_element_type=F32)
        o_ref[:, sl] = x1[:, sl] + gate * pp


def _post_kernel(x_ref, ya_ref, yb_ref, p_ref, wa_ref, wb_ref, pg_ref, wg_ref, wp_ref, o_ref):
    x1 = x_ref[...] + jnp.dot(ya_ref[...], wa_ref[...], preferred_element_type=F32)
    x1 = x1 + jnp.dot(yb_ref[...], wb_ref[...], preferred_element_type=F32)
    _post_tail(x1, p_ref, pg_ref, wg_ref, wp_ref, o_ref)


def _const_spec(shape):
    return pl.BlockSpec(shape, lambda i: (0,) * len(shape), pipeline_mode=pl.Buffered(1))


def _ple_rows_spec(layer, m, tm):
    return pl.BlockSpec((tm, PLE_DIM), lambda i: (layer * (m // tm) + i, 0))


def _post(x2, ya, ya_blk, yb, yb_blk, p_all, layer, w_out_bf, pg, wg_all_bf, wp_all_bf, name):
    m = x2.shape[0]
    tm = POST_TM
    half = D_MODEL // 2
    rows = lambda w, blk=0: pl.BlockSpec((tm, w), lambda i: (i, blk))
    layer_spec = lambda k: pl.BlockSpec((None, k, D_MODEL), lambda i: (layer, 0, 0), pipeline_mode=pl.Buffered(1))
    return pl.pallas_call(
        _post_kernel,
        out_shape=jax.ShapeDtypeStruct((m, D_MODEL), F32),
        grid=(m // tm,),
        in_specs=[
            rows(D_MODEL), rows(half, ya_blk), rows(half, yb_blk), _ple_rows_spec(layer, m, tm),
            pl.BlockSpec((half, D_MODEL), lambda i: (0, 0), pipeline_mode=pl.Buffered(1)),
            pl.BlockSpec((half, D_MODEL), lambda i: (1, 0), pipeline_mode=pl.Buffered(1)),
            _const_spec((1, D_MODEL)), layer_spec(D_MODEL), layer_spec(PLE_DIM),
        ],
        out_specs=rows(D_MODEL),
        compiler_params=pltpu.CompilerParams(
            dimension_semantics=("arbitrary",), vmem_limit_bytes=VMEM_LIMIT_BYTES),
        name=name,
    )(x2, ya, yb, p_all, w_out_bf, w_out_bf, pg, wg_all_bf, wp_all_bf)


def _in_odd_kernel(xa_ref, xb_ref, g_ref, wb_ref, wc_ref, wu_ref, wz_ref, cw_ref, y_ref, hn_ref, ext_ref,
                   halo_ref, *, tiles_per_seq):
    i, j = pl.program_id(0), pl.program_id(1)
    tm = xa_ref.shape[0]

    @pl.when(j == 0)
    def _():
        _pre_norm(xa_ref, xb_ref, g_ref, hn_ref)

    @pl.when((i == 0) & (j == 0))
    def _():
        halo_ref[...] = jnp.zeros(halo_ref.shape, F32)

    hn = hn_ref[...]
    cu = (jnp.dot(hn, wc_ref[...], preferred_element_type=F32)
          * jnp.dot(hn, wu_ref[...], preferred_element_type=F32))
    first = (i % tiles_per_seq) == 0
    ext_ref[0:HALO, :] = jnp.where(first, 0.0, halo_ref[j])
    ext_ref[HALO:HALO + tm, :] = cu
    halo_ref[j] = cu[tm - HALO:, :]
    conv = (cw_ref[0:1, :] * ext_ref[HALO - 2:HALO - 2 + tm, :]
            + cw_ref[1:2, :] * ext_ref[HALO - 1:HALO - 1 + tm, :]
            + cw_ref[2:3, :] * cu)
    bg = jnp.dot(hn, wb_ref[...], preferred_element_type=F32)
    z = jnp.dot(hn, wz_ref[...], preferred_element_type=F32)
    y_ref[...] = (bg * conv * _silu(z)).astype(BF16)


def _in_odd(x2, g, w_bf, conv_w, seq):
    m = x2.shape[0]
    tm, tc = IN_TM, IN_ODD_TC
    nb = D_MODEL // tc
    wspec = lambda seg: pl.BlockSpec((D_MODEL, tc), lambda i, j: (0, seg * nb + j))
    return pl.pallas_call(
        functools.partial(_in_odd_kernel, tiles_per_seq=seq // tm),
        out_shape=jax.ShapeDtypeStruct((m, D_MODEL), BF16),
        grid=(m // tm, nb),
        in_specs=[
            *_x_half_specs(tm, m // tm, nb),
            pl.BlockSpec((1, D_MODEL), lambda i, j: (0, 0)),
            wspec(0), wspec(1), wspec(2), wspec(3),
            pl.BlockSpec((CONV_WIDTH, tc), lambda i, j: (0, j)),
        ],
        out_specs=pl.BlockSpec((tm, tc), lambda i, j: (i, j)),
        scratch_shapes=[
            pltpu.VMEM((tm, D_MODEL), BF16),
            pltpu.VMEM((HALO + tm, tc), F32),
            pltpu.VMEM((nb, HALO, tc), F32),
        ],
        compiler_params=pltpu.CompilerParams(
            dimension_semantics=("arbitrary", "arbitrary"), vmem_limit_bytes=VMEM_LIMIT_BYTES),
        name="in_odd",
    )(x2, x2, g, w_bf, w_bf, w_bf, w_bf, conv_w)


def _even_gain_row(q_g, k_g):
    ones = jnp.ones((OFF_AQ,), F32)
    return jnp.concatenate([
        ones,
        jnp.tile(q_g * ATT_HD ** -0.5, ATT_HEADS),
        jnp.tile(k_g, ATT_HEADS),
        jnp.ones((IN_WIDTH_EVEN - OFF_AV,), F32),
    ]).reshape(1, IN_WIDTH_EVEN)


def kernel(x, p, pre_norm_g, w_in_even, q_norm_g, k_norm_g, ret_gn_g, w_out_even, w_in_odd, conv_w_odd,
           w_out_odd, ple_norm_g, w_ple_gate, w_ple_proj):
    batch, seq, d = x.shape
    depth = p.shape[0]
    assert d == D_MODEL and seq % (ATT_BLOCK * DILATED_PATTERNS[-1][1]) == 0 and seq % IN_TM == 0
    m = batch * seq
    slopes = jnp.asarray([(2.0 ** (-8.0 / ATT_HEADS)) ** (i + 1) for i in range(ATT_HEADS)], F32)
    x2 = x.reshape(m, d)
    p_all = p.reshape(depth * m, PLE_DIM)
    late_f32 = dict(w_out_even=w_out_even, w_ple_gate=w_ple_gate, w_ple_proj=w_ple_proj, w_in_odd=w_in_odd,
                    w_out_odd=w_out_odd)
    if w_in_even.shape[0] > 1:
        late_f32["w_in_even_rest"] = w_in_even[1:]
    bf = None
    for i in range(depth):
        j = i // 2
        pg = ple_norm_g[i].reshape(1, d)
        g = pre_norm_g[i].reshape(1, d)
        if i % 2 == 0:
            w_in_bf = w_in_even[0].astype(BF16) if j == 0 else bf["w_in_even_rest"][j - 1]
            proj = _in_even(x2, g, w_in_bf, _even_gain_row(q_norm_g[j], k_norm_g[j]))
            y_ret = _retention(proj, ret_gn_g[j], batch, seq)
            to_cast = [a.reshape(-1, a.shape[-1]) for a in late_f32.values()] if bf is None else []
            y_att, cast = _attention(proj, slopes, batch, seq, to_cast)
            if bf is None:
                bf = {name: c.reshape(a.shape) for (name, a), c in zip(late_f32.items(), cast)}
            x2 = _post(x2, y_ret, 0, y_att, 0, p_all, i, bf["w_out_even"][j], pg, bf["w_ple_gate"],
                       bf["w_ple_proj"], "post_even")
        else:
            y = _in_odd(x2, g, bf["w_in_odd"][j], conv_w_odd[j], seq)
            x2 = _post(x2, y, 0, y, 1, p_all, i, bf["w_out_odd"][j], pg, bf["w_ple_gate"], bf["w_ple_proj"],
                       "post_odd")
    return x2.reshape(batch, seq, d)
```

```python
import functools
import math

import jax
import jax.numpy as jnp
from jax import lax
from jax.experimental import pallas as pl
from jax.experimental.pallas import tpu as pltpu

F32 = jnp.float32
BF16 = jnp.bfloat16

D_MODEL = 2048
PLE_DIM = 256
EPS = 1e-6
NEG = -1e30
RET_DK = 128
RET_DV = 256
RET_HEADS = 4
RET_CHUNK = 256
ATT_HD = 128
ATT_HEADS = 8
ATT_BLOCK = 128
DILATED_PATTERNS = ((128, 1), (512, 4), (2048, 16))
RET_QK_WIDTH = RET_HEADS * RET_DK
RET_WIDTH = RET_HEADS * RET_DV
ATT_WIDTH = ATT_HEADS * ATT_HD
IN_WIDTH_EVEN = 2 * RET_QK_WIDTH + RET_WIDTH + 3 * ATT_WIDTH + RET_WIDTH + ATT_WIDTH
CONV_WIDTH = 3

REF_AQ = 2 * RET_QK_WIDTH + RET_WIDTH
REF_AV = REF_AQ + 2 * ATT_WIDTH
REF_Z = REF_AV + ATT_WIDTH
OFF_RQ = 0
OFF_RK = RET_QK_WIDTH
OFF_RV = 2 * RET_QK_WIDTH
OFF_AV = OFF_RV + RET_WIDTH
OFF_AQ = OFF_AV + ATT_WIDTH
OFF_AK = OFF_AQ + ATT_WIDTH
OFF_Z = OFF_AK + ATT_WIDTH

VMEM_LIMIT_BYTES = 56 * 1024 * 1024
BF16_ROWS = 16

IN_TM = 1024
IN_EVEN_TN = 1792
IN_EVEN_NC = 256
IN_EVEN_LAST_ROW_SPLIT = 4
IN_ODD_TC = 256
RET_TQ = 1024
ATT_GROUP = 4
ATT_MERGE_ROWS = 512
POST_TM = 512
POST_NC = 256
HALO = 8


def _rms_rows(x, g):
    ms = jnp.mean(x * x, axis=-1, keepdims=True)
    return x * lax.rsqrt(ms + EPS) * g


def _silu(z):
    return z * jax.nn.sigmoid(z)


def _silu_cols(z, cols):
    del cols
    return _silu(z)


def _in_even_kernel(x_ref, g_ref, w_ref, gain_ref, o_ref, hn_ref):
    j = pl.program_id(1)

    @pl.when(j == 0)
    def _():
        hn_ref[...] = _rms_rows(x_ref[...], g_ref[...]).astype(BF16)

    tm, tn = o_ref.shape
    n_chunks = tn // IN_EVEN_NC

    def tile(jj):
        for cc in range(n_chunks):
            col0 = jj * tn + cc * IN_EVEN_NC
            n_row = IN_EVEN_LAST_ROW_SPLIT if cc == n_chunks - 1 else 1
            for rr in range(n_row):
                rows = slice(rr * (tm // n_row), (rr + 1) * (tm // n_row))
                acc = jnp.dot(hn_ref[rows, :], w_ref[:, cc * IN_EVEN_NC:(cc + 1) * IN_EVEN_NC],
                              preferred_element_type=F32)
                for hh in range(IN_EVEN_NC // ATT_HD):
                    sl = slice(cc * IN_EVEN_NC + hh * ATT_HD, cc * IN_EVEN_NC + (hh + 1) * ATT_HD)
                    a = acc[:, hh * ATT_HD:(hh + 1) * ATT_HD]
                    if col0 >= OFF_Z:
                        a = _silu(a)
                    elif col0 >= OFF_AQ:
                        a = _rms_rows(a, gain_ref[:, sl])
                    o_ref[rows, sl] = a.astype(BF16)

    for jj in range(IN_WIDTH_EVEN // tn):
        pl.when(j == jj)(functools.partial(tile, jj))


def _in_even(x2, g, w_bf, gain_row):
    m = x2.shape[0]
    tm, tn = IN_TM, IN_EVEN_TN
    return pl.pallas_call(
        _in_even_kernel,
        out_shape=jax.ShapeDtypeStruct((m, IN_WIDTH_EVEN), BF16),
        grid=(m // tm, IN_WIDTH_EVEN // tn),
        in_specs=[
            pl.BlockSpec((tm, D_MODEL), lambda i, j: (i, 0)),
            pl.BlockSpec((1, D_MODEL), lambda i, j: (0, 0)),
            pl.BlockSpec((D_MODEL, tn), lambda i, j: (0, j)),
            pl.BlockSpec((1, tn), lambda i, j: (0, j)),
        ],
        out_specs=pl.BlockSpec((tm, tn), lambda i, j: (i, j)),
        scratch_shapes=[pltpu.VMEM((tm, D_MODEL), BF16)],
        compiler_params=pltpu.CompilerParams(
            dimension_semantics=("arbitrary", "arbitrary"), vmem_limit_bytes=VMEM_LIMIT_BYTES),
        name="in_even",
    )(x2, g, w_bf, gain_row)


def _ret_kernel(q_ref, k_ref, v_ref, z_ref, gn_ref, o_ref, r_ref, decay_ref):
    c = RET_CHUNK
    tq = q_ref.shape[0]
    kscale = RET_DK ** -0.5
    log_g = [math.log(1.0 - 2.0 ** (-5.0 - h)) for h in range(RET_HEADS)]

    @pl.when(pl.program_id(1) == 0)
    def _():
        r_ref[...] = jnp.zeros(r_ref.shape, F32)

    @pl.when((pl.program_id(0) == 0) & (pl.program_id(1) == 0))
    def _():
        row = lax.broadcasted_iota(jnp.int32, (c, c), 0)
        col = lax.broadcasted_iota(jnp.int32, (c, c), 1)
        diff = (row - col).astype(F32)
        for h in range(RET_HEADS):
            decay_ref[h] = jnp.where(diff >= 0, jnp.exp(log_g[h] * jnp.maximum(diff, 0.0)), 0.0) * kscale

    pos = lax.broadcasted_iota(jnp.int32, (c, 1), 0).astype(F32)

    for h in range(RET_HEADS):
        lg = log_g[h]
        decay = decay_ref[h]
        xi = jnp.exp(lg * (pos + 1.0))
        zeta = jnp.exp(lg * (c - 1.0 - pos)) * kscale
        g_chunk = math.exp(lg * c)
        qk_sl = slice(h * RET_DK, (h + 1) * RET_DK)
        v_sl = slice(h * RET_DV, (h + 1) * RET_DV)
        gn = gn_ref[h:h + 1, :]
        chunks = [slice(ci * c, (ci + 1) * c) for ci in range(tq // c)]

        contrib = []
        for rows in chunks:
            kz_t = (k_ref[rows, qk_sl].astype(F32) * zeta).T.astype(BF16)
            contrib.append(jnp.dot(kz_t, v_ref[rows, v_sl], preferred_element_type=F32))

        state = r_ref[h]
        states = []
        for s_c in contrib:
            states.append(state)
            state = g_chunk * state + s_c
        r_ref[h] = state

        for rows, r_prev in zip(chunks, states):
            q = q_ref[rows, qk_sl]
            v = v_ref[rows, v_sl]
            s = lax.dot_general(q, k_ref[rows, qk_sl], (((1,), (1,)), ((), ())),
                                preferred_element_type=F32) * decay
            y = jnp.dot(s.astype(BF16), v, preferred_element_type=F32)
            y = y + jnp.dot(q, r_prev.astype(BF16), preferred_element_type=F32) * xi
            mu = jnp.mean(y, axis=-1, keepdims=True)
            yc = y - mu
            var = jnp.mean(yc * yc, axis=-1, keepdims=True)
            yn = yc * lax.rsqrt(var + EPS) * gn
            o_ref[rows, v_sl] = (yn * z_ref[rows, v_sl].astype(F32)).astype(BF16)


def _retention(proj, gn_g, batch, seq):
    m = proj.shape[0]
    tq = RET_TQ
    nt = seq // tq
    row = lambda b, t: b * nt + t
    return pl.pallas_call(
        _ret_kernel,
        out_shape=jax.ShapeDtypeStruct((m, RET_WIDTH), BF16),
        grid=(batch, nt),
        in_specs=[
            pl.BlockSpec((tq, RET_QK_WIDTH), lambda b, t: (row(b, t), OFF_RQ // RET_QK_WIDTH)),
            pl.BlockSpec((tq, RET_QK_WIDTH), lambda b, t: (row(b, t), OFF_RK // RET_QK_WIDTH)),
            pl.BlockSpec((tq, RET_WIDTH), lambda b, t: (row(b, t), OFF_RV // RET_WIDTH)),
            pl.BlockSpec((tq, RET_WIDTH), lambda b, t: (row(b, t), OFF_Z // RET_WIDTH)),
            pl.BlockSpec((RET_HEADS, RET_DV), lambda b, t: (0, 0)),
        ],
        out_specs=pl.BlockSpec((tq, RET_WIDTH), lambda b, t: (row(b, t), 0)),
        scratch_shapes=[pltpu.VMEM((RET_HEADS, RET_DK, RET_DV), F32),
                        pltpu.VMEM((RET_HEADS, RET_CHUNK, RET_CHUNK), F32)],
        compiler_params=pltpu.CompilerParams(
            dimension_semantics=("arbitrary", "arbitrary"), vmem_limit_bytes=VMEM_LIMIT_BYTES),
        name="retention",
    )(proj, proj, proj, proj, gn_g)


def _att_kernel(*refs, n_cast):
    slopes_ref, q_ref, k_ref, v_ref, z_ref = refs[:5]
    cast_in, o_ref, cast_out = refs[5:5 + n_cast], refs[5 + n_cast], refs[6 + n_cast:6 + 2 * n_cast]
    (stage_q, stage_k, stage_v, q4, k4, v4, qp4, qp16, kp1, kp4, kp16, vp1, vp4, vp16, bias_ref, s_ref, pn_ref,
     o0, o1, o2, l0, l1, l2) = refs[6 + 2 * n_cast:]
    for src, dst in zip(cast_in, cast_out):
        dst[...] = src[...].astype(BF16)
    blk = ATT_BLOCK
    grp = ATT_GROUP
    seq = q_ref.shape[0]
    nblk = seq // blk
    n_it = nblk // grp
    d4, d16 = DILATED_PATTERNS[1][1], DILATED_PATTERNS[2][1]
    assert DILATED_PATTERNS[0][1] == 1 and grp == d4 and d16 == d4 * d4
    assert n_it % 2 == 0 and n_it >= 4 and nblk % d16 == 0
    slope = slopes_ref[pl.program_id(1)]
    nt = (((1,), (1,)), ((), ()))

    def natural_rows(bi, dil):
        nb = nblk // dil
        start = (bi // nb) + (bi % nb) * (blk * dil)
        if dil == 1:
            return pl.ds(pl.multiple_of(start, blk), blk)
        return pl.ds(start, blk, stride=dil)

    def block_rows(bi, lead=0):
        return pl.ds(pl.multiple_of(bi * blk + lead, blk), blk)

    for pad in (kp1, kp4, kp16, vp1, vp4, vp16):
        pad[0:blk, :] = jnp.zeros((blk, ATT_HD), BF16)
    kp1[blk:, :] = k_ref[...]
    vp1[blk:, :] = v_ref[...]

    def copy_4(it):
        rows = pl.ds(pl.multiple_of(it * (grp * blk), grp * blk), grp * blk)
        stage_q[...] = q_ref[rows, :].astype(F32)
        stage_k[...] = k_ref[rows, :].astype(F32)
        stage_v[...] = v_ref[rows, :].astype(F32)
        for r in range(d4):
            bi = r * (nblk // d4) + it
            src = pl.ds(r, blk, stride=d4)
            q, k, v = stage_q[src, :], stage_k[src, :], stage_v[src, :]
            q4[block_rows(bi), :], k4[block_rows(bi), :], v4[block_rows(bi), :] = q, k, v
            qp4[block_rows(bi), :] = q.astype(BF16)
            kp4[block_rows(bi, blk), :] = k.astype(BF16)
            vp4[block_rows(bi, blk), :] = v.astype(BF16)

    def copy_16(it):
        halves = (nblk // d4) // grp
        r4, n16 = it // halves, it % halves
        for c in range(d4):
            bi = (r4 + d4 * c) * (nblk // d16) + n16
            src = pl.ds(r4 * (seq // d4) + n16 * (grp * blk) + c, blk, stride=d4)
            qp16[block_rows(bi), :] = q4[src, :].astype(BF16)
            kp16[block_rows(bi, blk), :] = k4[src, :].astype(BF16)
            vp16[block_rows(bi, blk), :] = v4[src, :].astype(BF16)

    qi = lax.broadcasted_iota(jnp.int32, (blk, 2 * blk), 0)
    kj = lax.broadcasted_iota(jnp.int32, (blk, 2 * blk), 1)
    rel = qi - kj + blk
    for pi, (window, dil) in enumerate(DILATED_PATTERNS):
        valid = (rel >= 0) & (rel <= window // dil)
        alibi = -slope * (rel * dil).astype(F32)
        bias_ref[pi, 0] = jnp.where(valid & (kj >= blk), alibi, NEG)
        bias_ref[pi, 1] = jnp.where(valid, alibi, NEG)

    o16p, l16p = q4, k4

    def rows_in_d4_order(bi16):
        r16, n16 = bi16 // (nblk // d16), bi16 % (nblk // d16)
        return pl.ds((r16 % d4) * (seq // d4) + n16 * (grp * blk) + r16 // d4, blk, stride=d4)

    def un_permute(it):
        for g in range(grp):
            bi4 = it * grp + g
            o2[natural_rows(bi4, d4), :] = o16p[block_rows(bi4), :]
            l2[natural_rows(bi4, d4), :] = l16p[block_rows(bi4), :]

    def make_stages(pi, dil, q_src, kp, vp, o_g, l_g, copy_next):
        nb = nblk // dil
        two_step = dil == d16
        block_of = (lambda it, g: it + n_it * g) if two_step else (lambda it, g: it * grp + g)
        out_rows = rows_in_d4_order if two_step else (lambda bi: natural_rows(bi, dil))
        o_dst, l_dst = (o16p, l16p) if two_step else (o_g, l_g)

        def scores(it):
            for g in range(grp):
                bi = block_of(it, g)
                kb = kp[pl.ds(pl.multiple_of(bi * blk, blk), 2 * blk), :]
                s_ref[it % 2, g] = lax.dot_general(q_src[block_rows(bi), :], kb, nt, preferred_element_type=F32)
            if copy_next is not None:
                copy_next(it)

        def probs(it):
            for g in range(grp):
                bi = block_of(it, g)
                s = s_ref[it % 2, g] + bias_ref[pi, jnp.minimum(bi % nb, 1)]
                mx = jnp.max(jnp.maximum(s[:, :blk], s[:, blk:]), axis=-1, keepdims=True)
                pr = jnp.exp(s - mx)
                den = jnp.sum(pr[:, :blk] + pr[:, blk:], axis=-1, keepdims=True)
                pn_ref[it % 2, g] = (pr * (1.0 / den)).astype(BF16)
                l_dst[out_rows(bi), :] = jnp.broadcast_to(mx + jnp.log(den), (blk, ATT_HD))

        def values(it):
            for g in range(grp):
                bi = block_of(it, g)
                vb = vp[pl.ds(pl.multiple_of(bi * blk, blk), 2 * blk), :]
                o_dst[out_rows(bi), :] = jnp.dot(pn_ref[it % 2, g], vb, preferred_element_type=F32)
            if two_step:
                un_permute(it)

        return scores, probs, values

    stages = [make_stages(pi, dil, *refs) for pi, ((_, dil), *refs) in enumerate(zip(
        DILATED_PATTERNS, (q_ref, qp4, qp16), (kp1, kp4, kp16), (vp1, vp4, vp16), (o0, o1, o2), (l0, l1, l2),
        (copy_4, copy_16, None)))]

    def fill(scores, probs, values):
        scores(0)
        probs(0)
        scores(1)

    def drain(scores, probs, values):
        values(n_it - 2)
        probs(n_it - 1)
        values(n_it - 1)

    fill(*stages[0])
    for pi, (scores, probs, values) in enumerate(stages):
        def body(it, carry, scores=scores, probs=probs, values=values):
            values(it - 2)
            probs(it - 1)
            scores(it)
            return carry

        lax.fori_loop(2, n_it, body, 0)
        drain(scores, probs, values)
        if pi + 1 < len(stages):
            fill(*stages[pi + 1])

    mrows = ATT_MERGE_ROWS

    def merge(ci, carry):
        rows = pl.ds(pl.multiple_of(ci * mrows, mrows), mrows)
        a0, a1, a2 = l0[rows, :], l1[rows, :], l2[rows, :]
        mx = jnp.maximum(jnp.maximum(a0, a1), a2)
        e0, e1, e2 = jnp.exp(a0 - mx), jnp.exp(a1 - mx), jnp.exp(a2 - mx)
        y = (e0 * o0[rows, :] + e1 * o1[rows, :] + e2 * o2[rows, :]) / (e0 + e1 + e2)
        o_ref[rows, :] = (y * z_ref[rows, :].astype(F32)).astype(BF16)
        return carry

    lax.fori_loop(0, seq // mrows, merge, 0)


def _attention(proj, slopes, batch, seq, to_cast):
    m = proj.shape[0]
    hd = ATT_HD
    n_steps = batch * ATT_HEADS
    spec = lambda off: pl.BlockSpec((seq, hd), lambda b, h: (b, off // hd + h))
    slab = lambda a: pl.BlockSpec((a.shape[0] // n_steps, a.shape[1]), lambda b, h: (b * ATT_HEADS + h, 0))
    assert all(a.shape[0] % (BF16_ROWS * n_steps) == 0 for a in to_cast)
    f32_rows = pltpu.VMEM((seq, hd), F32)
    bf16_rows = pltpu.VMEM((seq, hd), BF16)
    bf16_padded = pltpu.VMEM((seq + ATT_BLOCK, hd), BF16)
    f32_group = pltpu.VMEM((ATT_GROUP * ATT_BLOCK, hd), F32)
    y_att, *cast = pl.pallas_call(
        functools.partial(_att_kernel, n_cast=len(to_cast)),
        out_shape=[jax.ShapeDtypeStruct((m, ATT_WIDTH), BF16)]
        + [jax.ShapeDtypeStruct(a.shape, BF16) for a in to_cast],
        grid=(batch, ATT_HEADS),
        in_specs=[
            pl.BlockSpec(memory_space=pltpu.SMEM),
            spec(OFF_AQ), spec(OFF_AK), spec(OFF_AV), spec(OFF_Z + RET_WIDTH),
        ] + [slab(a) for a in to_cast],
        out_specs=[pl.BlockSpec((seq, hd), lambda b, h: (b, h))] + [slab(a) for a in to_cast],
        scratch_shapes=[
            f32_group, f32_group, f32_group, f32_rows, f32_rows, f32_rows,
            bf16_rows, bf16_rows,
            bf16_padded, bf16_padded, bf16_padded, bf16_padded, bf16_padded, bf16_padded,
            pltpu.VMEM((len(DILATED_PATTERNS), 2, ATT_BLOCK, 2 * ATT_BLOCK), F32),
            pltpu.VMEM((2, ATT_GROUP, ATT_BLOCK, 2 * ATT_BLOCK), F32),
            pltpu.VMEM((2, ATT_GROUP, ATT_BLOCK, 2 * ATT_BLOCK), BF16),
            f32_rows, f32_rows, f32_rows, f32_rows, f32_rows, f32_rows,
        ],
        compiler_params=pltpu.CompilerParams(
            dimension_semantics=("arbitrary", "arbitrary"), vmem_limit_bytes=VMEM_LIMIT_BYTES),
        name="dilated_attention",
    )(slopes, proj, proj, proj, proj, *to_cast)
    return y_att, cast


def _post_tail(x1, p_ref, pg_ref, wg_ref, wp_ref, o_ref):
    h2 = _rms_rows(x1, pg_ref[...]).astype(BF16)
    pb = p_ref[...].astype(BF16)
    for cc in range(D_MODEL // POST_NC):
        sl = slice(cc * POST_NC, (cc + 1) * POST_NC)
        gate = jax.nn.sigmoid(jnp.dot(h2, wg_ref[:, sl], preferred_element_type=F32))
        pp = jnp.dot(pb, wp_ref[:, sl], preferred_element_type=F32)
        o_ref[:, sl] = x1[:, sl] + gate * pp


def _post_kernel(x_ref, ya_ref, yb_ref, p_ref, wa_ref, wb_ref, pg_ref, wg_ref, wp_ref, o_ref):
    x1 = x_ref[...] + jnp.dot(ya_ref[...], wa_ref[...], preferred_element_type=F32)
    x1 = x1 + jnp.dot(yb_ref[...], wb_ref[...], preferred_element_type=F32)
    _post_tail(x1, p_ref, pg_ref, wg_ref, wp_ref, o_ref)


def _const_spec(shape):
    return pl.BlockSpec(shape, lambda i: (0,) * len(shape), pipeline_mode=pl.Buffered(1))


def _ple_rows_spec(layer, m, tm):
    return pl.BlockSpec((tm, PLE_DIM), lambda i: (layer * (m // tm) + i, 0))


def _post(x2, ya, ya_blk, yb, yb_blk, p_all, layer, w_out_bf, pg, wg_all_bf, wp_all_bf, name):
    m = x2.shape[0]
    tm = POST_TM
    half = D_MODEL // 2
    rows = lambda w, blk=0: pl.BlockSpec((tm, w), lambda i: (i, blk))
    layer_spec = lambda k: pl.BlockSpec((None, k, D_MODEL), lambda i: (layer, 0, 0), pipeline_mode=pl.Buffered(1))
    return pl.pallas_call(
        _post_kernel,
        out_shape=jax.ShapeDtypeStruct((m, D_MODEL), F32),
        grid=(m // tm,),
        in_specs=[
            rows(D_MODEL), rows(half, ya_blk), rows(half, yb_blk), _ple_rows_spec(layer, m, tm),
            pl.BlockSpec((half, D_MODEL), lambda i: (0, 0), pipeline_mode=pl.Buffered(1)),
            pl.BlockSpec((half, D_MODEL), lambda i: (1, 0), pipeline_mode=pl.Buffered(1)),
            _const_spec((1, D_MODEL)), layer_spec(D_MODEL), layer_spec(PLE_DIM),
        ],
        out_specs=rows(D_MODEL),
        compiler_params=pltpu.CompilerParams(
            dimension_semantics=("arbitrary",), vmem_limit_bytes=VMEM_LIMIT_BYTES),
        name=name,
    )(x2, ya, yb, p_all, w_out_bf, w_out_bf, pg, wg_all_bf, wp_all_bf)


def _in_odd_kernel(x_ref, g_ref, wb_ref, wc_ref, wu_ref, wz_ref, cw_ref, y_ref, hn_ref, ext_ref, halo_ref, *,
                   tiles_per_seq):
    i, j = pl.program_id(0), pl.program_id(1)
    tm = x_ref.shape[0]

    @pl.when(j == 0)
    def _():
        hn_ref[...] = _rms_rows(x_ref[...], g_ref[...]).astype(BF16)

    @pl.when((i == 0) & (j == 0))
    def _():
        halo_ref[...] = jnp.zeros(halo_ref.shape, F32)

    hn = hn_ref[...]
    cu = (jnp.dot(hn, wc_ref[...], preferred_element_type=F32)
          * jnp.dot(hn, wu_ref[...], preferred_element_type=F32))
    first = (i % tiles_per_seq) == 0
    ext_ref[0:HALO, :] = jnp.where(first, 0.0, halo_ref[j])
    ext_ref[HALO:HALO + tm, :] = cu
    halo_ref[j] = cu[tm - HALO:, :]
    conv = (cw_ref[0:1, :] * ext_ref[HALO - 2:HALO - 2 + tm, :]
            + cw_ref[1:2, :] * ext_ref[HALO - 1:HALO - 1 + tm, :]
            + cw_ref[2:3, :] * cu)
    bg = jnp.dot(hn, wb_ref[...], preferred_element_type=F32)
    z = jnp.dot(hn, wz_ref[...], preferred_element_type=F32)
    y_ref[...] = (bg * conv * _silu(z)).astype(BF16)


def _in_odd(x2, g, w_bf, conv_w, seq):
    m = x2.shape[0]
    tm, tc = IN_TM, IN_ODD_TC
    nb = D_MODEL // tc
    wspec = lambda seg: pl.BlockSpec((D_MODEL, tc), lambda i, j: (0, seg * nb + j))
    return pl.pallas_call(
        functools.partial(_in_odd_kernel, tiles_per_seq=seq // tm),
        out_shape=jax.ShapeDtypeStruct((m, D_MODEL), BF16),
        grid=(m // tm, nb),
        in_specs=[
            pl.BlockSpec((tm, D_MODEL), lambda i, j: (i, 0)),
            pl.BlockSpec((1, D_MODEL), lambda i, j: (0, 0)),
            wspec(0), wspec(1), wspec(2), wspec(3),
            pl.BlockSpec((CONV_WIDTH, tc), lambda i, j: (0, j)),
        ],
        out_specs=pl.BlockSpec((tm, tc), lambda i, j: (i, j)),
        scratch_shapes=[
            pltpu.VMEM((tm, D_MODEL), BF16),
            pltpu.VMEM((HALO + tm, tc), F32),
            pltpu.VMEM((nb, HALO, tc), F32),
        ],
        compiler_params=pltpu.CompilerParams(
            dimension_semantics=("arbitrary", "arbitrary"), vmem_limit_bytes=VMEM_LIMIT_BYTES),
        name="in_odd",
    )(x2, g, w_bf, w_bf, w_bf, w_bf, conv_w)


def _even_gain_row(q_g, k_g):
    return jnp.concatenate([
        jnp.ones((OFF_AQ,), F32),
        jnp.tile(q_g * ATT_HD ** -0.5, ATT_HEADS),
        jnp.tile(k_g, ATT_HEADS),
        jnp.ones((IN_WIDTH_EVEN - OFF_Z,), F32),
    ]).reshape(1, IN_WIDTH_EVEN)


def _even_kernel_columns(w):
    return jnp.concatenate([w[..., :REF_AQ], w[..., REF_AV:REF_Z], w[..., REF_AQ:REF_AV], w[..., REF_Z:]], axis=-1)


def kernel(x, p, pre_norm_g, w_in_even, q_norm_g, k_norm_g, ret_gn_g, w_out_even, w_in_odd, conv_w_odd,
           w_out_odd, ple_norm_g, w_ple_gate, w_ple_proj):
    batch, seq, d = x.shape
    depth = p.shape[0]
    assert d == D_MODEL and seq % (ATT_BLOCK * DILATED_PATTERNS[-1][1]) == 0 and seq % IN_TM == 0
    m = batch * seq
    slopes = jnp.asarray([(2.0 ** (-8.0 / ATT_HEADS)) ** (i + 1) for i in range(ATT_HEADS)], F32)
    x2 = x.reshape(m, d)
    p_all = p.reshape(depth * m, PLE_DIM)
    late_f32 = dict(w_out_even=w_out_even, w_ple_gate=w_ple_gate, w_ple_proj=w_ple_proj, w_in_odd=w_in_odd,
                    w_out_odd=w_out_odd)
    if w_in_even.shape[0] > 1:
        late_f32["w_in_even_rest"] = _even_kernel_columns(w_in_even[1:])
    bf = None
    for i in range(depth):
        j = i // 2
        pg = ple_norm_g[i].reshape(1, d)
        g = pre_norm_g[i].reshape(1, d)
        if i % 2 == 0:
            w_in_bf = _even_kernel_columns(w_in_even[0]).astype(BF16) if j == 0 else bf["w_in_even_rest"][j - 1]
            proj = _in_even(x2, g, w_in_bf, _even_gain_row(q_norm_g[j], k_norm_g[j]))
            y_ret = _retention(proj, ret_gn_g[j], batch, seq)
            to_cast = [a.reshape(-1, a.shape[-1]) for a in late_f32.values()] if bf is None else []
            y_att, cast = _attention(proj, slopes, batch, seq, to_cast)
            if bf is None:
                bf = {name: c.reshape(a.shape) for (name, a), c in zip(late_f32.items(), cast)}
            x2 = _post(x2, y_ret, 0, y_att, 0, p_all, i, bf["w_out_even"][j], pg, bf["w_ple_gate"],
                       bf["w_ple_proj"], "post_even")
        else:
            y = _in_odd(x2, g, bf["w_in_odd"][j], conv_w_odd[j], seq)
            x2 = _post(x2, y, 0, y, 1, p_all, i, bf["w_out_odd"][j], pg, bf["w_ple_gate"], bf["w_ple_proj"],
                       "post_odd")
    return x2.reshape(batch, seq, d)
```

```python
import functools
import math

import jax
import jax.numpy as jnp
from jax import lax
from jax.experimental import pallas as pl
from jax.experimental.pallas import tpu as pltpu

F32 = jnp.float32
BF16 = jnp.bfloat16

D_MODEL = 2048
PLE_DIM = 256
EPS = 1e-6
NEG = -1e30
RET_DK = 128
RET_DV = 256
RET_HEADS = 4
RET_CHUNK = 256
ATT_HD = 128
ATT_HEADS = 8
ATT_BLOCK = 128
DILATED_PATTERNS = ((128, 1), (512, 4), (2048, 16))
RET_QK_WIDTH = RET_HEADS * RET_DK
RET_WIDTH = RET_HEADS * RET_DV
ATT_WIDTH = ATT_HEADS * ATT_HD
IN_WIDTH_EVEN = 2 * RET_QK_WIDTH + RET_WIDTH + 3 * ATT_WIDTH + RET_WIDTH + ATT_WIDTH
CONV_WIDTH = 3

REF_AQ = 2 * RET_QK_WIDTH + RET_WIDTH
REF_AV = REF_AQ + 2 * ATT_WIDTH
OFF_RQ = 0
OFF_RK = RET_QK_WIDTH
OFF_RV = 2 * RET_QK_WIDTH
OFF_AV = OFF_RV + RET_WIDTH
OFF_AQ = OFF_AV + ATT_WIDTH
OFF_AK = OFF_AQ + ATT_WIDTH
OFF_Z = OFF_AK + ATT_WIDTH

VMEM_LIMIT_BYTES = 56 * 1024 * 1024
BF16_ROWS = 16

IN_TM = 1024
IN_EVEN_TN = 1792
IN_EVEN_NC = 256
IN_EVEN_LAST_ROW_SPLIT = 4
IN_ODD_TC = 256
RET_TQ = 1024
ATT_GROUP = 4
ATT_MERGE_ROWS = 512
POST_TM = 512
POST_NC = 256
HALO = 8


def _rms_rows(x, g):
    ms = jnp.mean(x * x, axis=-1, keepdims=True)
    return x * lax.rsqrt(ms + EPS) * g


def _silu(z):
    return z * jax.nn.sigmoid(z)


def _in_even_kernel(x_ref, g_ref, gain_ref, *refs):
    *w_refs, o_ref, hn_ref = refs
    j = pl.program_id(1)

    @pl.when(j == 0)
    def _():
        hn_ref[...] = _rms_rows(x_ref[...], g_ref[...]).astype(BF16)

    tm, tn = o_ref.shape
    n_chunks = tn // IN_EVEN_NC

    def tile(jj):
        for cc in range(n_chunks):
            col0 = jj * tn + cc * IN_EVEN_NC
            n_row = IN_EVEN_LAST_ROW_SPLIT if cc == n_chunks - 1 else 1
            for rr in range(n_row):
                rows = slice(rr * (tm // n_row), (rr + 1) * (tm // n_row))
                acc = jnp.dot(hn_ref[rows, :], w_refs[cc][...], preferred_element_type=F32)
                for hh in range(IN_EVEN_NC // ATT_HD):
                    sl = slice(cc * IN_EVEN_NC + hh * ATT_HD, cc * IN_EVEN_NC + (hh + 1) * ATT_HD)
                    a = acc[:, hh * ATT_HD:(hh + 1) * ATT_HD]
                    if col0 >= OFF_Z:
                        a = _silu(a)
                    elif col0 >= OFF_AQ:
                        a = _rms_rows(a, gain_ref[:, sl])
                    o_ref[rows, sl] = a.astype(BF16)

    for jj in range(IN_WIDTH_EVEN // tn):
        pl.when(j == jj)(functools.partial(tile, jj))


def _even_weight_chunk_spec(cc, chunks_per_tile):
    nc = IN_EVEN_NC

    def index(i, j):
        k = j * chunks_per_tile + cc
        in_av = (k >= OFF_AV // nc) & (k < OFF_AQ // nc)
        in_qk = (k >= OFF_AQ // nc) & (k < OFF_Z // nc)
        ref = jnp.where(in_av, k + (REF_AV - OFF_AV) // nc, jnp.where(in_qk, k - (OFF_AQ - REF_AQ) // nc, k))
        return (0, ref)

    return pl.BlockSpec((D_MODEL, nc), index)


def _in_even(x2, g, w_bf, gain_row):
    m = x2.shape[0]
    tm, tn = IN_TM, IN_EVEN_TN
    n_chunks = tn // IN_EVEN_NC
    return pl.pallas_call(
        _in_even_kernel,
        out_shape=jax.ShapeDtypeStruct((m, IN_WIDTH_EVEN), BF16),
        grid=(m // tm, IN_WIDTH_EVEN // tn),
        in_specs=[
            pl.BlockSpec((tm, D_MODEL), lambda i, j: (i, 0)),
            pl.BlockSpec((1, D_MODEL), lambda i, j: (0, 0)),
            pl.BlockSpec((1, tn), lambda i, j: (0, j)),
            *[_even_weight_chunk_spec(cc, n_chunks) for cc in range(n_chunks)],
        ],
        out_specs=pl.BlockSpec((tm, tn), lambda i, j: (i, j)),
        scratch_shapes=[pltpu.VMEM((tm, D_MODEL), BF16)],
        compiler_params=pltpu.CompilerParams(
            dimension_semantics=("arbitrary", "arbitrary"), vmem_limit_bytes=VMEM_LIMIT_BYTES),
        name="in_even",
    )(x2, g, gain_row, *([w_bf] * n_chunks))


def _ret_kernel(q_ref, k_ref, v_ref, z_ref, gn_ref, o_ref, r_ref, decay_ref):
    c = RET_CHUNK
    tq = q_ref.shape[0]
    kscale = RET_DK ** -0.5
    log_g = [math.log(1.0 - 2.0 ** (-5.0 - h)) for h in range(RET_HEADS)]

    @pl.when(pl.program_id(1) == 0)
    def _():
        r_ref[...] = jnp.zeros(r_ref.shape, F32)

    @pl.when((pl.program_id(0) == 0) & (pl.program_id(1) == 0))
    def _():
        row = lax.broadcasted_iota(jnp.int32, (c, c), 0)
        col = lax.broadcasted_iota(jnp.int32, (c, c), 1)
        diff = (row - col).astype(F32)
        for h in range(RET_HEADS):
            decay_ref[h] = jnp.where(diff >= 0, jnp.exp(log_g[h] * jnp.maximum(diff, 0.0)), 0.0) * kscale

    pos = lax.broadcasted_iota(jnp.int32, (c, 1), 0).astype(F32)

    for h in range(RET_HEADS):
        lg = log_g[h]
        decay = decay_ref[h]
        xi = jnp.exp(lg * (pos + 1.0))
        zeta = jnp.exp(lg * (c - 1.0 - pos)) * kscale
        g_chunk = math.exp(lg * c)
        qk_sl = slice(h * RET_DK, (h + 1) * RET_DK)
        v_sl = slice(h * RET_DV, (h + 1) * RET_DV)
        gn = gn_ref[h:h + 1, :]
        chunks = [slice(ci * c, (ci + 1) * c) for ci in range(tq // c)]

        contrib = []
        for rows in chunks:
            kz_t = (k_ref[rows, qk_sl].astype(F32) * zeta).T.astype(BF16)
            contrib.append(jnp.dot(kz_t, v_ref[rows, v_sl], preferred_element_type=F32))

        state = r_ref[h]
        states = []
        for s_c in contrib:
            states.append(state)
            state = g_chunk * state + s_c
        r_ref[h] = state

        for rows, r_prev in zip(chunks, states):
            q = q_ref[rows, qk_sl]
            v = v_ref[rows, v_sl]
            s = lax.dot_general(q, k_ref[rows, qk_sl], (((1,), (1,)), ((), ())),
                                preferred_element_type=F32) * decay
            y = jnp.dot(s.astype(BF16), v, preferred_element_type=F32)
            y = y + jnp.dot(q, r_prev.astype(BF16), preferred_element_type=F32) * xi
            mu = jnp.mean(y, axis=-1, keepdims=True)
            yc = y - mu
            var = jnp.mean(yc * yc, axis=-1, keepdims=True)
            yn = yc * lax.rsqrt(var + EPS) * gn
            o_ref[rows, v_sl] = (yn * z_ref[rows, v_sl].astype(F32)).astype(BF16)


def _retention(proj, gn_g, batch, seq):
    m = proj.shape[0]
    tq = RET_TQ
    nt = seq // tq
    row = lambda b, t: b * nt + t
    return pl.pallas_call(
        _ret_kernel,
        out_shape=jax.ShapeDtypeStruct((m, RET_WIDTH), BF16),
        grid=(batch, nt),
        in_specs=[
            pl.BlockSpec((tq, RET_QK_WIDTH), lambda b, t: (row(b, t), OFF_RQ // RET_QK_WIDTH)),
            pl.BlockSpec((tq, RET_QK_WIDTH), lambda b, t: (row(b, t), OFF_RK // RET_QK_WIDTH)),
            pl.BlockSpec((tq, RET_WIDTH), lambda b, t: (row(b, t), OFF_RV // RET_WIDTH)),
            pl.BlockSpec((tq, RET_WIDTH), lambda b, t: (row(b, t), OFF_Z // RET_WIDTH)),
            pl.BlockSpec((RET_HEADS, RET_DV), lambda b, t: (0, 0)),
        ],
        out_specs=pl.BlockSpec((tq, RET_WIDTH), lambda b, t: (row(b, t), 0)),
        scratch_shapes=[pltpu.VMEM((RET_HEADS, RET_DK, RET_DV), F32),
                        pltpu.VMEM((RET_HEADS, RET_CHUNK, RET_CHUNK), F32)],
        compiler_params=pltpu.CompilerParams(
            dimension_semantics=("arbitrary", "arbitrary"), vmem_limit_bytes=VMEM_LIMIT_BYTES),
        name="retention",
    )(proj, proj, proj, proj, gn_g)


def _att_kernel(*refs, n_cast):
    slopes_ref, q_ref, k_ref, v_ref, z_ref = refs[:5]
    cast_in, o_ref, cast_out = refs[5:5 + n_cast], refs[5 + n_cast], refs[6 + n_cast:6 + 2 * n_cast]
    (stage_q, stage_k, stage_v, q4, k4, v4, qp4, qp16, kp1, kp4, kp16, vp1, vp4, vp16, bias_ref, s_ref, pn_ref,
     o0, o1, o2, l0, l1, l2) = refs[6 + 2 * n_cast:]
    for src, dst in zip(cast_in, cast_out):
        dst[...] = src[...].astype(BF16)
    blk = ATT_BLOCK
    grp = ATT_GROUP
    seq = q_ref.shape[0]
    nblk = seq // blk
    n_it = nblk // grp
    d4, d16 = DILATED_PATTERNS[1][1], DILATED_PATTERNS[2][1]
    assert DILATED_PATTERNS[0][1] == 1 and grp == d4 and d16 == d4 * d4
    assert n_it % 2 == 0 and n_it >= 4 and nblk % d16 == 0
    slope = slopes_ref[pl.program_id(1)]
    nt = (((1,), (1,)), ((), ()))

    def natural_rows(bi, dil):
        nb = nblk // dil
        start = (bi // nb) + (bi % nb) * (blk * dil)
        if dil == 1:
            return pl.ds(pl.multiple_of(start, blk), blk)
        return pl.ds(start, blk, stride=dil)

    def block_rows(bi, lead=0):
        return pl.ds(pl.multiple_of(bi * blk + lead, blk), blk)

    for pad in (kp1, kp4, kp16, vp1, vp4, vp16):
        pad[0:blk, :] = jnp.zeros((blk, ATT_HD), BF16)
    kp1[blk:, :] = k_ref[...]
    vp1[blk:, :] = v_ref[...]

    def copy_4(it):
        rows = pl.ds(pl.multiple_of(it * (grp * blk), grp * blk), grp * blk)
        stage_q[...] = q_ref[rows, :].astype(F32)
        stage_k[...] = k_ref[rows, :].astype(F32)
        stage_v[...] = v_ref[rows, :].astype(F32)
        for r in range(d4):
            bi = r * (nblk // d4) + it
            src = pl.ds(r, blk, stride=d4)
            q, k, v = stage_q[src, :], stage_k[src, :], stage_v[src, :]
            q4[block_rows(bi), :], k4[block_rows(bi), :], v4[block_rows(bi), :] = q, k, v
            qp4[block_rows(bi), :] = q.astype(BF16)
            kp4[block_rows(bi, blk), :] = k.astype(BF16)
            vp4[block_rows(bi, blk), :] = v.astype(BF16)

    def copy_16(it):
        halves = (nblk // d4) // grp
        r4, n16 = it // halves, it % halves
        for c in range(d4):
            bi = (r4 + d4 * c) * (nblk // d16) + n16
            src = pl.ds(r4 * (seq // d4) + n16 * (grp * blk) + c, blk, stride=d4)
            qp16[block_rows(bi), :] = q4[src, :].astype(BF16)
            kp16[block_rows(bi, blk), :] = k4[src, :].astype(BF16)
            vp16[block_rows(bi, blk), :] = v4[src, :].astype(BF16)

    qi = lax.broadcasted_iota(jnp.int32, (blk, 2 * blk), 0)
    kj = lax.broadcasted_iota(jnp.int32, (blk, 2 * blk), 1)
    rel = qi - kj + blk
    for pi, (window, dil) in enumerate(DILATED_PATTERNS):
        valid = (rel >= 0) & (rel <= window // dil)
        alibi = -slope * (rel * dil).astype(F32)
        bias_ref[pi, 0] = jnp.where(valid & (kj >= blk), alibi, NEG)
        bias_ref[pi, 1] = jnp.where(valid, alibi, NEG)

    o16p, l16p = q4, k4

    def rows_in_d4_order(bi16):
        r16, n16 = bi16 // (nblk // d16), bi16 % (nblk // d16)
        return pl.ds((r16 % d4) * (seq // d4) + n16 * (grp * blk) + r16 // d4, blk, stride=d4)

    def un_permute(it):
        for g in range(grp):
            bi4 = it * grp + g
            o2[natural_rows(bi4, d4), :] = o16p[block_rows(bi4), :]
            l2[natural_rows(bi4, d4), :] = l16p[block_rows(bi4), :]

    def make_stages(pi, dil, q_src, kp, vp, o_g, l_g, copy_next):
        nb = nblk // dil
        two_step = dil == d16
        block_of = (lambda it, g: it + n_it * g) if two_step else (lambda it, g: it * grp + g)
        out_rows = rows_in_d4_order if two_step else (lambda bi: natural_rows(bi, dil))
        o_dst, l_dst = (o16p, l16p) if two_step else (o_g, l_g)

        def scores(it):
            for g in range(grp):
                bi = block_of(it, g)
                kb = kp[pl.ds(pl.multiple_of(bi * blk, blk), 2 * blk), :]
                s_ref[it % 2, g] = lax.dot_general(q_src[block_rows(bi), :], kb, nt, preferred_element_type=F32)
            if copy_next is not None:
                copy_next(it)

        def probs(it):
            for g in range(grp):
                bi = block_of(it, g)
                s = s_ref[it % 2, g] + bias_ref[pi, jnp.minimum(bi % nb, 1)]
                mx = jnp.max(jnp.maximum(s[:, :blk], s[:, blk:]), axis=-1, keepdims=True)
                pr = jnp.exp(s - mx)
                den = jnp.sum(pr[:, :blk] + pr[:, blk:], axis=-1, keepdims=True)
                pn_ref[it % 2, g] = (pr * (1.0 / den)).astype(BF16)
                l_dst[out_rows(bi), :] = jnp.broadcast_to(mx + jnp.log(den), (blk, ATT_HD))

        def values(it):
            for g in range(grp):
                bi = block_of(it, g)
                vb = vp[pl.ds(pl.multiple_of(bi * blk, blk), 2 * blk), :]
                o_dst[out_rows(bi), :] = jnp.dot(pn_ref[it % 2, g], vb, preferred_element_type=F32)
            if two_step:
                un_permute(it)

        return scores, probs, values

    stages = [make_stages(pi, dil, *refs) for pi, ((_, dil), *refs) in enumerate(zip(
        DILATED_PATTERNS, (q_ref, qp4, qp16), (kp1, kp4, kp16), (vp1, vp4, vp16), (o0, o1, o2), (l0, l1, l2),
        (copy_4, copy_16, None)))]

    def fill(scores, probs, values):
        scores(0)
        probs(0)
        scores(1)

    def drain(scores, probs, values):
        values(n_it - 2)
        probs(n_it - 1)
        values(n_it - 1)

    fill(*stages[0])
    for pi, (scores, probs, values) in enumerate(stages):
        def body(it, carry, scores=scores, probs=probs, values=values):
            values(it - 2)
            probs(it - 1)
            scores(it)
            return carry

        lax.fori_loop(2, n_it, body, 0)
        drain(scores, probs, values)
        if pi + 1 < len(stages):
            fill(*stages[pi + 1])

    mrows = ATT_MERGE_ROWS

    def merge(ci, carry):
        rows = pl.ds(pl.multiple_of(ci * mrows, mrows), mrows)
        a0, a1, a2 = l0[rows, :], l1[rows, :], l2[rows, :]
        mx = jnp.maximum(jnp.maximum(a0, a1), a2)
        e0, e1, e2 = jnp.exp(a0 - mx), jnp.exp(a1 - mx), jnp.exp(a2 - mx)
        y = (e0 * o0[rows, :] + e1 * o1[rows, :] + e2 * o2[rows, :]) / (e0 + e1 + e2)
        o_ref[rows, :] = (y * z_ref[rows, :].astype(F32)).astype(BF16)
        return carry

    lax.fori_loop(0, seq // mrows, merge, 0)


def _attention(proj, slopes, batch, seq, to_cast):
    m = proj.shape[0]
    hd = ATT_HD
    n_steps = batch * ATT_HEADS
    spec = lambda off: pl.BlockSpec((seq, hd), lambda b, h: (b, off // hd + h))
    slab = lambda a: pl.BlockSpec((a.shape[0] // n_steps, a.shape[1]), lambda b, h: (b * ATT_HEADS + h, 0))
    assert all(a.shape[0] % (BF16_ROWS * n_steps) == 0 for a in to_cast)
    f32_rows = pltpu.VMEM((seq, hd), F32)
    bf16_rows = pltpu.VMEM((seq, hd), BF16)
    bf16_padded = pltpu.VMEM((seq + ATT_BLOCK, hd), BF16)
    f32_group = pltpu.VMEM((ATT_GROUP * ATT_BLOCK, hd), F32)
    y_att, *cast = pl.pallas_call(
        functools.partial(_att_kernel, n_cast=len(to_cast)),
        out_shape=[jax.ShapeDtypeStruct((m, ATT_WIDTH), BF16)]
        + [jax.ShapeDtypeStruct(a.shape, BF16) for a in to_cast],
        grid=(batch, ATT_HEADS),
        in_specs=[
            pl.BlockSpec(memory_space=pltpu.SMEM),
            spec(OFF_AQ), spec(OFF_AK), spec(OFF_AV), spec(OFF_Z + RET_WIDTH),
        ] + [slab(a) for a in to_cast],
        out_specs=[pl.BlockSpec((seq, hd), lambda b, h: (b, h))] + [slab(a) for a in to_cast],
        scratch_shapes=[
            f32_group, f32_group, f32_group, f32_rows, f32_rows, f32_rows,
            bf16_rows, bf16_rows,
            bf16_padded, bf16_padded, bf16_padded, bf16_padded, bf16_padded, bf16_padded,
            pltpu.VMEM((len(DILATED_PATTERNS), 2, ATT_BLOCK, 2 * ATT_BLOCK), F32),
            pltpu.VMEM((2, ATT_GROUP, ATT_BLOCK, 2 * ATT_BLOCK), F32),
            pltpu.VMEM((2, ATT_GROUP, ATT_BLOCK, 2 * ATT_BLOCK), BF16),
            f32_rows, f32_rows, f32_rows, f32_rows, f32_rows, f32_rows,
        ],
        compiler_params=pltpu.CompilerParams(
            dimension_semantics=("arbitrary", "arbitrary"), vmem_limit_bytes=VMEM_LIMIT_BYTES),
        name="dilated_attention",
    )(slopes, proj, proj, proj, proj, *to_cast)
    return y_att, cast


def _post_tail(x1, p_ref, pg_ref, wg_ref, wp_ref, o_ref):
    h2 = _rms_rows(x1, pg_ref[...]).astype(BF16)
    pb = p_ref[...].astype(BF16)
    for cc in range(D_MODEL // POST_NC):
        sl = slice(cc * POST_NC, (cc + 1) * POST_NC)
        gate = jax.nn.sigmoid(jnp.dot(h2, wg_ref[:, sl], preferred_element_type=F32))
        pp = jnp.dot(pb, wp_ref[:, sl], preferred_element_type=F32)
        o_ref[:, sl] = x1[:, sl] + gate * pp


def _post_kernel(x_ref, ya_ref, yb_ref, p_ref, wa_ref, wb_ref, pg_ref, wg_ref, wp_ref, o_ref):
    x1 = x_ref[...] + jnp.dot(ya_ref[...], wa_ref[...], preferred_element_type=F32)
    x1 = x1 + jnp.dot(yb_ref[...], wb_ref[...], preferred_element_type=F32)
    _post_tail(x1, p_ref, pg_ref, wg_ref, wp_ref, o_ref)


def _const_spec(shape):
    return pl.BlockSpec(shape, lambda i: (0,) * len(shape), pipeline_mode=pl.Buffered(1))


def _ple_rows_spec(layer, m, tm):
    return pl.BlockSpec((tm, PLE_DIM), lambda i: (layer * (m // tm) + i, 0))


def _post(x2, ya, ya_blk, yb, yb_blk, p_all, layer, w_out_bf, pg, wg_all_bf, wp_all_bf, name):
    m = x2.shape[0]
    tm = POST_TM
    half = D_MODEL // 2
    rows = lambda w, blk=0: pl.BlockSpec((tm, w), lambda i: (i, blk))
    layer_spec = lambda k: pl.BlockSpec((None, k, D_MODEL), lambda i: (layer, 0, 0), pipeline_mode=pl.Buffered(1))
    return pl.pallas_call(
        _post_kernel,
        out_shape=jax.ShapeDtypeStruct((m, D_MODEL), F32),
        grid=(m // tm,),
        in_specs=[
            rows(D_MODEL), rows(half, ya_blk), rows(half, yb_blk), _ple_rows_spec(layer, m, tm),
            pl.BlockSpec((half, D_MODEL), lambda i: (0, 0), pipeline_mode=pl.Buffered(1)),
            pl.BlockSpec((half, D_MODEL), lambda i: (1, 0), pipeline_mode=pl.Buffered(1)),
            _const_spec((1, D_MODEL)), layer_spec(D_MODEL), layer_spec(PLE_DIM),
        ],
        out_specs=rows(D_MODEL),
        compiler_params=pltpu.CompilerParams(
            dimension_semantics=("arbitrary",), vmem_limit_bytes=VMEM_LIMIT_BYTES),
        name=name,
    )(x2, ya, yb, p_all, w_out_bf, w_out_bf, pg, wg_all_bf, wp_all_bf)


def _in_odd_kernel(x_ref, g_ref, wb_ref, wc_ref, wu_ref, wz_ref, cw_ref, y_ref, hn_ref, ext_ref, halo_ref, *,
                   tiles_per_seq):
    i, j = pl.program_id(0), pl.program_id(1)
    tm = x_ref.shape[0]

    @pl.when(j == 0)
    def _():
        hn_ref[...] = _rms_rows(x_ref[...], g_ref[...]).astype(BF16)

    @pl.when((i == 0) & (j == 0))
    def _():
        halo_ref[...] = jnp.zeros(halo_ref.shape, F32)

    hn = hn_ref[...]
    cu = (jnp.dot(hn, wc_ref[...], preferred_element_type=F32)
          * jnp.dot(hn, wu_ref[...], preferred_element_type=F32))
    first = (i % tiles_per_seq) == 0
    ext_ref[0:HALO, :] = jnp.where(first, 0.0, halo_ref[j])
    ext_ref[HALO:HALO + tm, :] = cu
    halo_ref[j] = cu[tm - HALO:, :]
    conv = (cw_ref[0:1, :] * ext_ref[HALO - 2:HALO - 2 + tm, :]
            + cw_ref[1:2, :] * ext_ref[HALO - 1:HALO - 1 + tm, :]
            + cw_ref[2:3, :] * cu)
    bg = jnp.dot(hn, wb_ref[...], preferred_element_type=F32)
    z = jnp.dot(hn, wz_ref[...], preferred_element_type=F32)
    y_ref[...] = (bg * conv * _silu(z)).astype(BF16)


def _in_odd(x2, g, w_bf, conv_w, seq):
    m = x2.shape[0]
    tm, tc = IN_TM, IN_ODD_TC
    nb = D_MODEL // tc
    wspec = lambda seg: pl.BlockSpec((D_MODEL, tc), lambda i, j: (0, seg * nb + j))
    return pl.pallas_call(
        functools.partial(_in_odd_kernel, tiles_per_seq=seq // tm),
        out_shape=jax.ShapeDtypeStruct((m, D_MODEL), BF16),
        grid=(m // tm, nb),
        in_specs=[
            pl.BlockSpec((tm, D_MODEL), lambda i, j: (i, 0)),
            pl.BlockSpec((1, D_MODEL), lambda i, j: (0, 0)),
            wspec(0), wspec(1), wspec(2), wspec(3),
            pl.BlockSpec((CONV_WIDTH, tc), lambda i, j: (0, j)),
        ],
        out_specs=pl.BlockSpec((tm, tc), lambda i, j: (i, j)),
        scratch_shapes=[
            pltpu.VMEM((tm, D_MODEL), BF16),
            pltpu.VMEM((HALO + tm, tc), F32),
            pltpu.VMEM((nb, HALO, tc), F32),
        ],
        compiler_params=pltpu.CompilerParams(
            dimension_semantics=("arbitrary", "arbitrary"), vmem_limit_bytes=VMEM_LIMIT_BYTES),
        name="in_odd",
    )(x2, g, w_bf, w_bf, w_bf, w_bf, conv_w)


def _even_gain_row(q_g, k_g):
    return jnp.concatenate([
        jnp.ones((OFF_AQ,), F32),
        jnp.tile(q_g * ATT_HD ** -0.5, ATT_HEADS),
        jnp.tile(k_g, ATT_HEADS),
        jnp.ones((IN_WIDTH_EVEN - OFF_Z,), F32),
    ]).reshape(1, IN_WIDTH_EVEN)

def kernel(x, p, pre_norm_g, w_in_even, q_norm_g, k_norm_g, ret_gn_g, w_out_even, w_in_odd, conv_w_odd,
           w_out_odd, ple_norm_g, w_ple_gate, w_ple_proj):
    batch, seq, d = x.shape
    depth = p.shape[0]
    assert d == D_MODEL and seq % (ATT_BLOCK * DILATED_PATTERNS[-1][1]) == 0 and seq % IN_TM == 0
    m = batch * seq
    slopes = jnp.asarray([(2.0 ** (-8.0 / ATT_HEADS)) ** (i + 1) for i in range(ATT_HEADS)], F32)
    x2 = x.reshape(m, d)
    p_all = p.reshape(depth * m, PLE_DIM)
    late_f32 = dict(w_out_even=w_out_even, w_ple_gate=w_ple_gate, w_ple_proj=w_ple_proj, w_in_odd=w_in_odd,
                    w_out_odd=w_out_odd)
    if w_in_even.shape[0] > 1:
        late_f32["w_in_even_rest"] = w_in_even[1:]
    bf = None
    for i in range(depth):
        j = i // 2
        pg = ple_norm_g[i].reshape(1, d)
        g = pre_norm_g[i].reshape(1, d)
        if i % 2 == 0:
            w_in_bf = w_in_even[0].astype(BF16) if j == 0 else bf["w_in_even_rest"][j - 1]
            proj = _in_even(x2, g, w_in_bf, _even_gain_row(q_norm_g[j], k_norm_g[j]))
            y_ret = _retention(proj, ret_gn_g[j], batch, seq)
            to_cast = [a.reshape(-1, a.shape[-1]) for a in late_f32.values()] if bf is None else []
            y_att, cast = _attention(proj, slopes, batch, seq, to_cast)
            if bf is None:
                bf = {name: c.reshape(a.shape) for (name, a), c in zip(late_f32.items(), cast)}
            x2 = _post(x2, y_ret, 0, y_att, 0, p_all, i, bf["w_out_even"][j], pg, bf["w_ple_gate"],
                       bf["w_ple_proj"], "post_even")
        else:
            y = _in_odd(x2, g, bf["w_in_odd"][j], conv_w_odd[j], seq)
            x2 = _post(x2, y, 0, y, 1, p_all, i, bf["w_out_odd"][j], pg, bf["w_ple_gate"], bf["w_ple_proj"],
                       "post_odd")
    return x2.reshape(batch, seq, d)
```

```python
import functools
import math

import jax
import jax.numpy as jnp
from jax import lax
from jax.experimental import pallas as pl
from jax.experimental.pallas import tpu as pltpu

F32 = jnp.float32
BF16 = jnp.bfloat16

D_MODEL = 2048
PLE_DIM = 256
EPS = 1e-6
NEG = -1e30
RET_DK = 128
RET_DV = 256
RET_HEADS = 4
RET_CHUNK = 256
ATT_HD = 128
ATT_HEADS = 8
ATT_BLOCK = 128
DILATED_PATTERNS = ((128, 1), (512, 4), (2048, 16))
RET_QK_WIDTH = RET_HEADS * RET_DK
RET_WIDTH = RET_HEADS * RET_DV
ATT_WIDTH = ATT_HEADS * ATT_HD
IN_WIDTH_EVEN = 2 * RET_QK_WIDTH + RET_WIDTH + 3 * ATT_WIDTH + RET_WIDTH + ATT_WIDTH
CONV_WIDTH = 3

REF_AQ = 2 * RET_QK_WIDTH + RET_WIDTH
REF_AV = REF_AQ + 2 * ATT_WIDTH
OFF_RQ = 0
OFF_RK = RET_QK_WIDTH
OFF_RV = 2 * RET_QK_WIDTH
OFF_AV = OFF_RV + RET_WIDTH
OFF_AQ = OFF_AV + ATT_WIDTH
OFF_AK = OFF_AQ + ATT_WIDTH
OFF_Z = OFF_AK + ATT_WIDTH

VMEM_LIMIT_BYTES = 56 * 1024 * 1024
BF16_ROWS = 16

IN_TM = 1024
IN_EVEN_TN = 1792
IN_EVEN_NC = 256
IN_EVEN_LAST_ROW_SPLIT = 4
IN_ODD_TC = 256
RET_TQ = 1024
ATT_GROUP = 4
ATT_MERGE_ROWS = 512
POST_TM = 512
POST_NC = 256
HALO = 8


def _rms_rows(x, g):
    ms = jnp.mean(x * x, axis=-1, keepdims=True)
    return x * lax.rsqrt(ms + EPS) * g


def _silu(z):
    return z * jax.nn.sigmoid(z)


def _in_even_kernel(x_ref, g_ref, gain_ref, *refs):
    *w_refs, o_ref, hn_ref = refs
    j = pl.program_id(1)

    @pl.when(j == 0)
    def _():
        hn_ref[...] = _rms_rows(x_ref[...], g_ref[...]).astype(BF16)

    tm, tn = o_ref.shape
    n_chunks = tn // IN_EVEN_NC

    def tile(jj):
        for cc in range(n_chunks):
            col0 = jj * tn + cc * IN_EVEN_NC
            n_row = IN_EVEN_LAST_ROW_SPLIT if cc == n_chunks - 1 else 1
            for rr in range(n_row):
                rows = slice(rr * (tm // n_row), (rr + 1) * (tm // n_row))
                acc = jnp.dot(hn_ref[rows, :], w_refs[cc][...], preferred_element_type=F32)
                for hh in range(IN_EVEN_NC // ATT_HD):
                    sl = slice(cc * IN_EVEN_NC + hh * ATT_HD, cc * IN_EVEN_NC + (hh + 1) * ATT_HD)
                    a = acc[:, hh * ATT_HD:(hh + 1) * ATT_HD]
                    if col0 >= OFF_Z:
                        a = _silu(a)
                    elif col0 >= OFF_AQ:
                        a = _rms_rows(a, gain_ref[:, sl])
                    o_ref[rows, sl] = a.astype(BF16)

    for jj in range(IN_WIDTH_EVEN // tn):
        pl.when(j == jj)(functools.partial(tile, jj))


def _even_weight_chunk_spec(cc, chunks_per_tile):
    nc = IN_EVEN_NC

    def index(i, j):
        k = j * chunks_per_tile + cc
        in_av = (k >= OFF_AV // nc) & (k < OFF_AQ // nc)
        in_qk = (k >= OFF_AQ // nc) & (k < OFF_Z // nc)
        ref = jnp.where(in_av, k + (REF_AV - OFF_AV) // nc, jnp.where(in_qk, k - (OFF_AQ - REF_AQ) // nc, k))
        return (0, ref)

    return pl.BlockSpec((D_MODEL, nc), index)


def _in_even(x2, g, w_bf, gain_row):
    m = x2.shape[0]
    tm, tn = IN_TM, IN_EVEN_TN
    n_chunks = tn // IN_EVEN_NC
    return pl.pallas_call(
        _in_even_kernel,
        out_shape=jax.ShapeDtypeStruct((m, IN_WIDTH_EVEN), BF16),
        grid=(m // tm, IN_WIDTH_EVEN // tn),
        in_specs=[
            pl.BlockSpec((tm, D_MODEL), lambda i, j: (i, 0)),
            pl.BlockSpec((1, D_MODEL), lambda i, j: (0, 0)),
            pl.BlockSpec((1, tn), lambda i, j: (0, j)),
            *[_even_weight_chunk_spec(cc, n_chunks) for cc in range(n_chunks)],
        ],
        out_specs=pl.BlockSpec((tm, tn), lambda i, j: (i, j)),
        scratch_shapes=[pltpu.VMEM((tm, D_MODEL), BF16)],
        compiler_params=pltpu.CompilerParams(
            dimension_semantics=("arbitrary", "arbitrary"), vmem_limit_bytes=VMEM_LIMIT_BYTES),
        name="in_even",
    )(x2, g, gain_row, *([w_bf] * n_chunks))


def _ret_kernel(q_ref, k_ref, v_ref, z_ref, gn_ref, o_ref, r_ref, decay_ref):
    c = RET_CHUNK
    tq = q_ref.shape[0]
    kscale = RET_DK ** -0.5
    log_g = [math.log(1.0 - 2.0 ** (-5.0 - h)) for h in range(RET_HEADS)]

    @pl.when(pl.program_id(1) == 0)
    def _():
        r_ref[...] = jnp.zeros(r_ref.shape, F32)

    @pl.when((pl.program_id(0) == 0) & (pl.program_id(1) == 0))
    def _():
        row = lax.broadcasted_iota(jnp.int32, (c, c), 0)
        col = lax.broadcasted_iota(jnp.int32, (c, c), 1)
        diff = (row - col).astype(F32)
        for h in range(RET_HEADS):
            decay_ref[h] = jnp.where(diff >= 0, jnp.exp(log_g[h] * jnp.maximum(diff, 0.0)), 0.0) * kscale

    pos = lax.broadcasted_iota(jnp.int32, (c, 1), 0).astype(F32)

    for h in range(RET_HEADS):
        lg = log_g[h]
        decay = decay_ref[h]
        xi = jnp.exp(lg * (pos + 1.0))
        zeta = jnp.exp(lg * (c - 1.0 - pos)) * kscale
        g_chunk = math.exp(lg * c)
        qk_sl = slice(h * RET_DK, (h + 1) * RET_DK)
        v_sl = slice(h * RET_DV, (h + 1) * RET_DV)
        gn = gn_ref[h:h + 1, :]
        chunks = [slice(ci * c, (ci + 1) * c) for ci in range(tq // c)]

        contrib = []
        for rows in chunks:
            kz_t = (k_ref[rows, qk_sl].astype(F32) * zeta).T.astype(BF16)
            contrib.append(jnp.dot(kz_t, v_ref[rows, v_sl], preferred_element_type=F32))

        state = r_ref[h]
        states = []
        for s_c in contrib:
            states.append(state)
            state = g_chunk * state + s_c
        r_ref[h] = state

        for rows, r_prev in zip(chunks, states):
            q = q_ref[rows, qk_sl]
            v = v_ref[rows, v_sl]
            s = lax.dot_general(q, k_ref[rows, qk_sl], (((1,), (1,)), ((), ())),
                                preferred_element_type=F32) * decay
            y = jnp.dot(s.astype(BF16), v, preferred_element_type=F32)
            y = y + jnp.dot(q, r_prev.astype(BF16), preferred_element_type=F32) * xi
            mu = jnp.mean(y, axis=-1, keepdims=True)
            yc = y - mu
            var = jnp.mean(yc * yc, axis=-1, keepdims=True)
            yn = yc * lax.rsqrt(var + EPS) * gn
            o_ref[rows, v_sl] = (yn * z_ref[rows, v_sl].astype(F32)).astype(BF16)


def _retention(proj, gn_g, batch, seq):
    m = proj.shape[0]
    tq = RET_TQ
    nt = seq // tq
    row = lambda b, t: b * nt + t
    return pl.pallas_call(
        _ret_kernel,
        out_shape=jax.ShapeDtypeStruct((m, RET_WIDTH), BF16),
        grid=(batch, nt),
        in_specs=[
            pl.BlockSpec((tq, RET_QK_WIDTH), lambda b, t: (row(b, t), OFF_RQ // RET_QK_WIDTH)),
            pl.BlockSpec((tq, RET_QK_WIDTH), lambda b, t: (row(b, t), OFF_RK // RET_QK_WIDTH)),
            pl.BlockSpec((tq, RET_WIDTH), lambda b, t: (row(b, t), OFF_RV // RET_WIDTH)),
            pl.BlockSpec((tq, RET_WIDTH), lambda b, t: (row(b, t), OFF_Z // RET_WIDTH)),
            pl.BlockSpec((RET_HEADS, RET_DV), lambda b, t: (0, 0)),
        ],
        out_specs=pl.BlockSpec((tq, RET_WIDTH), lambda b, t: (row(b, t), 0)),
        scratch_shapes=[pltpu.VMEM((RET_HEADS, RET_DK, RET_DV), F32),
                        pltpu.VMEM((RET_HEADS, RET_CHUNK, RET_CHUNK), F32)],
        compiler_params=pltpu.CompilerParams(
            dimension_semantics=("arbitrary", "arbitrary"), vmem_limit_bytes=VMEM_LIMIT_BYTES),
        name="retention",
    )(proj, proj, proj, proj, gn_g)


def _att_kernel(*refs, n_cast):
    slopes_ref, q_ref, k_ref, v_ref, z_ref = refs[:5]
    cast_in, o_ref, cast_out = refs[5:5 + n_cast], refs[5 + n_cast], refs[6 + n_cast:6 + 2 * n_cast]
    (stage_q, stage_k, stage_v, q4, k4, v4, qp4, qp16, kp1, kp4, kp16, vp1, vp4, vp16, bias_ref, s_ref, pn_ref,
     o0, o1, o2, l0, l1, l2) = refs[6 + 2 * n_cast:]
    for src, dst in zip(cast_in, cast_out):
        dst[...] = src[...].astype(BF16)
    blk = ATT_BLOCK
    grp = ATT_GROUP
    seq = q_ref.shape[0]
    nblk = seq // blk
    n_it = nblk // grp
    d4, d16 = DILATED_PATTERNS[1][1], DILATED_PATTERNS[2][1]
    assert DILATED_PATTERNS[0][1] == 1 and grp == d4 and d16 == d4 * d4
    assert n_it % 2 == 0 and n_it >= 4 and nblk % d16 == 0
    slope = slopes_ref[pl.program_id(1)]
    nt = (((1,), (1,)), ((), ()))

    def natural_rows(bi, dil):
        nb = nblk // dil
        start = (bi // nb) + (bi % nb) * (blk * dil)
        if dil == 1:
            return pl.ds(pl.multiple_of(start, blk), blk)
        return pl.ds(start, blk, stride=dil)

    def block_rows(bi, lead=0):
        return pl.ds(pl.multiple_of(bi * blk + lead, blk), blk)

    for pad in (kp1, kp4, kp16, vp1, vp4, vp16):
        pad[0:blk, :] = jnp.zeros((blk, ATT_HD), BF16)
    kp1[blk:, :] = k_ref[...]
    vp1[blk:, :] = v_ref[...]

    def copy_4(it):
        rows = pl.ds(pl.multiple_of(it * (grp * blk), grp * blk), grp * blk)
        stage_q[...] = q_ref[rows, :].astype(F32)
        stage_k[...] = k_ref[rows, :].astype(F32)
        stage_v[...] = v_ref[rows, :].astype(F32)
        for r in range(d4):
            bi = r * (nblk // d4) + it
            src = pl.ds(r, blk, stride=d4)
            q, k, v = stage_q[src, :], stage_k[src, :], stage_v[src, :]
            q4[block_rows(bi), :], k4[block_rows(bi), :], v4[block_rows(bi), :] = q, k, v
            qp4[block_rows(bi), :] = q.astype(BF16)
            kp4[block_rows(bi, blk), :] = k.astype(BF16)
            vp4[block_rows(bi, blk), :] = v.astype(BF16)

    def copy_16(it):
        halves = (nblk // d4) // grp
        r4, n16 = it // halves, it % halves
        for c in range(d4):
            bi = (r4 + d4 * c) * (nblk // d16) + n16
            src = pl.ds(r4 * (seq // d4) + n16 * (grp * blk) + c, blk, stride=d4)
            qp16[block_rows(bi), :] = q4[src, :].astype(BF16)
            kp16[block_rows(bi, blk), :] = k4[src, :].astype(BF16)
            vp16[block_rows(bi, blk), :] = v4[src, :].astype(BF16)

    qi = lax.broadcasted_iota(jnp.int32, (blk, 2 * blk), 0)
    kj = lax.broadcasted_iota(jnp.int32, (blk, 2 * blk), 1)
    rel = qi - kj + blk
    for pi, (window, dil) in enumerate(DILATED_PATTERNS):
        valid = (rel >= 0) & (rel <= window // dil)
        alibi = -slope * (rel * dil).astype(F32)
        bias_ref[pi, 0] = jnp.where(valid & (kj >= blk), alibi, NEG)
        bias_ref[pi, 1] = jnp.where(valid, alibi, NEG)

    o16p, l16p = q4, k4

    def rows_in_d4_order(bi16):
        r16, n16 = bi16 // (nblk // d16), bi16 % (nblk // d16)
        return pl.ds((r16 % d4) * (seq // d4) + n16 * (grp * blk) + r16 // d4, blk, stride=d4)

    def un_permute(it):
        for g in range(grp):
            bi4 = it * grp + g
            o2[natural_rows(bi4, d4), :] = o16p[block_rows(bi4), :]
            l2[natural_rows(bi4, d4), :] = l16p[block_rows(bi4), :]

    def make_stages(pi, dil, q_src, kp, vp, o_g, l_g, copy_next):
        nb = nblk // dil
        two_step = dil == d16
        block_of = (lambda it, g: it + n_it * g) if two_step else (lambda it, g: it * grp + g)
        out_rows = rows_in_d4_order if two_step else (lambda bi: natural_rows(bi, dil))
        o_dst, l_dst = (o16p, l16p) if two_step else (o_g, l_g)

        def scores(it):
            for g in range(grp):
                bi = block_of(it, g)
                kb = kp[pl.ds(pl.multiple_of(bi * blk, blk), 2 * blk), :]
                s_ref[it % 2, g] = lax.dot_general(q_src[block_rows(bi), :], kb, nt, preferred_element_type=F32)
            if copy_next is not None:
                copy_next(it)

        def probs(it):
            for g in range(grp):
                bi = block_of(it, g)
                s = s_ref[it % 2, g] + bias_ref[pi, jnp.minimum(bi % nb, 1)]
                mx = jnp.max(jnp.maximum(s[:, :blk], s[:, blk:]), axis=-1, keepdims=True)
                pr = jnp.exp(s - mx)
                den = jnp.sum(pr[:, :blk] + pr[:, blk:], axis=-1, keepdims=True)
                pn_ref[it % 2, g] = (pr * (1.0 / den)).astype(BF16)
                l_dst[out_rows(bi), :] = jnp.broadcast_to(mx + jnp.log(den), (blk, ATT_HD))

        def values(it):
            for g in range(grp):
                bi = block_of(it, g)
                vb = vp[pl.ds(pl.multiple_of(bi * blk, blk), 2 * blk), :]
                o_dst[out_rows(bi), :] = jnp.dot(pn_ref[it % 2, g], vb, preferred_element_type=F32)
            if two_step:
                un_permute(it)

        return scores, probs, values

    stages = [make_stages(pi, dil, *refs) for pi, ((_, dil), *refs) in enumerate(zip(
        DILATED_PATTERNS, (q_ref, qp4, qp16), (kp1, kp4, kp16), (vp1, vp4, vp16), (o0, o1, o2), (l0, l1, l2),
        (copy_4, copy_16, None)))]

    def fill(scores, probs, values):
        scores(0)
        probs(0)
        scores(1)

    def drain(scores, probs, values):
        values(n_it - 2)
        probs(n_it - 1)
        values(n_it - 1)

    fill(*stages[0])
    for pi, (scores, probs, values) in enumerate(stages):
        def body(it, carry, scores=scores, probs=probs, values=values):
            values(it - 2)
            probs(it - 1)
            scores(it)
            return carry

        lax.fori_loop(2, n_it, body, 0)
        drain(scores, probs, values)
        if pi + 1 < len(stages):
            fill(*stages[pi + 1])

    mrows = ATT_MERGE_ROWS

    def merge(ci, carry):
        rows = pl.ds(pl.multiple_of(ci * mrows, mrows), mrows)
        a0, a1, a2 = l0[rows, :], l1[rows, :], l2[rows, :]
        mx = jnp.maximum(jnp.maximum(a0, a1), a2)
        e0, e1, e2 = jnp.exp(a0 - mx), jnp.exp(a1 - mx), jnp.exp(a2 - mx)
        y = (e0 * o0[rows, :] + e1 * o1[rows, :] + e2 * o2[rows, :]) / (e0 + e1 + e2)
        o_ref[rows, :] = (y * z_ref[rows, :].astype(F32)).astype(BF16)
        return carry

    lax.fori_loop(0, seq // mrows, merge, 0)


def _attention(proj, slopes, batch, seq, to_cast):
    m = proj.shape[0]
    hd = ATT_HD
    n_steps = batch * ATT_HEADS
    spec = lambda off: pl.BlockSpec((seq, hd), lambda b, h: (b, off // hd + h))
    slab = lambda a: pl.BlockSpec((a.shape[0] // n_steps, a.shape[1]), lambda b, h: (b * ATT_HEADS + h, 0))
    assert all(a.shape[0] % (BF16_ROWS * n_steps) == 0 for a in to_cast)
    f32_rows = pltpu.VMEM((seq, hd), F32)
    bf16_rows = pltpu.VMEM((seq, hd), BF16)
    bf16_padded = pltpu.VMEM((seq + ATT_BLOCK, hd), BF16)
    f32_group = pltpu.VMEM((ATT_GROUP * ATT_BLOCK, hd), F32)
    y_att, *cast = pl.pallas_call(
        functools.partial(_att_kernel, n_cast=len(to_cast)),
        out_shape=[jax.ShapeDtypeStruct((m, ATT_WIDTH), BF16)]
        + [jax.ShapeDtypeStruct(a.shape, BF16) for a in to_cast],
        grid=(batch, ATT_HEADS),
        in_specs=[
            pl.BlockSpec(memory_space=pltpu.SMEM),
            spec(OFF_AQ), spec(OFF_AK), spec(OFF_AV), spec(OFF_Z + RET_WIDTH),
        ] + [slab(a) for a in to_cast],
        out_specs=[pl.BlockSpec((seq, hd), lambda b, h: (b, h))] + [slab(a) for a in to_cast],
        scratch_shapes=[
            f32_group, f32_group, f32_group, f32_rows, f32_rows, f32_rows,
            bf16_rows, bf16_rows,
            bf16_padded, bf16_padded, bf16_padded, bf16_padded, bf16_padded, bf16_padded,
            pltpu.VMEM((len(DILATED_PATTERNS), 2, ATT_BLOCK, 2 * ATT_BLOCK), F32),
            pltpu.VMEM((2, ATT_GROUP, ATT_BLOCK, 2 * ATT_BLOCK), F32),
            pltpu.VMEM((2, ATT_GROUP, ATT_BLOCK, 2 * ATT_BLOCK), BF16),
            f32_rows, f32_rows, f32_rows, f32_rows, f32_rows, f32_rows,
        ],
        compiler_params=pltpu.CompilerParams(
            dimension_semantics=("arbitrary", "arbitrary"), vmem_limit_bytes=VMEM_LIMIT_BYTES),
        name="dilated_attention",
    )(slopes, proj, proj, proj, proj, *to_cast)
    return y_att, cast


def _post_tail(x1, p_ref, pg_ref, wg_ref, wp_ref, o_ref):
    h2 = _rms_rows(x1, pg_ref[...]).astype(BF16)
    pb = p_ref[...].astype(BF16)
    for cc in range(D_MODEL // POST_NC):
        sl = slice(cc * POST_NC, (cc + 1) * POST_NC)
        gate = jax.nn.sigmoid(jnp.dot(h2, wg_ref[:, sl], preferred_element_type=F32))
        pp = jnp.dot(pb, wp_ref[:, sl], preferred_element_type=F32)
        o_ref[:, sl] = x1[:, sl] + gate * pp


def _post_kernel(x_ref, ya_ref, yb_ref, p_ref, wa_ref, wb_ref, pg_ref, wg_ref, wp_ref, o_ref):
    x1 = x_ref[...] + jnp.dot(ya_ref[...], wa_ref[...], preferred_element_type=F32)
    x1 = x1 + jnp.dot(yb_ref[...], wb_ref[...], preferred_element_type=F32)
    _post_tail(x1, p_ref, pg_ref, wg_ref, wp_ref, o_ref)


def _const_spec(shape):
    return pl.BlockSpec(shape, lambda i: (0,) * len(shape), pipeline_mode=pl.Buffered(1))


def _ple_rows_spec(layer, m, tm):
    return pl.BlockSpec((tm, PLE_DIM), lambda i: (layer * (m // tm) + i, 0))


def _post(x2, ya, ya_blk, yb, yb_blk, p_all, layer, w_out_bf, pg, wg_all_bf, wp_all_bf, name):
    m = x2.shape[0]
    tm = POST_TM
    half = D_MODEL // 2
    rows = lambda w, blk=0: pl.BlockSpec((tm, w), lambda i: (i, blk))
    layer_spec = lambda k: pl.BlockSpec((None, k, D_MODEL), lambda i: (layer, 0, 0), pipeline_mode=pl.Buffered(1))
    return pl.pallas_call(
        _post_kernel,
        out_shape=jax.ShapeDtypeStruct((m, D_MODEL), F32),
        grid=(m // tm,),
        in_specs=[
            rows(D_MODEL), rows(half, ya_blk), rows(half, yb_blk), _ple_rows_spec(layer, m, tm),
            pl.BlockSpec((half, D_MODEL), lambda i: (0, 0), pipeline_mode=pl.Buffered(1)),
            pl.BlockSpec((half, D_MODEL), lambda i: (1, 0), pipeline_mode=pl.Buffered(1)),
            _const_spec((1, D_MODEL)), layer_spec(D_MODEL), layer_spec(PLE_DIM),
        ],
        out_specs=rows(D_MODEL),
        compiler_params=pltpu.CompilerParams(
            dimension_semantics=("arbitrary",), vmem_limit_bytes=VMEM_LIMIT_BYTES),
        name=name,
    )(x2, ya, yb, p_all, w_out_bf, w_out_bf, pg, wg_all_bf, wp_all_bf)


def _in_odd_kernel(x_ref, g_ref, wb_ref, wc_ref, wu_ref, wz_ref, cw_ref, y_ref, hn_ref, ext_ref, halo_ref, *,
                   tiles_per_seq):
    i, j = pl.program_id(0), pl.program_id(1)
    tm = x_ref.shape[0]

    @pl.when((i == 0) & (j == 0))
    def _():
        halo_ref[...] = jnp.zeros(halo_ref.shape, F32)

    def tile(with_pre_norm):
        if with_pre_norm:
            hn_ref[...] = _rms_rows(x_ref[...], g_ref[...]).astype(BF16)
        hn = hn_ref[...]
        cu = (jnp.dot(hn, wc_ref[...], preferred_element_type=F32)
              * jnp.dot(hn, wu_ref[...], preferred_element_type=F32))
        first = (i % tiles_per_seq) == 0
        ext_ref[0:HALO, :] = jnp.where(first, 0.0, halo_ref[j])
        ext_ref[HALO:HALO + tm, :] = cu
        halo_ref[j] = cu[tm - HALO:, :]
        conv = (cw_ref[0:1, :] * ext_ref[HALO - 2:HALO - 2 + tm, :]
                + cw_ref[1:2, :] * ext_ref[HALO - 1:HALO - 1 + tm, :]
                + cw_ref[2:3, :] * cu)
        bg = jnp.dot(hn, wb_ref[...], preferred_element_type=F32)
        z = jnp.dot(hn, wz_ref[...], preferred_element_type=F32)
        y_ref[...] = (bg * conv * _silu(z)).astype(BF16)

    pl.when(j == 0)(functools.partial(tile, True))
    pl.when(j > 0)(functools.partial(tile, False))


def _in_odd(x2, g, w_bf, conv_w, seq):
    m = x2.shape[0]
    tm, tc = IN_TM, IN_ODD_TC
    nb = D_MODEL // tc
    wspec = lambda seg: pl.BlockSpec((D_MODEL, tc), lambda i, j: (0, seg * nb + j))
    return pl.pallas_call(
        functools.partial(_in_odd_kernel, tiles_per_seq=seq // tm),
        out_shape=jax.ShapeDtypeStruct((m, D_MODEL), BF16),
        grid=(m // tm, nb),
        in_specs=[
            pl.BlockSpec((tm, D_MODEL), lambda i, j: (i, 0)),
            pl.BlockSpec((1, D_MODEL), lambda i, j: (0, 0)),
            wspec(0), wspec(1), wspec(2), wspec(3),
            pl.BlockSpec((CONV_WIDTH, tc), lambda i, j: (0, j)),
        ],
        out_specs=pl.BlockSpec((tm, tc), lambda i, j: (i, j)),
        scratch_shapes=[
            pltpu.VMEM((tm, D_MODEL), BF16),
            pltpu.VMEM((HALO + tm, tc), F32),
            pltpu.VMEM((nb, HALO, tc), F32),
        ],
        compiler_params=pltpu.CompilerParams(
            dimension_semantics=("arbitrary", "arbitrary"), vmem_limit_bytes=VMEM_LIMIT_BYTES),
        name="in_odd",
    )(x2, g, w_bf, w_bf, w_bf, w_bf, conv_w)


def _even_gain_row(q_g, k_g):
    return jnp.concatenate([
        jnp.ones((OFF_AQ,), F32),
        jnp.tile(q_g * ATT_HD ** -0.5, ATT_HEADS),
        jnp.tile(k_g, ATT_HEADS),
        jnp.ones((IN_WIDTH_EVEN - OFF_Z,), F32),
    ]).reshape(1, IN_WIDTH_EVEN)

def kernel(x, p, pre_norm_g, w_in_even, q_norm_g, k_norm_g, ret_gn_g, w_out_even, w_in_odd, conv_w_odd,
           w_out_odd, ple_norm_g, w_ple_gate, w_ple_proj):
    batch, seq, d = x.shape
    depth = p.shape[0]
    assert d == D_MODEL and seq % (ATT_BLOCK * DILATED_PATTERNS[-1][1]) == 0 and seq % IN_TM == 0
    m = batch * seq
    slopes = jnp.asarray([(2.0 ** (-8.0 / ATT_HEADS)) ** (i + 1) for i in range(ATT_HEADS)], F32)
    x2 = x.reshape(m, d)
    p_all = p.reshape(depth * m, PLE_DIM)
    late_f32 = dict(w_out_even=w_out_even, w_ple_gate=w_ple_gate, w_ple_proj=w_ple_proj, w_in_odd=w_in_odd,
                    w_out_odd=w_out_odd)
    if w_in_even.shape[0] > 1:
        late_f32["w_in_even_rest"] = w_in_even[1:]
    bf = None
    for i in range(depth):
        j = i // 2
        pg = ple_norm_g[i].reshape(1, d)
        g = pre_norm_g[i].reshape(1, d)
        if i % 2 == 0:
            w_in_bf = w_in_even[0].astype(BF16) if j == 0 else bf["w_in_even_rest"][j - 1]
            proj = _in_even(x2, g, w_in_bf, _even_gain_row(q_norm_g[j], k_norm_g[j]))
            y_ret = _retention(proj, ret_gn_g[j], batch, seq)
            to_cast = [a.reshape(-1, a.shape[-1]) for a in late_f32.values()] if bf is None else []
            y_att, cast = _attention(proj, slopes, batch, seq, to_cast)
            if bf is None:
                bf = {name: c.reshape(a.shape) for (name, a), c in zip(late_f32.items(), cast)}
            x2 = _post(x2, y_ret, 0, y_att, 0, p_all, i, bf["w_out_even"][j], pg, bf["w_ple_gate"],
                       bf["w_ple_proj"], "post_even")
        else:
            y = _in_odd(x2, g, bf["w_in_odd"][j], conv_w_odd[j], seq)
            x2 = _post(x2, y, 0, y, 1, p_all, i, bf["w_out_odd"][j], pg, bf["w_ple_gate"], bf["w_ple_proj"],
                       "post_odd")
    return x2.reshape(batch, seq, d)
```

```python
import functools
import math

import jax
import jax.numpy as jnp
from jax import lax
from jax.experimental import pallas as pl
from jax.experimental.pallas import tpu as pltpu

F32 = jnp.float32
BF16 = jnp.bfloat16

D_MODEL = 2048
PLE_DIM = 256
EPS = 1e-6
NEG = -1e30
RET_DK = 128
RET_DV = 256
RET_HEADS = 4
RET_CHUNK = 256
ATT_HD = 128
ATT_HEADS = 8
ATT_BLOCK = 128
DILATED_PATTERNS = ((128, 1), (512, 4), (2048, 16))
RET_QK_WIDTH = RET_HEADS * RET_DK
RET_WIDTH = RET_HEADS * RET_DV
ATT_WIDTH = ATT_HEADS * ATT_HD
IN_WIDTH_EVEN = 2 * RET_QK_WIDTH + RET_WIDTH + 3 * ATT_WIDTH + RET_WIDTH + ATT_WIDTH
CONV_WIDTH = 3

REF_AQ = 2 * RET_QK_WIDTH + RET_WIDTH
REF_AV = REF_AQ + 2 * ATT_WIDTH
OFF_RQ = 0
OFF_RK = RET_QK_WIDTH
OFF_RV = 2 * RET_QK_WIDTH
OFF_AV = OFF_RV + RET_WIDTH
OFF_AQ = OFF_AV + ATT_WIDTH
OFF_AK = OFF_AQ + ATT_WIDTH
OFF_Z = OFF_AK + ATT_WIDTH

VMEM_LIMIT_BYTES = 56 * 1024 * 1024
BF16_ROWS = 16

IN_TM = 1024
IN_EVEN_TN = 1792
IN_EVEN_NC = 256
IN_EVEN_LAST_ROW_SPLIT = 4
IN_ODD_TC = 512
IN_ODD_NC = 256
RET_TQ = 1024
ATT_GROUP = 4
ATT_MERGE_ROWS = 512
POST_TM = 512
POST_NC = 256
HALO = 8


def _rms_rows(x, g):
    ms = jnp.mean(x * x, axis=-1, keepdims=True)
    return x * lax.rsqrt(ms + EPS) * g


def _silu(z):
    return z * jax.nn.sigmoid(z)


def _in_even_kernel(x_ref, g_ref, gain_ref, *refs):
    *w_refs, o_ref, hn_ref = refs
    j = pl.program_id(1)

    @pl.when(j == 0)
    def _():
        hn_ref[...] = _rms_rows(x_ref[...], g_ref[...]).astype(BF16)

    tm, tn = o_ref.shape
    n_chunks = tn // IN_EVEN_NC

    def tile(jj):
        for cc in range(n_chunks):
            col0 = jj * tn + cc * IN_EVEN_NC
            n_row = IN_EVEN_LAST_ROW_SPLIT if cc == n_chunks - 1 else 1
            for rr in range(n_row):
                rows = slice(rr * (tm // n_row), (rr + 1) * (tm // n_row))
                acc = jnp.dot(hn_ref[rows, :], w_refs[cc][...], preferred_element_type=F32)
                for hh in range(IN_EVEN_NC // ATT_HD):
                    sl = slice(cc * IN_EVEN_NC + hh * ATT_HD, cc * IN_EVEN_NC + (hh + 1) * ATT_HD)
                    a = acc[:, hh * ATT_HD:(hh + 1) * ATT_HD]
                    if col0 >= OFF_Z:
                        a = _silu(a)
                    elif col0 >= OFF_AQ:
                        a = _rms_rows(a, gain_ref[:, sl])
                    o_ref[rows, sl] = a.astype(BF16)

    for jj in range(IN_WIDTH_EVEN // tn):
        pl.when(j == jj)(functools.partial(tile, jj))


def _even_weight_chunk_spec(cc, chunks_per_tile):
    nc = IN_EVEN_NC

    def index(i, j):
        k = j * chunks_per_tile + cc
        in_av = (k >= OFF_AV // nc) & (k < OFF_AQ // nc)
        in_qk = (k >= OFF_AQ // nc) & (k < OFF_Z // nc)
        ref = jnp.where(in_av, k + (REF_AV - OFF_AV) // nc, jnp.where(in_qk, k - (OFF_AQ - REF_AQ) // nc, k))
        return (0, ref)

    return pl.BlockSpec((D_MODEL, nc), index)


def _in_even(x2, g, w_bf, gain_row):
    m = x2.shape[0]
    tm, tn = IN_TM, IN_EVEN_TN
    n_chunks = tn // IN_EVEN_NC
    return pl.pallas_call(
        _in_even_kernel,
        out_shape=jax.ShapeDtypeStruct((m, IN_WIDTH_EVEN), BF16),
        grid=(m // tm, IN_WIDTH_EVEN // tn),
        in_specs=[
            pl.BlockSpec((tm, D_MODEL), lambda i, j: (i, 0)),
            pl.BlockSpec((1, D_MODEL), lambda i, j: (0, 0)),
            pl.BlockSpec((1, tn), lambda i, j: (0, j)),
            *[_even_weight_chunk_spec(cc, n_chunks) for cc in range(n_chunks)],
        ],
        out_specs=pl.BlockSpec((tm, tn), lambda i, j: (i, j)),
        scratch_shapes=[pltpu.VMEM((tm, D_MODEL), BF16)],
        compiler_params=pltpu.CompilerParams(
            dimension_semantics=("arbitrary", "arbitrary"), vmem_limit_bytes=VMEM_LIMIT_BYTES),
        name="in_even",
    )(x2, g, gain_row, *([w_bf] * n_chunks))


def _ret_kernel(q_ref, k_ref, v_ref, z_ref, gn_ref, o_ref, r_ref, decay_ref):
    c = RET_CHUNK
    tq = q_ref.shape[0]
    kscale = RET_DK ** -0.5
    log_g = [math.log(1.0 - 2.0 ** (-5.0 - h)) for h in range(RET_HEADS)]

    @pl.when(pl.program_id(1) == 0)
    def _():
        r_ref[...] = jnp.zeros(r_ref.shape, F32)

    @pl.when((pl.program_id(0) == 0) & (pl.program_id(1) == 0))
    def _():
        row = lax.broadcasted_iota(jnp.int32, (c, c), 0)
        col = lax.broadcasted_iota(jnp.int32, (c, c), 1)
        diff = (row - col).astype(F32)
        for h in range(RET_HEADS):
            decay_ref[h] = jnp.where(diff >= 0, jnp.exp(log_g[h] * jnp.maximum(diff, 0.0)), 0.0) * kscale

    pos = lax.broadcasted_iota(jnp.int32, (c, 1), 0).astype(F32)

    for h in range(RET_HEADS):
        lg = log_g[h]
        decay = decay_ref[h]
        xi = jnp.exp(lg * (pos + 1.0))
        zeta = jnp.exp(lg * (c - 1.0 - pos)) * kscale
        g_chunk = math.exp(lg * c)
        qk_sl = slice(h * RET_DK, (h + 1) * RET_DK)
        v_sl = slice(h * RET_DV, (h + 1) * RET_DV)
        gn = gn_ref[h:h + 1, :]
        chunks = [slice(ci * c, (ci + 1) * c) for ci in range(tq // c)]

        contrib = []
        for rows in chunks:
            kz_t = (k_ref[rows, qk_sl].astype(F32) * zeta).T.astype(BF16)
            contrib.append(jnp.dot(kz_t, v_ref[rows, v_sl], preferred_element_type=F32))

        state = r_ref[h]
        states = []
        for s_c in contrib:
            states.append(state)
            state = g_chunk * state + s_c
        r_ref[h] = state

        for rows, r_prev in zip(chunks, states):
            q = q_ref[rows, qk_sl]
            v = v_ref[rows, v_sl]
            s = lax.dot_general(q, k_ref[rows, qk_sl], (((1,), (1,)), ((), ())),
                                preferred_element_type=F32) * decay
            y = jnp.dot(s.astype(BF16), v, preferred_element_type=F32)
            y = y + jnp.dot(q, r_prev.astype(BF16), preferred_element_type=F32) * xi
            mu = jnp.mean(y, axis=-1, keepdims=True)
            yc = y - mu
            var = jnp.mean(yc * yc, axis=-1, keepdims=True)
            yn = yc * lax.rsqrt(var + EPS) * gn
            o_ref[rows, v_sl] = (yn * z_ref[rows, v_sl].astype(F32)).astype(BF16)


def _retention(proj, gn_g, batch, seq):
    m = proj.shape[0]
    tq = RET_TQ
    nt = seq // tq
    row = lambda b, t: b * nt + t
    return pl.pallas_call(
        _ret_kernel,
        out_shape=jax.ShapeDtypeStruct((m, RET_WIDTH), BF16),
        grid=(batch, nt),
        in_specs=[
            pl.BlockSpec((tq, RET_QK_WIDTH), lambda b, t: (row(b, t), OFF_RQ // RET_QK_WIDTH)),
            pl.BlockSpec((tq, RET_QK_WIDTH), lambda b, t: (row(b, t), OFF_RK // RET_QK_WIDTH)),
            pl.BlockSpec((tq, RET_WIDTH), lambda b, t: (row(b, t), OFF_RV // RET_WIDTH)),
            pl.BlockSpec((tq, RET_WIDTH), lambda b, t: (row(b, t), OFF_Z // RET_WIDTH)),
            pl.BlockSpec((RET_HEADS, RET_DV), lambda b, t: (0, 0)),
        ],
        out_specs=pl.BlockSpec((tq, RET_WIDTH), lambda b, t: (row(b, t), 0)),
        scratch_shapes=[pltpu.VMEM((RET_HEADS, RET_DK, RET_DV), F32),
                        pltpu.VMEM((RET_HEADS, RET_CHUNK, RET_CHUNK), F32)],
        compiler_params=pltpu.CompilerParams(
            dimension_semantics=("arbitrary", "arbitrary"), vmem_limit_bytes=VMEM_LIMIT_BYTES),
        name="retention",
    )(proj, proj, proj, proj, gn_g)


def _att_kernel(*refs, n_cast):
    slopes_ref, q_ref, k_ref, v_ref, z_ref = refs[:5]
    cast_in, o_ref, cast_out = refs[5:5 + n_cast], refs[5 + n_cast], refs[6 + n_cast:6 + 2 * n_cast]
    (stage_q, stage_k, stage_v, q4, k4, v4, qp4, qp16, kp1, kp4, kp16, vp1, vp4, vp16, bias_ref, s_ref, pn_ref,
     o0, o1, o2, l0, l1, l2) = refs[6 + 2 * n_cast:]
    for src, dst in zip(cast_in, cast_out):
        dst[...] = src[...].astype(BF16)
    blk = ATT_BLOCK
    grp = ATT_GROUP
    seq = q_ref.shape[0]
    nblk = seq // blk
    n_it = nblk // grp
    d4, d16 = DILATED_PATTERNS[1][1], DILATED_PATTERNS[2][1]
    assert DILATED_PATTERNS[0][1] == 1 and grp == d4 and d16 == d4 * d4
    assert n_it % 2 == 0 and n_it >= 4 and nblk % d16 == 0
    slope = slopes_ref[pl.program_id(1)]
    nt = (((1,), (1,)), ((), ()))

    def natural_rows(bi, dil):
        nb = nblk // dil
        start = (bi // nb) + (bi % nb) * (blk * dil)
        if dil == 1:
            return pl.ds(pl.multiple_of(start, blk), blk)
        return pl.ds(start, blk, stride=dil)

    def block_rows(bi, lead=0):
        return pl.ds(pl.multiple_of(bi * blk + lead, blk), blk)

    for pad in (kp1, kp4, kp16, vp1, vp4, vp16):
        pad[0:blk, :] = jnp.zeros((blk, ATT_HD), BF16)
    kp1[blk:, :] = k_ref[...]
    vp1[blk:, :] = v_ref[...]

    def copy_4(it):
        rows = pl.ds(pl.multiple_of(it * (grp * blk), grp * blk), grp * blk)
        stage_q[...] = q_ref[rows, :].astype(F32)
        stage_k[...] = k_ref[rows, :].astype(F32)
        stage_v[...] = v_ref[rows, :].astype(F32)
        for r in range(d4):
            bi = r * (nblk // d4) + it
            src = pl.ds(r, blk, stride=d4)
            q, k, v = stage_q[src, :], stage_k[src, :], stage_v[src, :]
            q4[block_rows(bi), :], k4[block_rows(bi), :], v4[block_rows(bi), :] = q, k, v
            qp4[block_rows(bi), :] = q.astype(BF16)
            kp4[block_rows(bi, blk), :] = k.astype(BF16)
            vp4[block_rows(bi, blk), :] = v.astype(BF16)

    def copy_16(it):
        halves = (nblk // d4) // grp
        r4, n16 = it // halves, it % halves
        for c in range(d4):
            bi = (r4 + d4 * c) * (nblk // d16) + n16
            src = pl.ds(r4 * (seq // d4) + n16 * (grp * blk) + c, blk, stride=d4)
            qp16[block_rows(bi), :] = q4[src, :].astype(BF16)
            kp16[block_rows(bi, blk), :] = k4[src, :].astype(BF16)
            vp16[block_rows(bi, blk), :] = v4[src, :].astype(BF16)

    qi = lax.broadcasted_iota(jnp.int32, (blk, 2 * blk), 0)
    kj = lax.broadcasted_iota(jnp.int32, (blk, 2 * blk), 1)
    rel = qi - kj + blk
    for pi, (window, dil) in enumerate(DILATED_PATTERNS):
        valid = (rel >= 0) & (rel <= window // dil)
        alibi = -slope * (rel * dil).astype(F32)
        bias_ref[pi, 0] = jnp.where(valid & (kj >= blk), alibi, NEG)
        bias_ref[pi, 1] = jnp.where(valid, alibi, NEG)

    o16p, l16p = q4, k4

    def rows_in_d4_order(bi16):
        r16, n16 = bi16 // (nblk // d16), bi16 % (nblk // d16)
        return pl.ds((r16 % d4) * (seq // d4) + n16 * (grp * blk) + r16 // d4, blk, stride=d4)

    def un_permute(it):
        for g in range(grp):
            bi4 = it * grp + g
            o2[natural_rows(bi4, d4), :] = o16p[block_rows(bi4), :]
            l2[natural_rows(bi4, d4), :] = l16p[block_rows(bi4), :]

    def make_stages(pi, dil, q_src, kp, vp, o_g, l_g, copy_next):
        nb = nblk // dil
        two_step = dil == d16
        block_of = (lambda it, g: it + n_it * g) if two_step else (lambda it, g: it * grp + g)
        out_rows = rows_in_d4_order if two_step else (lambda bi: natural_rows(bi, dil))
        o_dst, l_dst = (o16p, l16p) if two_step else (o_g, l_g)

        def scores(it):
            for g in range(grp):
                bi = block_of(it, g)
                kb = kp[pl.ds(pl.multiple_of(bi * blk, blk), 2 * blk), :]
                s_ref[it % 2, g] = lax.dot_general(q_src[block_rows(bi), :], kb, nt, preferred_element_type=F32)
            if copy_next is not None:
                copy_next(it)

        def probs(it):
            for g in range(grp):
                bi = block_of(it, g)
                s = s_ref[it % 2, g] + bias_ref[pi, jnp.minimum(bi % nb, 1)]
                mx = jnp.max(jnp.maximum(s[:, :blk], s[:, blk:]), axis=-1, keepdims=True)
                pr = jnp.exp(s - mx)
                den = jnp.sum(pr[:, :blk] + pr[:, blk:], axis=-1, keepdims=True)
                pn_ref[it % 2, g] = (pr * (1.0 / den)).astype(BF16)
                l_dst[out_rows(bi), :] = jnp.broadcast_to(mx + jnp.log(den), (blk, ATT_HD))

        def values(it):
            for g in range(grp):
                bi = block_of(it, g)
                vb = vp[pl.ds(pl.multiple_of(bi * blk, blk), 2 * blk), :]
                o_dst[out_rows(bi), :] = jnp.dot(pn_ref[it % 2, g], vb, preferred_element_type=F32)
            if two_step:
                un_permute(it)

        return scores, probs, values

    stages = [make_stages(pi, dil, *refs) for pi, ((_, dil), *refs) in enumerate(zip(
        DILATED_PATTERNS, (q_ref, qp4, qp16), (kp1, kp4, kp16), (vp1, vp4, vp16), (o0, o1, o2), (l0, l1, l2),
        (copy_4, copy_16, None)))]

    def fill(scores, probs, values):
        scores(0)
        probs(0)
        scores(1)

    def drain(scores, probs, values):
        values(n_it - 2)
        probs(n_it - 1)
        values(n_it - 1)

    fill(*stages[0])
    for pi, (scores, probs, values) in enumerate(stages):
        def body(it, carry, scores=scores, probs=probs, values=values):
            values(it - 2)
            probs(it - 1)
            scores(it)
            return carry

        lax.fori_loop(2, n_it, body, 0)
        drain(scores, probs, values)
        if pi + 1 < len(stages):
            fill(*stages[pi + 1])

    mrows = ATT_MERGE_ROWS

    def merge(ci, carry):
        rows = pl.ds(pl.multiple_of(ci * mrows, mrows), mrows)
        a0, a1, a2 = l0[rows, :], l1[rows, :], l2[rows, :]
        mx = jnp.maximum(jnp.maximum(a0, a1), a2)
        e0, e1, e2 = jnp.exp(a0 - mx), jnp.exp(a1 - mx), jnp.exp(a2 - mx)
        y = (e0 * o0[rows, :] + e1 * o1[rows, :] + e2 * o2[rows, :]) / (e0 + e1 + e2)
        o_ref[rows, :] = (y * z_ref[rows, :].astype(F32)).astype(BF16)
        return carry

    lax.fori_loop(0, seq // mrows, merge, 0)


def _attention(proj, slopes, batch, seq, to_cast):
    m = proj.shape[0]
    hd = ATT_HD
    n_steps = batch * ATT_HEADS
    spec = lambda off: pl.BlockSpec((seq, hd), lambda b, h: (b, off // hd + h))
    slab = lambda a: pl.BlockSpec((a.shape[0] // n_steps, a.shape[1]), lambda b, h: (b * ATT_HEADS + h, 0))
    assert all(a.shape[0] % (BF16_ROWS * n_steps) == 0 for a in to_cast)
    f32_rows = pltpu.VMEM((seq, hd), F32)
    bf16_rows = pltpu.VMEM((seq, hd), BF16)
    bf16_padded = pltpu.VMEM((seq + ATT_BLOCK, hd), BF16)
    f32_group = pltpu.VMEM((ATT_GROUP * ATT_BLOCK, hd), F32)
    y_att, *cast = pl.pallas_call(
        functools.partial(_att_kernel, n_cast=len(to_cast)),
        out_shape=[jax.ShapeDtypeStruct((m, ATT_WIDTH), BF16)]
        + [jax.ShapeDtypeStruct(a.shape, BF16) for a in to_cast],
        grid=(batch, ATT_HEADS),
        in_specs=[
            pl.BlockSpec(memory_space=pltpu.SMEM),
            spec(OFF_AQ), spec(OFF_AK), spec(OFF_AV), spec(OFF_Z + RET_WIDTH),
        ] + [slab(a) for a in to_cast],
        out_specs=[pl.BlockSpec((seq, hd), lambda b, h: (b, h))] + [slab(a) for a in to_cast],
        scratch_shapes=[
            f32_group, f32_group, f32_group, f32_rows, f32_rows, f32_rows,
            bf16_rows, bf16_rows,
            bf16_padded, bf16_padded, bf16_padded, bf16_padded, bf16_padded, bf16_padded,
            pltpu.VMEM((len(DILATED_PATTERNS), 2, ATT_BLOCK, 2 * ATT_BLOCK), F32),
            pltpu.VMEM((2, ATT_GROUP, ATT_BLOCK, 2 * ATT_BLOCK), F32),
            pltpu.VMEM((2, ATT_GROUP, ATT_BLOCK, 2 * ATT_BLOCK), BF16),
            f32_rows, f32_rows, f32_rows, f32_rows, f32_rows, f32_rows,
        ],
        compiler_params=pltpu.CompilerParams(
            dimension_semantics=("arbitrary", "arbitrary"), vmem_limit_bytes=VMEM_LIMIT_BYTES),
        name="dilated_attention",
    )(slopes, proj, proj, proj, proj, *to_cast)
    return y_att, cast


def _post_tail(x1, p_ref, pg_ref, wg_ref, wp_ref, o_ref):
    h2 = _rms_rows(x1, pg_ref[...]).astype(BF16)
    pb = p_ref[...].astype(BF16)
    for cc in range(D_MODEL // POST_NC):
        sl = slice(cc * POST_NC, (cc + 1) * POST_NC)
        gate = jax.nn.sigmoid(jnp.dot(h2, wg_ref[:, sl], preferred_element_type=F32))
        pp = jnp.dot(pb, wp_ref[:, sl], preferred_element_type=F32)
        o_ref[:, sl] = x1[:, sl] + gate * pp


def _post_kernel(x_ref, ya_ref, yb_ref, p_ref, wa_ref, wb_ref, pg_ref, wg_ref, wp_ref, o_ref):
    x1 = x_ref[...] + jnp.dot(ya_ref[...], wa_ref[...], preferred_element_type=F32)
    x1 = x1 + jnp.dot(yb_ref[...], wb_ref[...], preferred_element_type=F32)
    _post_tail(x1, p_ref, pg_ref, wg_ref, wp_ref, o_ref)


def _const_spec(shape):
    return pl.BlockSpec(shape, lambda i: (0,) * len(shape), pipeline_mode=pl.Buffered(1))


def _ple_rows_spec(layer, m, tm):
    return pl.BlockSpec((tm, PLE_DIM), lambda i: (layer * (m // tm) + i, 0))


def _post(x2, ya, ya_blk, yb, yb_blk, p_all, layer, w_out_bf, pg, wg_all_bf, wp_all_bf, name):
    m = x2.shape[0]
    tm = POST_TM
    half = D_MODEL // 2
    rows = lambda w, blk=0: pl.BlockSpec((tm, w), lambda i: (i, blk))
    layer_spec = lambda k: pl.BlockSpec((None, k, D_MODEL), lambda i: (layer, 0, 0), pipeline_mode=pl.Buffered(1))
    return pl.pallas_call(
        _post_kernel,
        out_shape=jax.ShapeDtypeStruct((m, D_MODEL), F32),
        grid=(m // tm,),
        in_specs=[
            rows(D_MODEL), rows(half, ya_blk), rows(half, yb_blk), _ple_rows_spec(layer, m, tm),
            pl.BlockSpec((half, D_MODEL), lambda i: (0, 0), pipeline_mode=pl.Buffered(1)),
            pl.BlockSpec((half, D_MODEL), lambda i: (1, 0), pipeline_mode=pl.Buffered(1)),
            _const_spec((1, D_MODEL)), layer_spec(D_MODEL), layer_spec(PLE_DIM),
        ],
        out_specs=rows(D_MODEL),
        compiler_params=pltpu.CompilerParams(
            dimension_semantics=("arbitrary",), vmem_limit_bytes=VMEM_LIMIT_BYTES),
        name=name,
    )(x2, ya, yb, p_all, w_out_bf, w_out_bf, pg, wg_all_bf, wp_all_bf)


def _in_odd_kernel(x_ref, g_ref, wb_ref, wc_ref, wu_ref, wz_ref, cw_ref, y_ref, hn_ref, ext_ref, halo_ref, *,
                   tiles_per_seq):
    i, j = pl.program_id(0), pl.program_id(1)
    tm = x_ref.shape[0]

    @pl.when((i == 0) & (j == 0))
    def _():
        halo_ref[...] = jnp.zeros(halo_ref.shape, F32)

    def tile(with_pre_norm):
        if with_pre_norm:
            hn_ref[...] = _rms_rows(x_ref[...], g_ref[...]).astype(BF16)
        hn = hn_ref[...]
        first = (i % tiles_per_seq) == 0
        for sc in range(y_ref.shape[1] // IN_ODD_NC):
            cols = slice(sc * IN_ODD_NC, (sc + 1) * IN_ODD_NC)
            cu = (jnp.dot(hn, wc_ref[:, cols], preferred_element_type=F32)
                  * jnp.dot(hn, wu_ref[:, cols], preferred_element_type=F32))
            ext_ref[sc, 0:HALO, :] = jnp.where(first, 0.0, halo_ref[j, :, cols])
            ext_ref[sc, HALO:HALO + tm, :] = cu
            halo_ref[j, :, cols] = cu[tm - HALO:, :]
            conv = (cw_ref[0:1, cols] * ext_ref[sc, HALO - 2:HALO - 2 + tm, :]
                    + cw_ref[1:2, cols] * ext_ref[sc, HALO - 1:HALO - 1 + tm, :]
                    + cw_ref[2:3, cols] * cu)
            bg = jnp.dot(hn, wb_ref[:, cols], preferred_element_type=F32)
            z = jnp.dot(hn, wz_ref[:, cols], preferred_element_type=F32)
            y_ref[:, cols] = (bg * conv * _silu(z)).astype(BF16)

    pl.when(j == 0)(functools.partial(tile, True))
    pl.when(j > 0)(functools.partial(tile, False))


def _in_odd(x2, g, w_bf, conv_w, seq):
    m = x2.shape[0]
    tm, tc = IN_TM, IN_ODD_TC
    nb = D_MODEL // tc
    wspec = lambda seg: pl.BlockSpec((D_MODEL, tc), lambda i, j: (0, seg * nb + j))
    return pl.pallas_call(
        functools.partial(_in_odd_kernel, tiles_per_seq=seq // tm),
        out_shape=jax.ShapeDtypeStruct((m, D_MODEL), BF16),
        grid=(m // tm, nb),
        in_specs=[
            pl.BlockSpec((tm, D_MODEL), lambda i, j: (i, 0)),
            pl.BlockSpec((1, D_MODEL), lambda i, j: (0, 0)),
            wspec(0), wspec(1), wspec(2), wspec(3),
            pl.BlockSpec((CONV_WIDTH, tc), lambda i, j: (0, j)),
        ],
        out_specs=pl.BlockSpec((tm, tc), lambda i, j: (i, j)),
        scratch_shapes=[
            pltpu.VMEM((tm, D_MODEL), BF16),
            pltpu.VMEM((tc // IN_ODD_NC, HALO + tm, IN_ODD_NC), F32),
            pltpu.VMEM((nb, HALO, tc), F32),
        ],
        compiler_params=pltpu.CompilerParams(
            dimension_semantics=("arbitrary", "arbitrary"), vmem_limit_bytes=VMEM_LIMIT_BYTES),
        name="in_odd",
    )(x2, g, w_bf, w_bf, w_bf, w_bf, conv_w)


def _even_gain_row(q_g, k_g):
    return jnp.concatenate([
        jnp.ones((OFF_AQ,), F32),
        jnp.tile(q_g * ATT_HD ** -0.5, ATT_HEADS),
        jnp.tile(k_g, ATT_HEADS),
        jnp.ones((IN_WIDTH_EVEN - OFF_Z,), F32),
    ]).reshape(1, IN_WIDTH_EVEN)

def kernel(x, p, pre_norm_g, w_in_even, q_norm_g, k_norm_g, ret_gn_g, w_out_even, w_in_odd, conv_w_odd,
           w_out_odd, ple_norm_g, w_ple_gate, w_ple_proj):
    batch, seq, d = x.shape
    depth = p.shape[0]
    assert d == D_MODEL and seq % (ATT_BLOCK * DILATED_PATTERNS[-1][1]) == 0 and seq % IN_TM == 0
    m = batch * seq
    slopes = jnp.asarray([(2.0 ** (-8.0 / ATT_HEADS)) ** (i + 1) for i in range(ATT_HEADS)], F32)
    x2 = x.reshape(m, d)
    p_all = p.reshape(depth * m, PLE_DIM)
    late_f32 = dict(w_out_even=w_out_even, w_ple_gate=w_ple_gate, w_ple_proj=w_ple_proj, w_in_odd=w_in_odd,
                    w_out_odd=w_out_odd)
    if w_in_even.shape[0] > 1:
        late_f32["w_in_even_rest"] = w_in_even[1:]
    bf = None
    for i in range(depth):
        j = i // 2
        pg = ple_norm_g[i].reshape(1, d)
        g = pre_norm_g[i].reshape(1, d)
        if i % 2 == 0:
            w_in_bf = w_in_even[0].astype(BF16) if j == 0 else bf["w_in_even_rest"][j - 1]
            proj = _in_even(x2, g, w_in_bf, _even_gain_row(q_norm_g[j], k_norm_g[j]))
            y_ret = _retention(proj, ret_gn_g[j], batch, seq)
            to_cast = [a.reshape(-1, a.shape[-1]) for a in late_f32.values()] if bf is None else []
            y_att, cast = _attention(proj, slopes, batch, seq, to_cast)
            if bf is None:
                bf = {name: c.reshape(a.shape) for (name, a), c in zip(late_f32.items(), cast)}
            x2 = _post(x2, y_ret, 0, y_att, 0, p_all, i, bf["w_out_even"][j], pg, bf["w_ple_gate"],
                       bf["w_ple_proj"], "post_even")
        else:
            y = _in_odd(x2, g, bf["w_in_odd"][j], conv_w_odd[j], seq)
            x2 = _post(x2, y, 0, y, 1, p_all, i, bf["w_out_odd"][j], pg, bf["w_ple_gate"], bf["w_ple_proj"],
                       "post_odd")
    return x2.reshape(batch, seq, d)
```

```python
import functools
import math

import jax
import jax.numpy as jnp
from jax import lax
from jax.experimental import pallas as pl
from jax.experimental.pallas import tpu as pltpu

F32 = jnp.float32
BF16 = jnp.bfloat16

D_MODEL = 2048
PLE_DIM = 256
EPS = 1e-6
NEG = -1e30
RET_DK = 128
RET_DV = 256
RET_HEADS = 4
RET_CHUNK = 256
ATT_HD = 128
ATT_HEADS = 8
ATT_BLOCK = 128
DILATED_PATTERNS = ((128, 1), (512, 4), (2048, 16))
RET_QK_WIDTH = RET_HEADS * RET_DK
RET_WIDTH = RET_HEADS * RET_DV
ATT_WIDTH = ATT_HEADS * ATT_HD
IN_WIDTH_EVEN = 2 * RET_QK_WIDTH + RET_WIDTH + 3 * ATT_WIDTH + RET_WIDTH + ATT_WIDTH
CONV_WIDTH = 3

REF_AQ = 2 * RET_QK_WIDTH + RET_WIDTH
REF_AV = REF_AQ + 2 * ATT_WIDTH
OFF_RQ = 0
OFF_RK = RET_QK_WIDTH
OFF_RV = 2 * RET_QK_WIDTH
OFF_AV = OFF_RV + RET_WIDTH
OFF_AQ = OFF_AV + ATT_WIDTH
OFF_AK = OFF_AQ + ATT_WIDTH
OFF_Z = OFF_AK + ATT_WIDTH

VMEM_LIMIT_BYTES = 56 * 1024 * 1024
BF16_ROWS = 16

IN_TM = 1024
IN_EVEN_TN = 1792
IN_EVEN_NC = 256
IN_EVEN_LAST_ROW_SPLIT = 4
IN_ODD_TC = 512
IN_ODD_NC = 256
RET_TQ = 2048
ATT_GROUP = 4
ATT_MERGE_ROWS = 512
POST_TM = 512
POST_NC = 256
HALO = 8


def _rms_rows(x, g):
    ms = jnp.mean(x * x, axis=-1, keepdims=True)
    return x * lax.rsqrt(ms + EPS) * g


def _silu(z):
    return z * jax.nn.sigmoid(z)


def _in_even_kernel(x_ref, g_ref, gain_ref, *refs):
    *w_refs, o_ref, hn_ref = refs
    j = pl.program_id(1)

    @pl.when(j == 0)
    def _():
        hn_ref[...] = _rms_rows(x_ref[...], g_ref[...]).astype(BF16)

    tm, tn = o_ref.shape
    n_chunks = tn // IN_EVEN_NC

    def tile(jj):
        for cc in range(n_chunks):
            col0 = jj * tn + cc * IN_EVEN_NC
            n_row = IN_EVEN_LAST_ROW_SPLIT if cc == n_chunks - 1 else 1
            for rr in range(n_row):
                rows = slice(rr * (tm // n_row), (rr + 1) * (tm // n_row))
                acc = jnp.dot(hn_ref[rows, :], w_refs[cc][...], preferred_element_type=F32)
                for hh in range(IN_EVEN_NC // ATT_HD):
                    sl = slice(cc * IN_EVEN_NC + hh * ATT_HD, cc * IN_EVEN_NC + (hh + 1) * ATT_HD)
                    a = acc[:, hh * ATT_HD:(hh + 1) * ATT_HD]
                    if col0 >= OFF_Z:
                        a = _silu(a)
                    elif col0 >= OFF_AQ:
                        a = _rms_rows(a, gain_ref[:, sl])
                    o_ref[rows, sl] = a.astype(BF16)

    for jj in range(IN_WIDTH_EVEN // tn):
        pl.when(j == jj)(functools.partial(tile, jj))


def _even_weight_chunk_spec(cc, chunks_per_tile):
    nc = IN_EVEN_NC

    def index(i, j):
        k = j * chunks_per_tile + cc
        in_av = (k >= OFF_AV // nc) & (k < OFF_AQ // nc)
        in_qk = (k >= OFF_AQ // nc) & (k < OFF_Z // nc)
        ref = jnp.where(in_av, k + (REF_AV - OFF_AV) // nc, jnp.where(in_qk, k - (OFF_AQ - REF_AQ) // nc, k))
        return (0, ref)

    return pl.BlockSpec((D_MODEL, nc), index)


def _in_even(x2, g, w_bf, gain_row):
    m = x2.shape[0]
    tm, tn = IN_TM, IN_EVEN_TN
    n_chunks = tn // IN_EVEN_NC
    return pl.pallas_call(
        _in_even_kernel,
        out_shape=jax.ShapeDtypeStruct((m, IN_WIDTH_EVEN), BF16),
        grid=(m // tm, IN_WIDTH_EVEN // tn),
        in_specs=[
            pl.BlockSpec((tm, D_MODEL), lambda i, j: (i, 0)),
            pl.BlockSpec((1, D_MODEL), lambda i, j: (0, 0)),
            pl.BlockSpec((1, tn), lambda i, j: (0, j)),
            *[_even_weight_chunk_spec(cc, n_chunks) for cc in range(n_chunks)],
        ],
        out_specs=pl.BlockSpec((tm, tn), lambda i, j: (i, j)),
        scratch_shapes=[pltpu.VMEM((tm, D_MODEL), BF16)],
        compiler_params=pltpu.CompilerParams(
            dimension_semantics=("arbitrary", "arbitrary"), vmem_limit_bytes=VMEM_LIMIT_BYTES),
        name="in_even",
    )(x2, g, gain_row, *([w_bf] * n_chunks))


def _ret_kernel(q_ref, k_ref, v_ref, z_ref, gn_ref, o_ref, r_ref, decay_ref):
    c = RET_CHUNK
    tq = q_ref.shape[0]
    kscale = RET_DK ** -0.5
    log_g = [math.log(1.0 - 2.0 ** (-5.0 - h)) for h in range(RET_HEADS)]

    @pl.when(pl.program_id(1) == 0)
    def _():
        r_ref[...] = jnp.zeros(r_ref.shape, F32)

    @pl.when((pl.program_id(0) == 0) & (pl.program_id(1) == 0))
    def _():
        row = lax.broadcasted_iota(jnp.int32, (c, c), 0)
        col = lax.broadcasted_iota(jnp.int32, (c, c), 1)
        diff = (row - col).astype(F32)
        for h in range(RET_HEADS):
            decay_ref[h] = jnp.where(diff >= 0, jnp.exp(log_g[h] * jnp.maximum(diff, 0.0)), 0.0) * kscale

    pos = lax.broadcasted_iota(jnp.int32, (c, 1), 0).astype(F32)

    for h in range(RET_HEADS):
        lg = log_g[h]
        decay = decay_ref[h]
        xi = jnp.exp(lg * (pos + 1.0))
        zeta = jnp.exp(lg * (c - 1.0 - pos)) * kscale
        g_chunk = math.exp(lg * c)
        qk_sl = slice(h * RET_DK, (h + 1) * RET_DK)
        v_sl = slice(h * RET_DV, (h + 1) * RET_DV)
        gn = gn_ref[h:h + 1, :]
        chunks = [slice(ci * c, (ci + 1) * c) for ci in range(tq // c)]

        contrib = []
        for rows in chunks:
            kz_t = (k_ref[rows, qk_sl].astype(F32) * zeta).T.astype(BF16)
            contrib.append(jnp.dot(kz_t, v_ref[rows, v_sl], preferred_element_type=F32))

        state = r_ref[h]
        states = []
        for s_c in contrib:
            states.append(state)
            state = g_chunk * state + s_c
        r_ref[h] = state

        for rows, r_prev in zip(chunks, states):
            q = q_ref[rows, qk_sl]
            v = v_ref[rows, v_sl]
            s = lax.dot_general(q, k_ref[rows, qk_sl], (((1,), (1,)), ((), ())),
                                preferred_element_type=F32) * decay
            y = jnp.dot(s.astype(BF16), v, preferred_element_type=F32)
            y = y + jnp.dot(q, r_prev.astype(BF16), preferred_element_type=F32) * xi
            mu = jnp.mean(y, axis=-1, keepdims=True)
            yc = y - mu
            var = jnp.mean(yc * yc, axis=-1, keepdims=True)
            yn = yc * lax.rsqrt(var + EPS) * gn
            o_ref[rows, v_sl] = (yn * z_ref[rows, v_sl].astype(F32)).astype(BF16)


def _retention(proj, gn_g, batch, seq):
    m = proj.shape[0]
    tq = RET_TQ
    nt = seq // tq
    row = lambda b, t: b * nt + t
    return pl.pallas_call(
        _ret_kernel,
        out_shape=jax.ShapeDtypeStruct((m, RET_WIDTH), BF16),
        grid=(batch, nt),
        in_specs=[
            pl.BlockSpec((tq, RET_QK_WIDTH), lambda b, t: (row(b, t), OFF_RQ // RET_QK_WIDTH)),
            pl.BlockSpec((tq, RET_QK_WIDTH), lambda b, t: (row(b, t), OFF_RK // RET_QK_WIDTH)),
            pl.BlockSpec((tq, RET_WIDTH), lambda b, t: (row(b, t), OFF_RV // RET_WIDTH)),
            pl.BlockSpec((tq, RET_WIDTH), lambda b, t: (row(b, t), OFF_Z // RET_WIDTH)),
            pl.BlockSpec((RET_HEADS, RET_DV), lambda b, t: (0, 0)),
        ],
        out_specs=pl.BlockSpec((tq, RET_WIDTH), lambda b, t: (row(b, t), 0)),
        scratch_shapes=[pltpu.VMEM((RET_HEADS, RET_DK, RET_DV), F32),
                        pltpu.VMEM((RET_HEADS, RET_CHUNK, RET_CHUNK), F32)],
        compiler_params=pltpu.CompilerParams(
            dimension_semantics=("arbitrary", "arbitrary"), vmem_limit_bytes=VMEM_LIMIT_BYTES),
        name="retention",
    )(proj, proj, proj, proj, gn_g)


def _att_kernel(*refs, n_cast):
    slopes_ref, q_ref, k_ref, v_ref, z_ref = refs[:5]
    cast_in, o_ref, cast_out = refs[5:5 + n_cast], refs[5 + n_cast], refs[6 + n_cast:6 + 2 * n_cast]
    (stage_q, stage_k, stage_v, q4, k4, v4, qp4, qp16, kp1, kp4, kp16, vp1, vp4, vp16, bias_ref, s_ref, pn_ref,
     o0, o1, o2, l0, l1, l2) = refs[6 + 2 * n_cast:]
    for src, dst in zip(cast_in, cast_out):
        dst[...] = src[...].astype(BF16)
    blk = ATT_BLOCK
    grp = ATT_GROUP
    seq = q_ref.shape[0]
    nblk = seq // blk
    n_it = nblk // grp
    d4, d16 = DILATED_PATTERNS[1][1], DILATED_PATTERNS[2][1]
    assert DILATED_PATTERNS[0][1] == 1 and grp == d4 and d16 == d4 * d4
    assert n_it % 2 == 0 and n_it >= 4 and nblk % d16 == 0
    slope = slopes_ref[pl.program_id(1)]
    nt = (((1,), (1,)), ((), ()))

    def natural_rows(bi, dil):
        nb = nblk // dil
        start = (bi // nb) + (bi % nb) * (blk * dil)
        if dil == 1:
            return pl.ds(pl.multiple_of(start, blk), blk)
        return pl.ds(start, blk, stride=dil)

    def block_rows(bi, lead=0):
        return pl.ds(pl.multiple_of(bi * blk + lead, blk), blk)

    for pad in (kp1, kp4, kp16, vp1, vp4, vp16):
        pad[0:blk, :] = jnp.zeros((blk, ATT_HD), BF16)
    kp1[blk:, :] = k_ref[...]
    vp1[blk:, :] = v_ref[...]

    def copy_4(it):
        rows = pl.ds(pl.multiple_of(it * (grp * blk), grp * blk), grp * blk)
        stage_q[...] = q_ref[rows, :].astype(F32)
        stage_k[...] = k_ref[rows, :].astype(F32)
        stage_v[...] = v_ref[rows, :].astype(F32)
        for r in range(d4):
            bi = r * (nblk // d4) + it
            src = pl.ds(r, blk, stride=d4)
            q, k, v = stage_q[src, :], stage_k[src, :], stage_v[src, :]
            q4[block_rows(bi), :], k4[block_rows(bi), :], v4[block_rows(bi), :] = q, k, v
            qp4[block_rows(bi), :] = q.astype(BF16)
            kp4[block_rows(bi, blk), :] = k.astype(BF16)
            vp4[block_rows(bi, blk), :] = v.astype(BF16)

    def copy_16(it):
        halves = (nblk // d4) // grp
        r4, n16 = it // halves, it % halves
        for c in range(d4):
            bi = (r4 + d4 * c) * (nblk // d16) + n16
            src = pl.ds(r4 * (seq // d4) + n16 * (grp * blk) + c, blk, stride=d4)
            qp16[block_rows(bi), :] = q4[src, :].astype(BF16)
            kp16[block_rows(bi, blk), :] = k4[src, :].astype(BF16)
            vp16[block_rows(bi, blk), :] = v4[src, :].astype(BF16)

    qi = lax.broadcasted_iota(jnp.int32, (blk, 2 * blk), 0)
    kj = lax.broadcasted_iota(jnp.int32, (blk, 2 * blk), 1)
    rel = qi - kj + blk
    for pi, (window, dil) in enumerate(DILATED_PATTERNS):
        valid = (rel >= 0) & (rel <= window // dil)
        alibi = -slope * (rel * dil).astype(F32)
        bias_ref[pi, 0] = jnp.where(valid & (kj >= blk), alibi, NEG)
        bias_ref[pi, 1] = jnp.where(valid, alibi, NEG)

    o16p, l16p = q4, k4

    def rows_in_d4_order(bi16):
        r16, n16 = bi16 // (nblk // d16), bi16 % (nblk // d16)
        return pl.ds((r16 % d4) * (seq // d4) + n16 * (grp * blk) + r16 // d4, blk, stride=d4)

    def un_permute(it):
        for g in range(grp):
            bi4 = it * grp + g
            o2[natural_rows(bi4, d4), :] = o16p[block_rows(bi4), :]
            l2[natural_rows(bi4, d4), :] = l16p[block_rows(bi4), :]

    def make_stages(pi, dil, q_src, kp, vp, o_g, l_g, copy_next):
        nb = nblk // dil
        two_step = dil == d16
        block_of = (lambda it, g: it + n_it * g) if two_step else (lambda it, g: it * grp + g)
        out_rows = rows_in_d4_order if two_step else (lambda bi: natural_rows(bi, dil))
        o_dst, l_dst = (o16p, l16p) if two_step else (o_g, l_g)

        def scores(it):
            for g in range(grp):
                bi = block_of(it, g)
                kb = kp[pl.ds(pl.multiple_of(bi * blk, blk), 2 * blk), :]
                s_ref[it % 2, g] = lax.dot_general(q_src[block_rows(bi), :], kb, nt, preferred_element_type=F32)
            if copy_next is not None:
                copy_next(it)

        def probs(it):
            for g in range(grp):
                bi = block_of(it, g)
                s = s_ref[it % 2, g] + bias_ref[pi, jnp.minimum(bi % nb, 1)]
                mx = jnp.max(jnp.maximum(s[:, :blk], s[:, blk:]), axis=-1, keepdims=True)
                pr = jnp.exp(s - mx)
                den = jnp.sum(pr[:, :blk] + pr[:, blk:], axis=-1, keepdims=True)
                pn_ref[it % 2, g] = (pr * (1.0 / den)).astype(BF16)
                l_dst[out_rows(bi), :] = jnp.broadcast_to(mx + jnp.log(den), (blk, ATT_HD))

        def values(it):
            for g in range(grp):
                bi = block_of(it, g)
                vb = vp[pl.ds(pl.multiple_of(bi * blk, blk), 2 * blk), :]
                o_dst[out_rows(bi), :] = jnp.dot(pn_ref[it % 2, g], vb, preferred_element_type=F32)
            if two_step:
                un_permute(it)

        return scores, probs, values

    stages = [make_stages(pi, dil, *refs) for pi, ((_, dil), *refs) in enumerate(zip(
        DILATED_PATTERNS, (q_ref, qp4, qp16), (kp1, kp4, kp16), (vp1, vp4, vp16), (o0, o1, o2), (l0, l1, l2),
        (copy_4, copy_16, None)))]

    def fill(scores, probs, values):
        scores(0)
        probs(0)
        scores(1)

    def drain(scores, probs, values):
        values(n_it - 2)
        probs(n_it - 1)
        values(n_it - 1)

    fill(*stages[0])
    for pi, (scores, probs, values) in enumerate(stages):
        def body(it, carry, scores=scores, probs=probs, values=values):
            values(it - 2)
            probs(it - 1)
            scores(it)
            return carry

        lax.fori_loop(2, n_it, body, 0)
        drain(scores, probs, values)
        if pi + 1 < len(stages):
            fill(*stages[pi + 1])

    mrows = ATT_MERGE_ROWS

    def merge(ci, carry):
        rows = pl.ds(pl.multiple_of(ci * mrows, mrows), mrows)
        a0, a1, a2 = l0[rows, :], l1[rows, :], l2[rows, :]
        mx = jnp.maximum(jnp.maximum(a0, a1), a2)
        e0, e1, e2 = jnp.exp(a0 - mx), jnp.exp(a1 - mx), jnp.exp(a2 - mx)
        y = (e0 * o0[rows, :] + e1 * o1[rows, :] + e2 * o2[rows, :]) / (e0 + e1 + e2)
        o_ref[rows, :] = (y * z_ref[rows, :].astype(F32)).astype(BF16)
        return carry

    lax.fori_loop(0, seq // mrows, merge, 0)


def _attention(proj, slopes, batch, seq, to_cast):
    m = proj.shape[0]
    hd = ATT_HD
    n_steps = batch * ATT_HEADS
    spec = lambda off: pl.BlockSpec((seq, hd), lambda b, h: (b, off // hd + h))
    slab = lambda a: pl.BlockSpec((a.shape[0] // n_steps, a.shape[1]), lambda b, h: (b * ATT_HEADS + h, 0))
    assert all(a.shape[0] % (BF16_ROWS * n_steps) == 0 for a in to_cast)
    f32_rows = pltpu.VMEM((seq, hd), F32)
    bf16_rows = pltpu.VMEM((seq, hd), BF16)
    bf16_padded = pltpu.VMEM((seq + ATT_BLOCK, hd), BF16)
    f32_group = pltpu.VMEM((ATT_GROUP * ATT_BLOCK, hd), F32)
    y_att, *cast = pl.pallas_call(
        functools.partial(_att_kernel, n_cast=len(to_cast)),
        out_shape=[jax.ShapeDtypeStruct((m, ATT_WIDTH), BF16)]
        + [jax.ShapeDtypeStruct(a.shape, BF16) for a in to_cast],
        grid=(batch, ATT_HEADS),
        in_specs=[
            pl.BlockSpec(memory_space=pltpu.SMEM),
            spec(OFF_AQ), spec(OFF_AK), spec(OFF_AV), spec(OFF_Z + RET_WIDTH),
        ] + [slab(a) for a in to_cast],
        out_specs=[pl.BlockSpec((seq, hd), lambda b, h: (b, h))] + [slab(a) for a in to_cast],
        scratch_shapes=[
            f32_group, f32_group, f32_group, f32_rows, f32_rows, f32_rows,
            bf16_rows, bf16_rows,
            bf16_padded, bf16_padded, bf16_padded, bf16_padded, bf16_padded, bf16_padded,
            pltpu.VMEM((len(DILATED_PATTERNS), 2, ATT_BLOCK, 2 * ATT_BLOCK), F32),
            pltpu.VMEM((2, ATT_GROUP, ATT_BLOCK, 2 * ATT_BLOCK), F32),
            pltpu.VMEM((2, ATT_GROUP, ATT_BLOCK, 2 * ATT_BLOCK), BF16),
            f32_rows, f32_rows, f32_rows, f32_rows, f32_rows, f32_rows,
        ],
        compiler_params=pltpu.CompilerParams(
            dimension_semantics=("arbitrary", "arbitrary"), vmem_limit_bytes=VMEM_LIMIT_BYTES),
        name="dilated_attention",
    )(slopes, proj, proj, proj, proj, *to_cast)
    return y_att, cast


def _post_tail(x1, p_ref, pg_ref, wg_ref, wp_ref, o_ref):
    h2 = _rms_rows(x1, pg_ref[...]).astype(BF16)
    pb = p_ref[...].astype(BF16)
    for cc in range(D_MODEL // POST_NC):
        sl = slice(cc * POST_NC, (cc + 1) * POST_NC)
        gate = jax.nn.sigmoid(jnp.dot(h2, wg_ref[:, sl], preferred_element_type=F32))
        pp = jnp.dot(pb, wp_ref[:, sl], preferred_element_type=F32)
        o_ref[:, sl] = x1[:, sl] + gate * pp


def _post_kernel(x_ref, ya_ref, yb_ref, p_ref, wa_ref, wb_ref, pg_ref, wg_ref, wp_ref, o_ref):
    x1 = x_ref[...] + jnp.dot(ya_ref[...], wa_ref[...], preferred_element_type=F32)
    x1 = x1 + jnp.dot(yb_ref[...], wb_ref[...], preferred_element_type=F32)
    _post_tail(x1, p_ref, pg_ref, wg_ref, wp_ref, o_ref)


def _const_spec(shape):
    return pl.BlockSpec(shape, lambda i: (0,) * len(shape), pipeline_mode=pl.Buffered(1))


def _ple_rows_spec(layer, m, tm):
    return pl.BlockSpec((tm, PLE_DIM), lambda i: (layer * (m // tm) + i, 0))


def _post(x2, ya, ya_blk, yb, yb_blk, p_all, layer, w_out_bf, pg, wg_all_bf, wp_all_bf, name):
    m = x2.shape[0]
    tm = POST_TM
    half = D_MODEL // 2
    rows = lambda w, blk=0: pl.BlockSpec((tm, w), lambda i: (i, blk))
    layer_spec = lambda k: pl.BlockSpec((None, k, D_MODEL), lambda i: (layer, 0, 0), pipeline_mode=pl.Buffered(1))
    return pl.pallas_call(
        _post_kernel,
        out_shape=jax.ShapeDtypeStruct((m, D_MODEL), F32),
        grid=(m // tm,),
        in_specs=[
            rows(D_MODEL), rows(half, ya_blk), rows(half, yb_blk), _ple_rows_spec(layer, m, tm),
            pl.BlockSpec((half, D_MODEL), lambda i: (0, 0), pipeline_mode=pl.Buffered(1)),
            pl.BlockSpec((half, D_MODEL), lambda i: (1, 0), pipeline_mode=pl.Buffered(1)),
            _const_spec((1, D_MODEL)), layer_spec(D_MODEL), layer_spec(PLE_DIM),
        ],
        out_specs=rows(D_MODEL),
        compiler_params=pltpu.CompilerParams(
            dimension_semantics=("arbitrary",), vmem_limit_bytes=VMEM_LIMIT_BYTES),
        name=name,
    )(x2, ya, yb, p_all, w_out_bf, w_out_bf, pg, wg_all_bf, wp_all_bf)


def _in_odd_kernel(x_ref, g_ref, wb_ref, wc_ref, wu_ref, wz_ref, cw_ref, y_ref, hn_ref, ext_ref, halo_ref, *,
                   tiles_per_seq):
    i, j = pl.program_id(0), pl.program_id(1)
    tm = x_ref.shape[0]

    @pl.when((i == 0) & (j == 0))
    def _():
        halo_ref[...] = jnp.zeros(halo_ref.shape, F32)

    def tile(with_pre_norm):
        if with_pre_norm:
            hn_ref[...] = _rms_rows(x_ref[...], g_ref[...]).astype(BF16)
        hn = hn_ref[...]
        first = (i % tiles_per_seq) == 0
        for sc in range(y_ref.shape[1] // IN_ODD_NC):
            cols = slice(sc * IN_ODD_NC, (sc + 1) * IN_ODD_NC)
            cu = (jnp.dot(hn, wc_ref[:, cols], preferred_element_type=F32)
                  * jnp.dot(hn, wu_ref[:, cols], preferred_element_type=F32))
            ext_ref[sc, 0:HALO, :] = jnp.where(first, 0.0, halo_ref[j, :, cols])
            ext_ref[sc, HALO:HALO + tm, :] = cu
            halo_ref[j, :, cols] = cu[tm - HALO:, :]
            conv = (cw_ref[0:1, cols] * ext_ref[sc, HALO - 2:HALO - 2 + tm, :]
                    + cw_ref[1:2, cols] * ext_ref[sc, HALO - 1:HALO - 1 + tm, :]
                    + cw_ref[2:3, cols] * cu)
            bg = jnp.dot(hn, wb_ref[:, cols], preferred_element_type=F32)
            z = jnp.dot(hn, wz_ref[:, cols], preferred_element_type=F32)
            y_ref[:, cols] = (bg * conv * _silu(z)).astype(BF16)

    pl.when(j == 0)(functools.partial(tile, True))
    pl.when(j > 0)(functools.partial(tile, False))


def _in_odd(x2, g, w_bf, conv_w, seq):
    m = x2.shape[0]
    tm, tc = IN_TM, IN_ODD_TC
    nb = D_MODEL // tc
    wspec = lambda seg: pl.BlockSpec((D_MODEL, tc), lambda i, j: (0, seg * nb + j))
    return pl.pallas_call(
        functools.partial(_in_odd_kernel, tiles_per_seq=seq // tm),
        out_shape=jax.ShapeDtypeStruct((m, D_MODEL), BF16),
        grid=(m // tm, nb),
        in_specs=[
            pl.BlockSpec((tm, D_MODEL), lambda i, j: (i, 0)),
            pl.BlockSpec((1, D_MODEL), lambda i, j: (0, 0)),
            wspec(0), wspec(1), wspec(2), wspec(3),
            pl.BlockSpec((CONV_WIDTH, tc), lambda i, j: (0, j)),
        ],
        out_specs=pl.BlockSpec((tm, tc), lambda i, j: (i, j)),
        scratch_shapes=[
            pltpu.VMEM((tm, D_MODEL), BF16),
            pltpu.VMEM((tc // IN_ODD_NC, HALO + tm, IN_ODD_NC), F32),
            pltpu.VMEM((nb, HALO, tc), F32),
        ],
        compiler_params=pltpu.CompilerParams(
            dimension_semantics=("arbitrary", "arbitrary"), vmem_limit_bytes=VMEM_LIMIT_BYTES),
        name="in_odd",
    )(x2, g, w_bf, w_bf, w_bf, w_bf, conv_w)


def _even_gain_row(q_g, k_g):
    return jnp.concatenate([
        jnp.ones((OFF_AQ,), F32),
        jnp.tile(q_g * ATT_HD ** -0.5, ATT_HEADS),
        jnp.tile(k_g, ATT_HEADS),
        jnp.ones((IN_WIDTH_EVEN - OFF_Z,), F32),
    ]).reshape(1, IN_WIDTH_EVEN)

def kernel(x, p, pre_norm_g, w_in_even, q_norm_g, k_norm_g, ret_gn_g, w_out_even, w_in_odd, conv_w_odd,
           w_out_odd, ple_norm_g, w_ple_gate, w_ple_proj):
    batch, seq, d = x.shape
    depth = p.shape[0]
    assert d == D_MODEL and seq % (ATT_BLOCK * DILATED_PATTERNS[-1][1]) == 0 and seq % IN_TM == 0
    m = batch * seq
    slopes = jnp.asarray([(2.0 ** (-8.0 / ATT_HEADS)) ** (i + 1) for i in range(ATT_HEADS)], F32)
    x2 = x.reshape(m, d)
    p_all = p.reshape(depth * m, PLE_DIM)
    late_f32 = dict(w_out_even=w_out_even, w_ple_gate=w_ple_gate, w_ple_proj=w_ple_proj, w_in_odd=w_in_odd,
                    w_out_odd=w_out_odd)
    if w_in_even.shape[0] > 1:
        late_f32["w_in_even_rest"] = w_in_even[1:]
    bf = None
    for i in range(depth):
        j = i // 2
        pg = ple_norm_g[i].reshape(1, d)
        g = pre_norm_g[i].reshape(1, d)
        if i % 2 == 0:
            w_in_bf = w_in_even[0].astype(BF16) if j == 0 else bf["w_in_even_rest"][j - 1]
            proj = _in_even(x2, g, w_in_bf, _even_gain_row(q_norm_g[j], k_norm_g[j]))
            y_ret = _retention(proj, ret_gn_g[j], batch, seq)
            to_cast = [a.reshape(-1, a.shape[-1]) for a in late_f32.values()] if bf is None else []
            y_att, cast = _attention(proj, slopes, batch, seq, to_cast)
            if bf is None:
                bf = {name: c.reshape(a.shape) for (name, a), c in zip(late_f32.items(), cast)}
            x2 = _post(x2, y_ret, 0, y_att, 0, p_all, i, bf["w_out_even"][j], pg, bf["w_ple_gate"],
                       bf["w_ple_proj"], "post_even")
        else:
            y = _in_odd(x2, g, bf["w_in_odd"][j], conv_w_odd[j], seq)
            x2 = _post(x2, y, 0, y, 1, p_all, i, bf["w_out_odd"][j], pg, bf["w_ple_gate"], bf["w_ple_proj"],
                       "post_odd")
    return x2.reshape(batch, seq, d)
```

```python
import functools
import math

import jax
import jax.numpy as jnp
from jax import lax
from jax.experimental import pallas as pl
from jax.experimental.pallas import tpu as pltpu

F32 = jnp.float32
BF16 = jnp.bfloat16

D_MODEL = 2048
PLE_DIM = 256
EPS = 1e-6
NEG = -1e30
RET_DK = 128
RET_DV = 256
RET_HEADS = 4
RET_CHUNK = 256
ATT_HD = 128
ATT_HEADS = 8
ATT_BLOCK = 128
DILATED_PATTERNS = ((128, 1), (512, 4), (2048, 16))
RET_QK_WIDTH = RET_HEADS * RET_DK
RET_WIDTH = RET_HEADS * RET_DV
ATT_WIDTH = ATT_HEADS * ATT_HD
IN_WIDTH_EVEN = 2 * RET_QK_WIDTH + RET_WIDTH + 3 * ATT_WIDTH + RET_WIDTH + ATT_WIDTH
CONV_WIDTH = 3

REF_AQ = 2 * RET_QK_WIDTH + RET_WIDTH
REF_AV = REF_AQ + 2 * ATT_WIDTH
OFF_RQ = 0
OFF_RK = RET_QK_WIDTH
OFF_RV = 2 * RET_QK_WIDTH
OFF_AV = OFF_RV + RET_WIDTH
OFF_AQ = OFF_AV + ATT_WIDTH
OFF_AK = OFF_AQ + ATT_WIDTH
OFF_Z = OFF_AK + ATT_WIDTH

VMEM_LIMIT_BYTES = 56 * 1024 * 1024
BF16_ROWS = 16

IN_TM = 1024
IN_EVEN_TN = 1792
IN_EVEN_NC = 256
IN_EVEN_LAST_ROW_SPLIT = 4
IN_ODD_TC = 512
IN_ODD_NC = 256
RET_TQ = 2048
ATT_GROUP = 4
ATT_MERGE_ROWS = 512
POST_TM = 512
POST_NC = 256
HALO = 8


def _rms_rows(x, g):
    ms = jnp.mean(x * x, axis=-1, keepdims=True)
    return x * lax.rsqrt(ms + EPS) * g


def _silu(z):
    return z * jax.nn.sigmoid(z)


def _in_even_kernel(x_ref, g_ref, gain_ref, *refs):
    *w_refs, o_ref, hn_ref = refs
    j = pl.program_id(1)

    @pl.when(j == 0)
    def _():
        hn_ref[...] = _rms_rows(x_ref[...], g_ref[...]).astype(BF16)

    tm, tn = o_ref.shape
    n_chunks = tn // IN_EVEN_NC

    def tile(jj):
        for cc in range(n_chunks):
            col0 = jj * tn + cc * IN_EVEN_NC
            n_row = IN_EVEN_LAST_ROW_SPLIT if cc == n_chunks - 1 else 1
            for rr in range(n_row):
                rows = slice(rr * (tm // n_row), (rr + 1) * (tm // n_row))
                acc = jnp.dot(hn_ref[rows, :], w_refs[cc][...], preferred_element_type=F32)
                for hh in range(IN_EVEN_NC // ATT_HD):
                    sl = slice(cc * IN_EVEN_NC + hh * ATT_HD, cc * IN_EVEN_NC + (hh + 1) * ATT_HD)
                    a = acc[:, hh * ATT_HD:(hh + 1) * ATT_HD]
                    if col0 >= OFF_Z:
                        a = _silu(a)
                    elif col0 >= OFF_AQ:
                        a = _rms_rows(a, gain_ref[:, sl])
                    o_ref[rows, sl] = a.astype(BF16)

    for jj in range(IN_WIDTH_EVEN // tn):
        pl.when(j == jj)(functools.partial(tile, jj))


def _even_weight_chunk_spec(cc, chunks_per_tile):
    nc = IN_EVEN_NC

    def index(i, j):
        k = j * chunks_per_tile + cc
        in_av = (k >= OFF_AV // nc) & (k < OFF_AQ // nc)
        in_qk = (k >= OFF_AQ // nc) & (k < OFF_Z // nc)
        ref = jnp.where(in_av, k + (REF_AV - OFF_AV) // nc, jnp.where(in_qk, k - (OFF_AQ - REF_AQ) // nc, k))
        return (0, ref)

    return pl.BlockSpec((D_MODEL, nc), index)


def _in_even(x2, g, w_bf, gain_row):
    m = x2.shape[0]
    tm, tn = IN_TM, IN_EVEN_TN
    n_chunks = tn // IN_EVEN_NC
    return pl.pallas_call(
        _in_even_kernel,
        out_shape=jax.ShapeDtypeStruct((m, IN_WIDTH_EVEN), BF16),
        grid=(m // tm, IN_WIDTH_EVEN // tn),
        in_specs=[
            pl.BlockSpec((tm, D_MODEL), lambda i, j: (i, 0)),
            pl.BlockSpec((1, D_MODEL), lambda i, j: (0, 0)),
            pl.BlockSpec((1, tn), lambda i, j: (0, j)),
            *[_even_weight_chunk_spec(cc, n_chunks) for cc in range(n_chunks)],
        ],
        out_specs=pl.BlockSpec((tm, tn), lambda i, j: (i, j)),
        scratch_shapes=[pltpu.VMEM((tm, D_MODEL), BF16)],
        compiler_params=pltpu.CompilerParams(
            dimension_semantics=("arbitrary", "arbitrary"), vmem_limit_bytes=VMEM_LIMIT_BYTES),
        name="in_even",
    )(x2, g, gain_row, *([w_bf] * n_chunks))


def _ret_kernel(q_ref, k_ref, v_ref, z_ref, gn_ref, o_ref, r_ref, decay_ref):
    c = RET_CHUNK
    tq = q_ref.shape[0]
    kscale = RET_DK ** -0.5
    log_g = [math.log(1.0 - 2.0 ** (-5.0 - h)) for h in range(RET_HEADS)]

    @pl.when(pl.program_id(1) == 0)
    def _():
        r_ref[...] = jnp.zeros(r_ref.shape, F32)

    @pl.when((pl.program_id(0) == 0) & (pl.program_id(1) == 0))
    def _():
        row = lax.broadcasted_iota(jnp.int32, (c, c), 0)
        col = lax.broadcasted_iota(jnp.int32, (c, c), 1)
        diff = (row - col).astype(F32)
        for h in range(RET_HEADS):
            decay_ref[h] = jnp.where(diff >= 0, jnp.exp(log_g[h] * jnp.maximum(diff, 0.0)), 0.0) * kscale

    pos = lax.broadcasted_iota(jnp.int32, (c, 1), 0).astype(F32)

    for h in range(RET_HEADS):
        lg = log_g[h]
        decay = decay_ref[h]
        xi = jnp.exp(lg * (pos + 1.0))
        zeta = jnp.exp(lg * (c - 1.0 - pos)) * kscale
        g_chunk = math.exp(lg * c)
        qk_sl = slice(h * RET_DK, (h + 1) * RET_DK)
        v_sl = slice(h * RET_DV, (h + 1) * RET_DV)
        gn = gn_ref[h:h + 1, :]
        chunks = [slice(ci * c, (ci + 1) * c) for ci in range(tq // c)]

        contrib = []
        for rows in chunks:
            kz_t = (k_ref[rows, qk_sl].astype(F32) * zeta).T.astype(BF16)
            contrib.append(jnp.dot(kz_t, v_ref[rows, v_sl], preferred_element_type=F32))

        state = r_ref[h]
        states = []
        for s_c in contrib:
            states.append(state)
            state = g_chunk * state + s_c
        r_ref[h] = state

        for rows, r_prev in zip(chunks, states):
            q = q_ref[rows, qk_sl]
            v = v_ref[rows, v_sl]
            s = lax.dot_general(q, k_ref[rows, qk_sl], (((1,), (1,)), ((), ())),
                                preferred_element_type=F32) * decay
            y = jnp.dot(s.astype(BF16), v, preferred_element_type=F32)
            y = y + jnp.dot(q, r_prev.astype(BF16), preferred_element_type=F32) * xi
            mu = jnp.mean(y, axis=-1, keepdims=True)
            yc = y - mu
            var = jnp.mean(yc * yc, axis=-1, keepdims=True)
            yn = yc * lax.rsqrt(var + EPS) * gn
            o_ref[rows, v_sl] = (yn * z_ref[rows, v_sl].astype(F32)).astype(BF16)


def _retention(proj, gn_g, batch, seq):
    m = proj.shape[0]
    tq = RET_TQ
    nt = seq // tq
    row = lambda b, t: b * nt + t
    return pl.pallas_call(
        _ret_kernel,
        out_shape=jax.ShapeDtypeStruct((m, RET_WIDTH), BF16),
        grid=(batch, nt),
        in_specs=[
            pl.BlockSpec((tq, RET_QK_WIDTH), lambda b, t: (row(b, t), OFF_RQ // RET_QK_WIDTH)),
            pl.BlockSpec((tq, RET_QK_WIDTH), lambda b, t: (row(b, t), OFF_RK // RET_QK_WIDTH)),
            pl.BlockSpec((tq, RET_WIDTH), lambda b, t: (row(b, t), OFF_RV // RET_WIDTH)),
            pl.BlockSpec((tq, RET_WIDTH), lambda b, t: (row(b, t), OFF_Z // RET_WIDTH)),
            pl.BlockSpec((RET_HEADS, RET_DV), lambda b, t: (0, 0)),
        ],
        out_specs=pl.BlockSpec((tq, RET_WIDTH), lambda b, t: (row(b, t), 0)),
        scratch_shapes=[pltpu.VMEM((RET_HEADS, RET_DK, RET_DV), F32),
                        pltpu.VMEM((RET_HEADS, RET_CHUNK, RET_CHUNK), F32)],
        compiler_params=pltpu.CompilerParams(
            dimension_semantics=("arbitrary", "arbitrary"), vmem_limit_bytes=VMEM_LIMIT_BYTES),
        name="retention",
    )(proj, proj, proj, proj, gn_g)


def _att_kernel(*refs, n_cast):
    slopes_ref, q_ref, k_ref, v_ref, z_ref = refs[:5]
    cast_in, o_ref, cast_out = refs[5:5 + n_cast], refs[5 + n_cast], refs[6 + n_cast:6 + 2 * n_cast]
    (stage_q, stage_k, stage_v, q4, k4, v4, qp4, qp16, kp4, kp16, vp4, vp16, bias_ref, s_ref, pn_ref,
     o0, o1, o2, l0, l1, l2) = refs[6 + 2 * n_cast:]
    for src, dst in zip(cast_in, cast_out):
        dst[...] = src[...].astype(BF16)
    blk = ATT_BLOCK
    grp = ATT_GROUP
    seq = q_ref.shape[0]
    nblk = seq // blk
    n_it = nblk // grp
    d4, d16 = DILATED_PATTERNS[1][1], DILATED_PATTERNS[2][1]
    assert DILATED_PATTERNS[0][1] == 1 and grp == d4 and d16 == d4 * d4
    assert n_it % 2 == 0 and n_it >= 4 and nblk % d16 == 0
    slope = slopes_ref[pl.program_id(1)]
    nt = (((1,), (1,)), ((), ()))

    def natural_rows(bi, dil):
        nb = nblk // dil
        start = (bi // nb) + (bi % nb) * (blk * dil)
        if dil == 1:
            return pl.ds(pl.multiple_of(start, blk), blk)
        return pl.ds(start, blk, stride=dil)

    def block_rows(bi, lead=0):
        return pl.ds(pl.multiple_of(bi * blk + lead, blk), blk)

    for pad in (kp4, kp16, vp4, vp16):
        pad[0:blk, :] = jnp.zeros((blk, ATT_HD), BF16)

    def band_rows(bi, padded):
        start = bi * blk if padded else (max(bi - 1, 0) if isinstance(bi, int) else jnp.maximum(bi - 1, 0)) * blk
        return pl.ds(pl.multiple_of(start, blk), 2 * blk)

    def copy_4(it):
        rows = pl.ds(pl.multiple_of(it * (grp * blk), grp * blk), grp * blk)
        stage_q[...] = q_ref[rows, :].astype(F32)
        stage_k[...] = k_ref[rows, :].astype(F32)
        stage_v[...] = v_ref[rows, :].astype(F32)
        for r in range(d4):
            bi = r * (nblk // d4) + it
            src = pl.ds(r, blk, stride=d4)
            q, k, v = stage_q[src, :], stage_k[src, :], stage_v[src, :]
            q4[block_rows(bi), :], k4[block_rows(bi), :], v4[block_rows(bi), :] = q, k, v
            qp4[block_rows(bi), :] = q.astype(BF16)
            kp4[block_rows(bi, blk), :] = k.astype(BF16)
            vp4[block_rows(bi, blk), :] = v.astype(BF16)

    def copy_16(it):
        halves = (nblk // d4) // grp
        r4, n16 = it // halves, it % halves
        for c in range(d4):
            bi = (r4 + d4 * c) * (nblk // d16) + n16
            src = pl.ds(r4 * (seq // d4) + n16 * (grp * blk) + c, blk, stride=d4)
            qp16[block_rows(bi), :] = q4[src, :].astype(BF16)
            kp16[block_rows(bi, blk), :] = k4[src, :].astype(BF16)
            vp16[block_rows(bi, blk), :] = v4[src, :].astype(BF16)

    qi = lax.broadcasted_iota(jnp.int32, (blk, 2 * blk), 0)
    kj = lax.broadcasted_iota(jnp.int32, (blk, 2 * blk), 1)
    rel = qi - kj + blk
    for pi, (window, dil) in enumerate(DILATED_PATTERNS):
        valid = (rel >= 0) & (rel <= window // dil)
        alibi = -slope * (rel * dil).astype(F32)
        if dil == 1:
            own = qi - kj
            bias_ref[pi, 0] = jnp.where((own >= 0) & (kj < blk), -slope * own.astype(F32), NEG)
        else:
            bias_ref[pi, 0] = jnp.where(valid & (kj >= blk), alibi, NEG)
        bias_ref[pi, 1] = jnp.where(valid, alibi, NEG)

    o16p, l16p = q4, k4

    def rows_in_d4_order(bi16):
        r16, n16 = bi16 // (nblk // d16), bi16 % (nblk // d16)
        return pl.ds((r16 % d4) * (seq // d4) + n16 * (grp * blk) + r16 // d4, blk, stride=d4)

    def un_permute(it):
        for g in range(grp):
            bi4 = it * grp + g
            o2[natural_rows(bi4, d4), :] = o16p[block_rows(bi4), :]
            l2[natural_rows(bi4, d4), :] = l16p[block_rows(bi4), :]

    def make_stages(pi, dil, q_src, kp, vp, o_g, l_g, copy_next):
        nb = nblk // dil
        two_step = dil == d16
        block_of = (lambda it, g: it + n_it * g) if two_step else (lambda it, g: it * grp + g)
        out_rows = rows_in_d4_order if two_step else (lambda bi: natural_rows(bi, dil))
        o_dst, l_dst = (o16p, l16p) if two_step else (o_g, l_g)

        def scores(it):
            for g in range(grp):
                bi = block_of(it, g)
                kb = kp[band_rows(bi, dil != 1), :]
                s_ref[it % 2, g] = lax.dot_general(q_src[block_rows(bi), :], kb, nt, preferred_element_type=F32)
            if copy_next is not None:
                copy_next(it)

        def probs(it):
            for g in range(grp):
                bi = block_of(it, g)
                s = s_ref[it % 2, g] + bias_ref[pi, jnp.minimum(bi % nb, 1)]
                mx = jnp.max(jnp.maximum(s[:, :blk], s[:, blk:]), axis=-1, keepdims=True)
                pr = jnp.exp(s - mx)
                den = jnp.sum(pr[:, :blk] + pr[:, blk:], axis=-1, keepdims=True)
                pn_ref[it % 2, g] = (pr * (1.0 / den)).astype(BF16)
                l_dst[out_rows(bi), :] = jnp.broadcast_to(mx + jnp.log(den), (blk, ATT_HD))

        def values(it):
            for g in range(grp):
                bi = block_of(it, g)
                vb = vp[band_rows(bi, dil != 1), :]
                o_dst[out_rows(bi), :] = jnp.dot(pn_ref[it % 2, g], vb, preferred_element_type=F32)
            if two_step:
                un_permute(it)

        return scores, probs, values

    stages = [make_stages(pi, dil, *refs) for pi, ((_, dil), *refs) in enumerate(zip(
        DILATED_PATTERNS, (q_ref, qp4, qp16), (k_ref, kp4, kp16), (v_ref, vp4, vp16), (o0, o1, o2), (l0, l1, l2),
        (copy_4, copy_16, None)))]

    def fill(scores, probs, values):
        scores(0)
        probs(0)
        scores(1)

    def drain(scores, probs, values):
        values(n_it - 2)
        probs(n_it - 1)
        values(n_it - 1)

    fill(*stages[0])
    for pi, (scores, probs, values) in enumerate(stages):
        def body(it, carry, scores=scores, probs=probs, values=values):
            values(it - 2)
            probs(it - 1)
            scores(it)
            return carry

        lax.fori_loop(2, n_it, body, 0)
        drain(scores, probs, values)
        if pi + 1 < len(stages):
            fill(*stages[pi + 1])

    mrows = ATT_MERGE_ROWS

    def merge(ci, carry):
        rows = pl.ds(pl.multiple_of(ci * mrows, mrows), mrows)
        a0, a1, a2 = l0[rows, :], l1[rows, :], l2[rows, :]
        mx = jnp.maximum(jnp.maximum(a0, a1), a2)
        e0, e1, e2 = jnp.exp(a0 - mx), jnp.exp(a1 - mx), jnp.exp(a2 - mx)
        y = (e0 * o0[rows, :] + e1 * o1[rows, :] + e2 * o2[rows, :]) / (e0 + e1 + e2)
        o_ref[rows, :] = (y * z_ref[rows, :].astype(F32)).astype(BF16)
        return carry

    lax.fori_loop(0, seq // mrows, merge, 0)


def _attention(proj, slopes, batch, seq, to_cast):
    m = proj.shape[0]
    hd = ATT_HD
    n_steps = batch * ATT_HEADS
    spec = lambda off: pl.BlockSpec((seq, hd), lambda b, h: (b, off // hd + h))
    slab = lambda a: pl.BlockSpec((a.shape[0] // n_steps, a.shape[1]), lambda b, h: (b * ATT_HEADS + h, 0))
    assert all(a.shape[0] % (BF16_ROWS * n_steps) == 0 for a in to_cast)
    f32_rows = pltpu.VMEM((seq, hd), F32)
    bf16_rows = pltpu.VMEM((seq, hd), BF16)
    bf16_padded = pltpu.VMEM((seq + ATT_BLOCK, hd), BF16)
    f32_group = pltpu.VMEM((ATT_GROUP * ATT_BLOCK, hd), F32)
    y_att, *cast = pl.pallas_call(
        functools.partial(_att_kernel, n_cast=len(to_cast)),
        out_shape=[jax.ShapeDtypeStruct((m, ATT_WIDTH), BF16)]
        + [jax.ShapeDtypeStruct(a.shape, BF16) for a in to_cast],
        grid=(batch, ATT_HEADS),
        in_specs=[
            pl.BlockSpec(memory_space=pltpu.SMEM),
            spec(OFF_AQ), spec(OFF_AK), spec(OFF_AV), spec(OFF_Z + RET_WIDTH),
        ] + [slab(a) for a in to_cast],
        out_specs=[pl.BlockSpec((seq, hd), lambda b, h: (b, h))] + [slab(a) for a in to_cast],
        scratch_shapes=[
            f32_group, f32_group, f32_group, f32_rows, f32_rows, f32_rows,
            bf16_rows, bf16_rows,
            bf16_padded, bf16_padded, bf16_padded, bf16_padded,
            pltpu.VMEM((len(DILATED_PATTERNS), 2, ATT_BLOCK, 2 * ATT_BLOCK), F32),
            pltpu.VMEM((2, ATT_GROUP, ATT_BLOCK, 2 * ATT_BLOCK), F32),
            pltpu.VMEM((2, ATT_GROUP, ATT_BLOCK, 2 * ATT_BLOCK), BF16),
            f32_rows, f32_rows, f32_rows, f32_rows, f32_rows, f32_rows,
        ],
        compiler_params=pltpu.CompilerParams(
            dimension_semantics=("arbitrary", "arbitrary"), vmem_limit_bytes=VMEM_LIMIT_BYTES),
        name="dilated_attention",
    )(slopes, proj, proj, proj, proj, *to_cast)
    return y_att, cast


def _post_tail(x1, p_ref, pg_ref, wg_ref, wp_ref, o_ref):
    h2 = _rms_rows(x1, pg_ref[...]).astype(BF16)
    pb = p_ref[...].astype(BF16)
    for cc in range(D_MODEL // POST_NC):
        sl = slice(cc * POST_NC, (cc + 1) * POST_NC)
        gate = jax.nn.sigmoid(jnp.dot(h2, wg_ref[:, sl], preferred_element_type=F32))
        pp = jnp.dot(pb, wp_ref[:, sl], preferred_element_type=F32)
        o_ref[:, sl] = x1[:, sl] + gate * pp


def _post_kernel(x_ref, ya_ref, yb_ref, p_ref, wa_ref, wb_ref, pg_ref, wg_ref, wp_ref, o_ref):
    x1 = x_ref[...] + jnp.dot(ya_ref[...], wa_ref[...], preferred_element_type=F32)
    x1 = x1 + jnp.dot(yb_ref[...], wb_ref[...], preferred_element_type=F32)
    _post_tail(x1, p_ref, pg_ref, wg_ref, wp_ref, o_ref)


def _const_spec(shape):
    return pl.BlockSpec(shape, lambda i: (0,) * len(shape), pipeline_mode=pl.Buffered(1))


def _ple_rows_spec(layer, m, tm):
    return pl.BlockSpec((tm, PLE_DIM), lambda i: (layer * (m // tm) + i, 0))


def _post(x2, ya, ya_blk, yb, yb_blk, p_all, layer, w_out_bf, pg, wg_all_bf, wp_all_bf, name):
    m = x2.shape[0]
    tm = POST_TM
    half = D_MODEL // 2
    rows = lambda w, blk=0: pl.BlockSpec((tm, w), lambda i: (i, blk))
    layer_spec = lambda k: pl.BlockSpec((None, k, D_MODEL), lambda i: (layer, 0, 0), pipeline_mode=pl.Buffered(1))
    return pl.pallas_call(
        _post_kernel,
        out_shape=jax.ShapeDtypeStruct((m, D_MODEL), F32),
        grid=(m // tm,),
        in_specs=[
            rows(D_MODEL), rows(half, ya_blk), rows(half, yb_blk), _ple_rows_spec(layer, m, tm),
            pl.BlockSpec((half, D_MODEL), lambda i: (0, 0), pipeline_mode=pl.Buffered(1)),
            pl.BlockSpec((half, D_MODEL), lambda i: (1, 0), pipeline_mode=pl.Buffered(1)),
            _const_spec((1, D_MODEL)), layer_spec(D_MODEL), layer_spec(PLE_DIM),
        ],
        out_specs=rows(D_MODEL),
        compiler_params=pltpu.CompilerParams(
            dimension_semantics=("arbitrary",), vmem_limit_bytes=VMEM_LIMIT_BYTES),
        name=name,
    )(x2, ya, yb, p_all, w_out_bf, w_out_bf, pg, wg_all_bf, wp_all_bf)


def _in_odd_kernel(x_ref, g_ref, wb_ref, wc_ref, wu_ref, wz_ref, cw_ref, y_ref, hn_ref, ext_ref, halo_ref, *,
                   tiles_per_seq):
    i, j = pl.program_id(0), pl.program_id(1)
    tm = x_ref.shape[0]

    @pl.when((i == 0) & (j == 0))
    def _():
        halo_ref[...] = jnp.zeros(halo_ref.shape, F32)

    def tile(with_pre_norm):
        if with_pre_norm:
            hn_ref[...] = _rms_rows(x_ref[...], g_ref[...]).astype(BF16)
        hn = hn_ref[...]
        first = (i % tiles_per_seq) == 0
        for sc in range(y_ref.shape[1] // IN_ODD_NC):
            cols = slice(sc * IN_ODD_NC, (sc + 1) * IN_ODD_NC)
            cu = (jnp.dot(hn, wc_ref[:, cols], preferred_element_type=F32)
                  * jnp.dot(hn, wu_ref[:, cols], preferred_element_type=F32))
            ext_ref[sc, 0:HALO, :] = jnp.where(first, 0.0, halo_ref[j, :, cols])
            ext_ref[sc, HALO:HALO + tm, :] = cu
            halo_ref[j, :, cols] = cu[tm - HALO:, :]
            conv = (cw_ref[0:1, cols] * ext_ref[sc, HALO - 2:HALO - 2 + tm, :]
                    + cw_ref[1:2, cols] * ext_ref[sc, HALO - 1:HALO - 1 + tm, :]
                    + cw_ref[2:3, cols] * cu)
            bg = jnp.dot(hn, wb_ref[:, cols], preferred_element_type=F32)
            z = jnp.dot(hn, wz_ref[:, cols], preferred_element_type=F32)
            y_ref[:, cols] = (bg * conv * _silu(z)).astype(BF16)

    pl.when(j == 0)(functools.partial(tile, True))
    pl.when(j > 0)(functools.partial(tile, False))


def _in_odd(x2, g, w_bf, conv_w, seq):
    m = x2.shape[0]
    tm, tc = IN_TM, IN_ODD_TC
    nb = D_MODEL // tc
    wspec = lambda seg: pl.BlockSpec((D_MODEL, tc), lambda i, j: (0, seg * nb + j))
    return pl.pallas_call(
        functools.partial(_in_odd_kernel, tiles_per_seq=seq // tm),
        out_shape=jax.ShapeDtypeStruct((m, D_MODEL), BF16),
        grid=(m // tm, nb),
        in_specs=[
            pl.BlockSpec((tm, D_MODEL), lambda i, j: (i, 0)),
            pl.BlockSpec((1, D_MODEL), lambda i, j: (0, 0)),
            wspec(0), wspec(1), wspec(2), wspec(3),
            pl.BlockSpec((CONV_WIDTH, tc), lambda i, j: (0, j)),
        ],
        out_specs=pl.BlockSpec((tm, tc), lambda i, j: (i, j)),
        scratch_shapes=[
            pltpu.VMEM((tm, D_MODEL), BF16),
            pltpu.VMEM((tc // IN_ODD_NC, HALO + tm, IN_ODD_NC), F32),
            pltpu.VMEM((nb, HALO, tc), F32),
        ],
        compiler_params=pltpu.CompilerParams(
            dimension_semantics=("arbitrary", "arbitrary"), vmem_limit_bytes=VMEM_LIMIT_BYTES),
        name="in_odd",
    )(x2, g, w_bf, w_bf, w_bf, w_bf, conv_w)


def _even_gain_row(q_g, k_g):
    return jnp.concatenate([
        jnp.ones((OFF_AQ,), F32),
        jnp.tile(q_g * ATT_HD ** -0.5, ATT_HEADS),
        jnp.tile(k_g, ATT_HEADS),
        jnp.ones((IN_WIDTH_EVEN - OFF_Z,), F32),
    ]).reshape(1, IN_WIDTH_EVEN)

def kernel(x, p, pre_norm_g, w_in_even, q_norm_g, k_norm_g, ret_gn_g, w_out_even, w_in_odd, conv_w_odd,
           w_out_odd, ple_norm_g, w_ple_gate, w_ple_proj):
    batch, seq, d = x.shape
    depth = p.shape[0]
    assert d == D_MODEL and seq % (ATT_BLOCK * DILATED_PATTERNS[-1][1]) == 0 and seq % IN_TM == 0
    m = batch * seq
    slopes = jnp.asarray([(2.0 ** (-8.0 / ATT_HEADS)) ** (i + 1) for i in range(ATT_HEADS)], F32)
    x2 = x.reshape(m, d)
    p_all = p.reshape(depth * m, PLE_DIM)
    late_f32 = dict(w_out_even=w_out_even, w_ple_gate=w_ple_gate, w_ple_proj=w_ple_proj, w_in_odd=w_in_odd,
                    w_out_odd=w_out_odd)
    if w_in_even.shape[0] > 1:
        late_f32["w_in_even_rest"] = w_in_even[1:]
    bf = None
    for i in range(depth):
        j = i // 2
        pg = ple_norm_g[i].reshape(1, d)
        g = pre_norm_g[i].reshape(1, d)
        if i % 2 == 0:
            w_in_bf = w_in_even[0].astype(BF16) if j == 0 else bf["w_in_even_rest"][j - 1]
            proj = _in_even(x2, g, w_in_bf, _even_gain_row(q_norm_g[j], k_norm_g[j]))
            y_ret = _retention(proj, ret_gn_g[j], batch, seq)
            to_cast = [a.reshape(-1, a.shape[-1]) for a in late_f32.values()] if bf is None else []
            y_att, cast = _attention(proj, slopes, batch, seq, to_cast)
            if bf is None:
                bf = {name: c.reshape(a.shape) for (name, a), c in zip(late_f32.items(), cast)}
            x2 = _post(x2, y_ret, 0, y_att, 0, p_all, i, bf["w_out_even"][j], pg, bf["w_ple_gate"],
                       bf["w_ple_proj"], "post_even")
        else:
            y = _in_odd(x2, g, bf["w_in_odd"][j], conv_w_odd[j], seq)
            x2 = _post(x2, y, 0, y, 1, p_all, i, bf["w_out_odd"][j], pg, bf["w_ple_gate"], bf["w_ple_proj"],
                       "post_odd")
    return x2.reshape(batch, seq, d)
```
